```python
import functools
import jax, jax.numpy as jnp
from jax import lax
import numpy as np

D_MODEL = 4096
BATCH = 4
SEQ = 2048
DEPTH = 2
DEC_BATCH = 32
DEC_SEQ = 1
PAST_LEN = 16384
PAGE_SIZE = 128

EPS = 1e-6
SWA_HD = 64
SWA_HQ = (D_MODEL // 2) // SWA_HD
SWA_KV = SWA_HQ // 8
SWA_G = SWA_HQ // SWA_KV
SWA_W = SWA_HQ * SWA_HD
SWA_KVW = SWA_KV * SWA_HD
WINDOW = 128
ROPE_THETA = 10000.0
GLA_H = 4
GLA_W = D_MODEL // 4
GLA_DV = GLA_W // GLA_H
GLA_DK = GLA_DV // 2
GLA_KW = GLA_H * GLA_DK
GLA_RANK = 16
GLA_TAU = 16.0
GLA_CHUNK = 64
MEM_LEN = 256
MEM_H = 4
MEM_W = D_MODEL // 4
MEM_HD = MEM_W // MEM_H
IN_SIZES = (SWA_W, SWA_KVW, SWA_KVW, SWA_W, GLA_KW, GLA_KW, GLA_W, GLA_RANK, GLA_W, MEM_W, MEM_W)
N_IN = int(sum(IN_SIZES))
SPLIT_IDX = [int(i) for i in np.cumsum(IN_SIZES)[:-1]]

kernel_name = "hybrid_swa_sink_gla_memxattn_step"


def rmsnorm(x, g):
    xf = x.astype(jnp.float32)
    y = xf * lax.rsqrt(jnp.mean(xf * xf, axis=-1, keepdims=True) + EPS)
    return (y * g.astype(jnp.float32)).astype(x.dtype)


def rope(x, pos):
    half = x.shape[-1] // 2
    inv = ROPE_THETA ** (-jnp.arange(half, dtype=jnp.float32) / half)
    ang = pos.astype(jnp.float32)[:, None] * inv[None, :]
    cos, sin = jnp.cos(ang)[:, None, :], jnp.sin(ang)[:, None, :]
    xf = x.astype(jnp.float32)
    x1, x2 = xf[..., :half], xf[..., half:]
    return jnp.concatenate([x1 * cos - x2 * sin, x2 * cos + x1 * sin], axis=-1).astype(x.dtype)


def sink_softmax(s, mask, sinks):
    sk = sinks.astype(jnp.float32).reshape(SWA_KV, SWA_G, 1, 1)
    s = jnp.where(mask, s, -jnp.inf)
    m = jnp.maximum(jnp.max(s, axis=-1, keepdims=True), sk)
    p = jnp.exp(s - m)
    return p / (jnp.sum(p, axis=-1, keepdims=True) + jnp.exp(sk - m))


def swa_prompt(q, k, v, sinks):
    B, T = q.shape[:2]
    nb = T // WINDOW
    qb = q.reshape(B, nb, WINDOW, SWA_KV, SWA_G, SWA_HD)

    def band(t):
        tb = t.reshape(B, nb, WINDOW, SWA_KV, SWA_HD)
        prev = jnp.concatenate([jnp.zeros_like(tb[:, :1]), tb[:, :-1]], axis=1)
        return jnp.concatenate([prev, tb], axis=2)

    kw, vw = band(k), band(v)
    s = jnp.einsum('bnqkgd,bnskd->bnkgqs', qb, kw).astype(jnp.float32) * (SWA_HD ** -0.5)
    qi = jnp.arange(WINDOW)[:, None]
    sj = jnp.arange(2 * WINDOW)[None, :]
    diff = qi + WINDOW - sj
    blk = jnp.arange(nb)[:, None, None]
    mask = (diff >= 0) & (diff < WINDOW) & ((blk > 0) | (sj >= WINDOW))
    p = sink_softmax(s, mask[:, None, None], sinks)
    o = jnp.einsum('bnkgqs,bnskd->bnqkgd', p.astype(vw.dtype), vw).reshape(B, T, SWA_W)
    return o, k[:, T - WINDOW:], v[:, T - WINDOW:]


def swa_sample(q, k, v, sinks, kbuf, vbuf):
    B, T = q.shape[:2]
    WB = kbuf.shape[1]
    kall = jnp.concatenate([kbuf, k.astype(kbuf.dtype)], axis=1)
    vall = jnp.concatenate([vbuf, v.astype(vbuf.dtype)], axis=1)
    qpos = PAST_LEN + jnp.arange(T)
    kpos = PAST_LEN - WB + jnp.arange(WB + T)
    d = qpos[:, None] - kpos[None, :]
    mask = (d >= 0) & (d < WINDOW)
    qg = q.reshape(B, T, SWA_KV, SWA_G, SWA_HD)
    s = jnp.einsum('btkgd,bskd->bkgts', qg, kall.astype(q.dtype)).astype(jnp.float32) * (SWA_HD ** -0.5)
    p = sink_softmax(s, mask, sinks)
    o = jnp.einsum('bkgts,bskd->btkgd', p.astype(q.dtype), vall.astype(q.dtype)).reshape(B, T, SWA_W)
    return o, kall[:, -WB:], vall[:, -WB:]


def gla(q, k, v, g, s0):
    B, T = q.shape[:2]
    C = GLA_CHUNK if T >= GLA_CHUNK else T
    pad = (-T) % C
    n = (T + pad) // C

    def prep(t):
        t = jnp.pad(t.astype(jnp.float32), ((0, 0), (0, pad), (0, 0), (0, 0)))
        return jnp.moveaxis(t.reshape(B, n, C, *t.shape[2:]), 1, 0)

    qs, ks, vs, gs = prep(q * (GLA_DK ** -0.5)), prep(k), prep(v), prep(g)
    tri = jnp.tril(jnp.ones((C, C), dtype=bool))[None, :, :, None, None]

    def step(S, inp):
        qc, kc, vc, gc = inp
        b = jnp.cumsum(gc, axis=1)
        o_inter = jnp.einsum('bihk,bhkv->bihv', qc * jnp.exp(b), S)
        decay = jnp.exp(jnp.where(tri, b[:, :, None] - b[:, None, :], -jnp.inf))
        a = jnp.einsum('bihk,bjhk,bijhk->bijh', qc, kc, decay)
        o_intra = jnp.einsum('bijh,bjhv->bihv', a, vc)
        bl = b[:, -1]
        S = S * jnp.exp(bl)[..., None] + jnp.einsum('bjhk,bjhv->bhkv', kc * jnp.exp(bl[:, None] - b), vc)
        return S, o_inter + o_intra

    S, o = lax.scan(step, s0.astype(jnp.float32), (qs, ks, vs, gs))
    o = jnp.moveaxis(o, 0, 1).reshape(B, n * C, GLA_H, GLA_DV)[:, :T]
    return o, S.astype(s0.dtype)


def mem_kv(mem, g, w):
    B, M, _ = mem.shape
    h = rmsnorm(mem, g) @ w
    mk, mv = jnp.split(h, 2, axis=-1)
    return mk.reshape(B, M, MEM_H, MEM_HD), mv.reshape(B, M, MEM_H, MEM_HD)


def mem_attn(q, mk, mv):
    B, T = q.shape[:2]
    s = jnp.einsum('bthd,bmhd->bhtm', q, mk.astype(q.dtype)).astype(jnp.float32) * (MEM_HD ** -0.5)
    p = jax.nn.softmax(s, axis=-1)
    return jnp.einsum('bhtm,bmhd->bthd', p.astype(q.dtype), mv.astype(q.dtype)).reshape(B, T, MEM_W)


def mixer_layer(x, pos, mk, mv, swa_fn, gla_s0, norm_g, w_in, sinks, gla_w_gate, gla_b_gate, gla_norm_g, w_out):
    B, T, _ = x.shape
    h = rmsnorm(x, norm_g) @ w_in
    sq, sk, sv, sgate, gq, gk, gv, glr, ggate, mq, mgate = jnp.split(h, SPLIT_IDX, axis=-1)
    sq = rope(sq.reshape(B, T, SWA_HQ, SWA_HD), pos)
    sk = rope(sk.reshape(B, T, SWA_KV, SWA_HD), pos)
    sv = sv.reshape(B, T, SWA_KV, SWA_HD)
    o_swa, kbuf, vbuf = swa_fn(sq, sk, sv, sinks)
    glog = jax.nn.log_sigmoid((glr @ gla_w_gate + gla_b_gate).astype(jnp.float32)) / GLA_TAU
    o_gla, S = gla(gq.reshape(B, T, GLA_H, GLA_DK), gk.reshape(B, T, GLA_H, GLA_DK),
                   gv.reshape(B, T, GLA_H, GLA_DV), glog.reshape(B, T, GLA_H, GLA_DK), gla_s0)
    o_gla = rmsnorm(o_gla.astype(x.dtype), gla_norm_g.reshape(GLA_H, GLA_DV)).reshape(B, T, GLA_W)
    o_mem = mem_attn(mq.reshape(B, T, MEM_H, MEM_HD), mk, mv)
    y = jnp.concatenate([o_swa * jax.nn.silu(sgate), o_gla * jax.nn.silu(ggate),
                         o_mem * jax.nn.silu(mgate)], axis=-1) @ w_out
    return x + y, kbuf, vbuf, S


def setup_inputs(seed: int = 0) -> dict:
    key = jax.random.key(seed)
    ks = jax.random.split(key, 20)
    f32 = jnp.float32
    nrm = lambda k, shape, sc: jax.random.normal(k, shape, f32) * sc
    wb = min(WINDOW, PAST_LEN)
    return {
        "x_prompt": nrm(ks[0], (BATCH, SEQ, D_MODEL), 1.0),
        "mem_prompt": nrm(ks[1], (BATCH, MEM_LEN, D_MODEL), 1.0),
        "x_sample": nrm(ks[2], (DEC_BATCH, DEC_SEQ, D_MODEL), 1.0),
        "cache_swa_k": nrm(ks[3], (DEPTH, DEC_BATCH, wb, SWA_KV, SWA_HD), 1.0),
        "cache_swa_v": nrm(ks[4], (DEPTH, DEC_BATCH, wb, SWA_KV, SWA_HD), 1.0),
        "state_gla": nrm(ks[5], (DEPTH, DEC_BATCH, GLA_H, GLA_DK, GLA_DV), 0.3),
        "cache_mem_k": nrm(ks[6], (DEPTH, DEC_BATCH, MEM_LEN, MEM_H, MEM_HD), 1.0),
        "cache_mem_v": nrm(ks[7], (DEPTH, DEC_BATCH, MEM_LEN, MEM_H, MEM_HD), 1.0),
        "norm_g": 1.0 + nrm(ks[8], (DEPTH, D_MODEL), 0.02),
        "w_in": nrm(ks[9], (DEPTH, D_MODEL, N_IN), D_MODEL ** -0.5),
        "attn_sinks": nrm(ks[10], (DEPTH, SWA_HQ), 0.5),
        "gla_w_gate": nrm(ks[11], (DEPTH, GLA_RANK, GLA_KW), GLA_RANK ** -0.5),
        "gla_b_gate": nrm(ks[12], (DEPTH, GLA_KW), 0.1),
        "gla_norm_g": 1.0 + nrm(ks[13], (DEPTH, GLA_W), 0.02),
        "mem_norm_g": 1.0 + nrm(ks[14], (DEPTH, D_MODEL), 0.02),
        "w_mem_kv": nrm(ks[15], (DEPTH, D_MODEL, 2 * MEM_W), D_MODEL ** -0.5),
        "w_out": nrm(ks[16], (DEPTH, D_MODEL, D_MODEL), D_MODEL ** -0.5),
        "final_norm_g": 1.0 + nrm(ks[17], (D_MODEL,), 0.02),
    }


def reference(x_prompt, mem_prompt, x_sample, cache_swa_k, cache_swa_v, state_gla, cache_mem_k, cache_mem_v,
              norm_g, w_in, attn_sinks, gla_w_gate, gla_b_gate, gla_norm_g, mem_norm_g, w_mem_kv, w_out,
              final_norm_g):
    B, T = x_prompt.shape[:2]
    Bd, Td = x_sample.shape[:2]
    pos_p = jnp.arange(T)
    pos_s = PAST_LEN + jnp.arange(Td)
    xp, xs = x_prompt, x_sample
    kp_l, vp_l, sp_l, mkp_l, mvp_l, ks_l, vs_l, ss_l = [], [], [], [], [], [], [], []
    for l in range(DEPTH):
        lw = (norm_g[l], w_in[l], attn_sinks[l], gla_w_gate[l], gla_b_gate[l], gla_norm_g[l], w_out[l])
        mk_p, mv_p = mem_kv(mem_prompt, mem_norm_g[l], w_mem_kv[l])
        s0_p = jnp.zeros((B, GLA_H, GLA_DK, GLA_DV), xp.dtype)
        xp, kb, vb, Sp = mixer_layer(xp, pos_p, mk_p, mv_p, swa_prompt, s0_p, *lw)
        swa_s = functools.partial(swa_sample, kbuf=cache_swa_k[l], vbuf=cache_swa_v[l])
        xs, kbs, vbs, Ss = mixer_layer(xs, pos_s, cache_mem_k[l], cache_mem_v[l], swa_s, state_gla[l], *lw)
        kp_l.append(kb); vp_l.append(vb); sp_l.append(Sp); mkp_l.append(mk_p); mvp_l.append(mv_p)
        ks_l.append(kbs); vs_l.append(vbs); ss_l.append(Ss)
    y_prompt = rmsnorm(xp, final_norm_g)
    y_sample = rmsnorm(xs, final_norm_g)
    return (y_prompt, y_sample, jnp.stack(kp_l), jnp.stack(vp_l), jnp.stack(sp_l), jnp.stack(mkp_l),
            jnp.stack(mvp_l), jnp.stack(ks_l), jnp.stack(vs_l), jnp.stack(ss_l))
```

```python
import functools

import jax
import jax.numpy as jnp
import numpy as np
from jax import lax
from jax.experimental import pallas as pl
from jax.experimental.pallas import tpu as pltpu

F32 = jnp.float32
BF16 = jnp.bfloat16

D_MODEL = 4096
DEPTH = 2
EPS = 1e-6
PAST_LEN = 16384
WINDOW = 128
ROPE_THETA = 10000.0
SWA_HD = 64
SWA_HQ = 32
SWA_KV = 4
GLA_H = 4
GLA_DK = 128
GLA_DV = 256
GLA_RANK = 16
GLA_TAU = 16.0
MEM_LEN = 256
MEM_H = 4
MEM_HD = 256
LANES = 128
SUB = 16

_SRC = dict(sq=(0, 2048), sk=(2048, 256), sv=(2304, 256), sgate=(2560, 2048), gq=(4608, 512),
            gk=(5120, 512), gv=(5632, 1024), glr=(6656, 16), ggate=(6672, 1024), mq=(7696, 1024),
            mgate=(8720, 1024))
_ORDER = ("sq", "sgate", "gv", "ggate", "mq", "mgate", "gq", "gk", "sk", "sv", "glr")
SQ, SG, GV, GG, MQ, MG, GQ, GK, SK, SV, GLR = 0, 2048, 4096, 5120, 6144, 7168, 8192, 8704, 9216, 9472, 9728
N_PAD = 9984
VMEM_LIMIT = 56 * 1024 * 1024


def _cparams(sem):
    return pltpu.CompilerParams(dimension_semantics=sem, vmem_limit_bytes=VMEM_LIMIT)


def _sigmoid(x):
    return 1.0 / (1.0 + jnp.exp(-x))


def _rmsnorm_kernel(x_ref, g_ref, o_ref):
    x = x_ref[...]
    ms = jnp.mean(x * x, axis=-1, keepdims=True)
    o_ref[...] = (x * lax.rsqrt(ms + EPS) * g_ref[...]).astype(o_ref.dtype)


def rmsnorm(x, g, out_dtype, tm):
    m, d = x.shape
    return pl.pallas_call(
        _rmsnorm_kernel,
        grid=(m // tm,),
        in_specs=[pl.BlockSpec((tm, d), lambda i: (i, 0)), pl.BlockSpec((1, d), lambda i: (0, 0))],
        out_specs=pl.BlockSpec((tm, d), lambda i: (i, 0)),
        out_shape=jax.ShapeDtypeStruct((m, d), out_dtype),
        compiler_params=_cparams(("parallel",)),
        name="rmsnorm",
    )(x, g.reshape(1, d))


def _mm_kernel(a_ref, b_ref, o_ref):
    o_ref[...] = jnp.dot(a_ref[...], b_ref[...], preferred_element_type=F32).astype(o_ref.dtype)


def _mm_res_kernel(a_ref, b_ref, r_ref, o_ref):
    o_ref[...] = r_ref[...] + jnp.dot(a_ref[...], b_ref[...], preferred_element_type=F32)


def matmul(a, b, tm, tn, out_dtype=F32, res=None, name="matmul"):
    m, k = a.shape
    n = b.shape[1]
    in_specs = [pl.BlockSpec((tm, k), lambda i, j: (i, 0)), pl.BlockSpec((k, tn), lambda i, j: (0, j))]
    args = [a, b]
    kern = _mm_kernel
    if res is not None:
        in_specs.append(pl.BlockSpec((tm, tn), lambda i, j: (i, j)))
        args.append(res)
        kern = _mm_res_kernel
    return pl.pallas_call(
        kern,
        grid=(m // tm, n // tn),
        in_specs=in_specs,
        out_specs=pl.BlockSpec((tm, tn), lambda i, j: (i, j)),
        out_shape=jax.ShapeDtypeStruct((m, n), out_dtype),
        compiler_params=_cparams(("parallel", "arbitrary")),
        name=name,
    )(*args)


def _rope_tables(pos):
    half = SWA_HD // 2
    inv = ROPE_THETA ** (-jnp.arange(half, dtype=F32) / half)
    ang = pos.astype(F32)[:, None] * inv[None, :]
    cos, sin = jnp.cos(ang), jnp.sin(ang)
    return jnp.concatenate([cos, cos, cos, cos], axis=-1), jnp.concatenate([-sin, sin, -sin, sin], axis=-1)


def _rope128(x, cos, sin, first_half):
    partner = jnp.where(first_half, pltpu.roll(x, LANES - 32, 1), pltpu.roll(x, 32, 1))
    return x * cos + partner * sin


def _dup_halves(x, lo):
    r = pltpu.roll(x, 64, 1)
    return jnp.where(lo, x, r), jnp.where(lo, r, x)


def _swa_prompt_kernel(sink_ref, q_ref, g_ref, k_ref, v_ref, cos_ref, sin_ref,
                       o_ref, ko_ref, vo_ref, kprev, vprev):
    n = pl.program_id(1)

    @pl.when(n == 0)
    def _():
        kprev[...] = jnp.zeros_like(kprev)
        vprev[...] = jnp.zeros_like(vprev)

    w = WINDOW
    cos = cos_ref[...]
    sin = sin_ref[...]
    lane = lax.broadcasted_iota(jnp.int32, (w, LANES), 1)
    first_half = (lane & 32) == 0
    lo = lane < 64

    kall, vall = [], []
    kcur, vcur = [], []
    for p in range(2):
        kp = _rope128(k_ref[:, p * LANES:(p + 1) * LANES], cos, sin, first_half)
        vp = v_ref[:, p * LANES:(p + 1) * LANES]
        ko_ref[:, p * LANES:(p + 1) * LANES] = kp
        vo_ref[:, p * LANES:(p + 1) * LANES] = vp
        kcur += [x.astype(BF16) for x in _dup_halves(kp, lo)]
        vcur += [x.astype(BF16) for x in _dup_halves(vp, lo)]
    for kh in range(SWA_KV):
        kall.append(jnp.concatenate([kprev[kh], kcur[kh]], axis=0))
        vall.append(jnp.concatenate([vprev[kh], vcur[kh]], axis=0))
    for kh in range(SWA_KV):
        kprev[kh] = kcur[kh]
        vprev[kh] = vcur[kh]

    qi = lax.broadcasted_iota(jnp.int32, (w, 2 * w), 0)
    sj = lax.broadcasted_iota(jnp.int32, (w, 2 * w), 1)
    diff = qi + w - sj
    in_band = jnp.where(diff >= 0, jnp.where(diff < w, 1, 0), 0)
    seen = jnp.where(sj >= w, 1, jnp.where(n > 0, 1, 0))
    valid = (in_band * seen) > 0

    scale = SWA_HD ** -0.5
    for kh in range(SWA_KV):
        rows = []
        for pr in range(4):
            c = 4 * kh + pr
            qc = _rope128(q_ref[:, c * LANES:(c + 1) * LANES], cos, sin, first_half) * scale
            rows.append(jnp.where(lo, qc, 0.0).astype(BF16))
            rows.append(jnp.where(lo, 0.0, qc).astype(BF16))
        lhs = jnp.concatenate(rows, axis=0)
        s = lax.dot_general(lhs, kall[kh], (((1,), (1,)), ((), ())), preferred_element_type=F32)
        ps, invs = [], []
        for hl in range(8):
            sh = jnp.where(valid, s[hl * w:(hl + 1) * w], -1e30)
            sk = sink_ref[8 * kh + hl]
            m = jnp.maximum(jnp.max(sh, axis=-1, keepdims=True), sk)
            p = jnp.exp(sh - m)
            den = jnp.sum(p, axis=-1, keepdims=True) + jnp.exp(sk - m)
            ps.append(p.astype(BF16))
            invs.append(1.0 / den)
        o = jnp.dot(jnp.concatenate(ps, axis=0), vall[kh], preferred_element_type=F32)
        for pr in range(4):
            c = 4 * kh + pr
            oe = o[(2 * pr) * w:(2 * pr + 1) * w] * invs[2 * pr]
            oo = o[(2 * pr + 1) * w:(2 * pr + 2) * w] * invs[2 * pr + 1]
            gt = g_ref[:, c * LANES:(c + 1) * LANES]
            o_ref[:, c * LANES:(c + 1) * LANES] = (jnp.where(lo, oe, oo) * (gt * _sigmoid(gt))).astype(o_ref.dtype)


def swa_prompt(h, sinks, cos, sin, batch, seq):
    m = h.shape[0]
    nb = seq // WINDOW
    kvw = SWA_KV * SWA_HD
    row = lambda b, n: b * nb + n
    return pl.pallas_call(
        _swa_prompt_kernel,
        grid=(batch, nb),
        in_specs=[
            pl.BlockSpec(memory_space=pltpu.SMEM),
            pl.BlockSpec((WINDOW, 2048), lambda b, n: (row(b, n), SQ // 2048)),
            pl.BlockSpec((WINDOW, 2048), lambda b, n: (row(b, n), SG // 2048)),
            pl.BlockSpec((WINDOW, kvw), lambda b, n: (row(b, n), SK // kvw)),
            pl.BlockSpec((WINDOW, kvw), lambda b, n: (row(b, n), SV // kvw)),
            pl.BlockSpec((WINDOW, LANES), lambda b, n: (n, 0)),
            pl.BlockSpec((WINDOW, LANES), lambda b, n: (n, 0)),
        ],
        out_specs=[
            pl.BlockSpec((WINDOW, 2048), lambda b, n: (row(b, n), 0)),
            pl.BlockSpec((None, WINDOW, kvw), lambda b, n: (b, 0, 0)),
            pl.BlockSpec((None, WINDOW, kvw), lambda b, n: (b, 0, 0)),
        ],
        out_shape=[
            jax.ShapeDtypeStruct((m, D_MODEL), BF16),
            jax.ShapeDtypeStruct((batch, WINDOW, kvw), F32),
            jax.ShapeDtypeStruct((batch, WINDOW, kvw), F32),
        ],
        scratch_shapes=[pltpu.VMEM((SWA_KV, WINDOW, LANES), BF16), pltpu.VMEM((SWA_KV, WINDOW, LANES), BF16)],
        compiler_params=_cparams(("arbitrary", "arbitrary")),
        name="swa_prompt",
    )(sinks, h, h, h, h, cos, sin)


def _mem_prompt_kernel(q_ref, g_ref, k_ref, v_ref, mix_ref, o_ref):
    del mix_ref
    scale = MEM_HD ** -0.5
    for h in range(MEM_H):
        sl = slice(h * MEM_HD, (h + 1) * MEM_HD)
        q = (q_ref[:, sl] * scale).astype(BF16)
        k = k_ref[:, sl].astype(BF16)
        v = v_ref[:, sl].astype(BF16)
        s = lax.dot_general(q, k, (((1,), (1,)), ((), ())), preferred_element_type=F32)
        m = jnp.max(s, axis=-1, keepdims=True)
        p = jnp.exp(s - m)
        den = jnp.sum(p, axis=-1, keepdims=True)
        o = jnp.dot(p.astype(BF16), v, preferred_element_type=F32) * (1.0 / den)
        g = g_ref[:, sl]
        o_ref[:, sl] = (o * (g * _sigmoid(g))).astype(o_ref.dtype)


def mem_attn_prompt(h, kv, mix, batch, seq, tq):
    nt = seq // tq
    w = MEM_H * MEM_HD
    row = lambda b, t: b * nt + t
    return pl.pallas_call(
        _mem_prompt_kernel,
        grid=(batch, nt),
        in_specs=[
            pl.BlockSpec((tq, w), lambda b, t: (row(b, t), MQ // w)),
            pl.BlockSpec((tq, w), lambda b, t: (row(b, t), MG // w)),
            pl.BlockSpec((MEM_LEN, w), lambda b, t: (b, 0)),
            pl.BlockSpec((MEM_LEN, w), lambda b, t: (b, 1)),
            pl.BlockSpec(memory_space=pl.ANY),
        ],
        out_specs=pl.BlockSpec((tq, w), lambda b, t: (row(b, t), 3)),
        out_shape=jax.ShapeDtypeStruct(mix.shape, mix.dtype),
        input_output_aliases={4: 0},
        compiler_params=_cparams(("parallel", "parallel")),
        name="mem_prompt",
    )(h, h, kv, kv, mix)


def _log_sigmoid(x):
    return jnp.minimum(x, 0.0) - jnp.log1p(jnp.exp(-jnp.abs(x)))


def _gla_prompt_kernel(q_ref, k_ref, v_ref, g_ref, lr_ref, wg_ref, bg_ref, gn_ref, mix_ref,
                       o_ref, so_ref, st, qt, kt, eb, am, ob):
    del mix_ref
    t = pl.program_id(1)
    tb = q_ref.shape[0]
    ng = tb // SUB

    @pl.when(t == 0)
    def _():
        st[...] = jnp.zeros_like(st)

    r = lax.broadcasted_iota(jnp.int32, (tb, tb), 0)
    c = lax.broadcasted_iota(jnp.int32, (tb, tb), 1)
    same = (r // SUB) == (c // SUB)
    ltri = jnp.where(same, jnp.where(c <= r, 1.0, 0.0), 0.0)
    lone = jnp.where(same, 1.0, 0.0)
    lr = lr_ref[...].astype(BF16)
    ii = lax.broadcasted_iota(jnp.int32, (ng, SUB, GLA_DK), 1)
    jj = lax.broadcasted_iota(jnp.int32, (ng, SUB, SUB), 2)

    for hd in range(GLA_H):
        ks = slice(hd * GLA_DK, (hd + 1) * GLA_DK)
        x = jnp.dot(lr, wg_ref[:, ks], preferred_element_type=F32) + bg_ref[:, ks]
        g = _log_sigmoid(x) * (1.0 / GLA_TAU)
        b = jnp.dot(ltri, g, precision=lax.Precision.HIGHEST, preferred_element_type=F32)
        bl = jnp.dot(lone, g, precision=lax.Precision.HIGHEST, preferred_element_type=F32)
        q = q_ref[:, ks] * (GLA_DK ** -0.5)
        k = k_ref[:, ks]
        qt[hd] = (q * jnp.exp(b)).astype(BF16)
        kt[hd] = (k * jnp.exp(bl - b)).astype(BF16)
        eb[hd] = jnp.exp(bl)
        q3 = q.reshape(ng, SUB, GLA_DK)
        k3 = k.reshape(ng, SUB, GLA_DK)
        b3 = b.reshape(ng, SUB, GLA_DK)
        a3 = jnp.zeros((ng, SUB, SUB), F32)
        for j in range(SUB):
            dec = jnp.exp(jnp.where(ii >= j, b3 - b3[:, j:j + 1, :], -jnp.inf))
            col = jnp.sum(q3 * k3[:, j:j + 1, :] * dec, axis=-1, keepdims=True)
            a3 = jnp.where(jj == j, col, a3)
        am[hd] = a3.reshape(tb, SUB).astype(BF16)

    def body(s, carry):
        r0 = pl.multiple_of(s * SUB, SUB)
        rows = pl.ds(r0, SUB)
        for hd in range(GLA_H):
            vv = v_ref[rows, hd * GLA_DV:(hd + 1) * GLA_DV].astype(BF16)
            sb = st[hd]
            o = lax.dot_general(qt[hd, rows, :], sb.astype(BF16), (((1,), (1,)), ((), ())),
                                preferred_element_type=F32)
            o = o + jnp.dot(am[hd, rows, :], vv, preferred_element_type=F32)
            ob[hd, rows, :] = o
            upd = lax.dot_general(vv, kt[hd, rows, :], (((0,), (0,)), ((), ())), preferred_element_type=F32)
            st[hd] = sb * eb[hd, pl.ds(r0, 1), :] + upd
        return carry

    lax.fori_loop(0, ng, body, 0)

    for hd in range(GLA_H):
        vs = slice(hd * GLA_DV, (hd + 1) * GLA_DV)
        o = ob[hd]
        on = o * lax.rsqrt(jnp.mean(o * o, axis=-1, keepdims=True) + EPS) * gn_ref[:, vs]
        gg = g_ref[:, vs]
        o_ref[:, vs] = (on * (gg * _sigmoid(gg))).astype(o_ref.dtype)

    @pl.when(t == pl.num_programs(1) - 1)
    def _():
        for hd in range(GLA_H):
            so_ref[hd] = st[hd].T


def gla_prompt(h, wg, bg, gn, mix, batch, seq, tb):
    nt = seq // tb
    kw = GLA_H * GLA_DK
    vw = GLA_H * GLA_DV
    row = lambda b, t: b * nt + t
    return pl.pallas_call(
        _gla_prompt_kernel,
        grid=(batch, nt),
        in_specs=[
            pl.BlockSpec((tb, kw), lambda b, t: (row(b, t), GQ // kw)),
            pl.BlockSpec((tb, kw), lambda b, t: (row(b, t), GK // kw)),
            pl.BlockSpec((tb, vw), lambda b, t: (row(b, t), GV // vw)),
            pl.BlockSpec((tb, vw), lambda b, t: (row(b, t), GG // vw)),
            pl.BlockSpec((tb, LANES), lambda b, t: (row(b, t), GLR // LANES)),
            pl.BlockSpec((LANES, kw), lambda b, t: (0, 0)),
            pl.BlockSpec((1, kw), lambda b, t: (0, 0)),
            pl.BlockSpec((1, vw), lambda b, t: (0, 0)),
            pl.BlockSpec(memory_space=pl.ANY),
        ],
        out_specs=[
            pl.BlockSpec((tb, vw), lambda b, t: (row(b, t), 2)),
            pl.BlockSpec((None, GLA_H, GLA_DK, GLA_DV), lambda b, t: (b, 0, 0, 0)),
        ],
        out_shape=[
            jax.ShapeDtypeStruct(mix.shape, mix.dtype),
            jax.ShapeDtypeStruct((batch, GLA_H, GLA_DK, GLA_DV), F32),
        ],
        input_output_aliases={8: 0},
        scratch_shapes=[
            pltpu.VMEM((GLA_H, GLA_DV, GLA_DK), F32),
            pltpu.VMEM((GLA_H, tb, GLA_DK), BF16),
            pltpu.VMEM((GLA_H, tb, GLA_DK), BF16),
            pltpu.VMEM((GLA_H, tb, GLA_DK), F32),
            pltpu.VMEM((GLA_H, tb, SUB), BF16),
            pltpu.VMEM((GLA_H, tb, GLA_DV), F32),
        ],
        compiler_params=_cparams(("arbitrary", "arbitrary")),
        name="gla_prompt",
    )(h, h, h, h, h, wg, bg, gn, mix)


def _swa_sample_kernel(q_ref, g_ref, k_ref, v_ref, kb_ref, vb_ref, cos_ref, sin_ref, sink_ref,
                       o_ref, ko_ref, vo_ref):
    w = kb_ref.shape[0]
    cos = cos_ref[...]
    sin = sin_ref[...]
    lane1 = lax.broadcasted_iota(jnp.int32, (1, LANES), 1)
    lane = lax.broadcasted_iota(jnp.int32, (16, LANES), 1)
    rowi = lax.broadcasted_iota(jnp.int32, (w, LANES), 0)
    lo_w = lax.broadcasted_iota(jnp.int32, (w, LANES), 1) < 64
    lo = lane < 64

    kk, vv = [], []
    for p in range(2):
        sl = slice(p * LANES, (p + 1) * LANES)
        knew = _rope128(k_ref[:, sl], cos, sin, (lane1 & 32) == 0)
        vnew = v_ref[:, sl]
        kwin = jnp.where(rowi == w - 1, knew, pltpu.roll(kb_ref[:, sl], w - 1, 0))
        vwin = jnp.where(rowi == w - 1, vnew, pltpu.roll(vb_ref[:, sl], w - 1, 0))
        ko_ref[:, sl] = kwin
        vo_ref[:, sl] = vwin
        kk += [x.astype(BF16) for x in _dup_halves(kwin, lo_w)]
        vv += [x.astype(BF16) for x in _dup_halves(vwin, lo_w)]

    q = _rope128(q_ref[...], cos, sin, (lane & 32) == 0) * (SWA_HD ** -0.5)
    qq = jnp.concatenate([jnp.where(lo, q, 0.0), jnp.where(lo, 0.0, q)], axis=0)
    grp = (lax.broadcasted_iota(jnp.int32, (32, LANES), 0) % 16) // 4
    s = jnp.zeros((32, w), F32)
    for kh in range(SWA_KV):
        s = s + lax.dot_general(jnp.where(grp == kh, qq, 0.0).astype(BF16), kk[kh],
                                (((1,), (1,)), ((), ())), preferred_element_type=F32)
    sk = sink_ref[...][:, 0:1]
    m = jnp.maximum(jnp.max(s, axis=-1, keepdims=True), sk)
    p = jnp.exp(s - m)
    den = jnp.sum(p, axis=-1, keepdims=True) + jnp.exp(sk - m)
    o = jnp.zeros((32, LANES), F32)
    for kh in range(SWA_KV):
        o = o + jnp.dot(jnp.where(grp == kh, p, 0.0).astype(BF16), vv[kh], preferred_element_type=F32)
    o = o * (1.0 / den)
    g = g_ref[...]
    o_ref[...] = jnp.where(lo, o[0:16], o[16:32]) * (g * _sigmoid(g))


def swa_sample(q, g, k, v, kbuf, vbuf, cos, sin, sink_rows):
    bd, wb, kvw = kbuf.shape
    return pl.pallas_call(
        _swa_sample_kernel,
        grid=(bd,),
        in_specs=[
            pl.BlockSpec((None, 16, LANES), lambda b: (b, 0, 0)),
            pl.BlockSpec((None, 16, LANES), lambda b: (b, 0, 0)),
            pl.BlockSpec((None, 1, kvw), lambda b: (b, 0, 0)),
            pl.BlockSpec((None, 1, kvw), lambda b: (b, 0, 0)),
            pl.BlockSpec((None, wb, kvw), lambda b: (b, 0, 0)),
            pl.BlockSpec((None, wb, kvw), lambda b: (b, 0, 0)),
            pl.BlockSpec((1, LANES), lambda b: (0, 0)),
            pl.BlockSpec((1, LANES), lambda b: (0, 0)),
            pl.BlockSpec((32, LANES), lambda b: (0, 0)),
        ],
        out_specs=[
            pl.BlockSpec((None, 16, LANES), lambda b: (b, 0, 0)),
            pl.BlockSpec((None, wb, kvw), lambda b: (b, 0, 0)),
            pl.BlockSpec((None, wb, kvw), lambda b: (b, 0, 0)),
        ],
        out_shape=[
            jax.ShapeDtypeStruct((bd, 16, LANES), F32),
            jax.ShapeDtypeStruct((bd, wb, kvw), F32),
            jax.ShapeDtypeStruct((bd, wb, kvw), F32),
        ],
        compiler_params=_cparams(("parallel",)),
        name="swa_sample",
    )(q, g, k, v, kbuf, vbuf, cos, sin, sink_rows)


def _mem_sample_kernel(q_ref, g_ref, k_ref, v_ref, o_ref):
    q = q_ref[...] * (MEM_HD ** -0.5)
    rowh = lax.broadcasted_iota(jnp.int32, (8, MEM_HD), 0)
    s = jnp.zeros((8, MEM_LEN), F32)
    for h in range(MEM_H):
        sl = slice(h * MEM_HD, (h + 1) * MEM_HD)
        s = s + lax.dot_general(jnp.where(rowh == h, q, 0.0).astype(BF16), k_ref[:, sl].astype(BF16),
                                (((1,), (1,)), ((), ())), preferred_element_type=F32)
    m = jnp.max(s, axis=-1, keepdims=True)
    p = jnp.exp(s - m)
    den = jnp.sum(p, axis=-1, keepdims=True)
    o = jnp.zeros((8, MEM_HD), F32)
    for h in range(MEM_H):
        sl = slice(h * MEM_HD, (h + 1) * MEM_HD)
        o = o + jnp.dot(jnp.where(rowh == h, p, 0.0).astype(BF16), v_ref[:, sl].astype(BF16),
                        preferred_element_type=F32)
    g = g_ref[...]
    o_ref[...] = o * (1.0 / den) * (g * _sigmoid(g))


def mem_sample(q, g, k, v):
    bd = q.shape[0]
    w = MEM_H * MEM_HD
    return pl.pallas_call(
        _mem_sample_kernel,
        grid=(bd,),
        in_specs=[
            pl.BlockSpec((None, 8, MEM_HD), lambda b: (b, 0, 0)),
            pl.BlockSpec((None, 8, MEM_HD), lambda b: (b, 0, 0)),
            pl.BlockSpec((None, MEM_LEN, w), lambda b: (b, 0, 0)),
            pl.BlockSpec((None, MEM_LEN, w), lambda b: (b, 0, 0)),
        ],
        out_specs=pl.BlockSpec((None, 8, MEM_HD), lambda b: (b, 0, 0)),
        out_shape=jax.ShapeDtypeStruct((bd, 8, MEM_HD), F32),
        compiler_params=_cparams(("parallel",)),
        name="mem_sample",
    )(q, g, k, v)


def _gla_sample_kernel(q_ref, k_ref, v_ref, g_ref, lr_ref, wg_ref, bg_ref, gn_ref, s_ref, o_ref, so_ref):
    rowk = lax.broadcasted_iota(jnp.int32, (8, GLA_DK), 0)
    lr = jnp.broadcast_to(lr_ref[...], (8, LANES)).astype(BF16)
    xa = jnp.dot(lr, wg_ref[...], preferred_element_type=F32) + bg_ref[...]
    x = jnp.zeros((8, GLA_DK), F32)
    for hd in range(GLA_H):
        x = jnp.where(rowk == hd, xa[:, hd * GLA_DK:(hd + 1) * GLA_DK], x)
    gl = _log_sigmoid(x) * (1.0 / GLA_TAU)
    eg = jnp.exp(gl)
    q = q_ref[...] * (GLA_DK ** -0.5)
    k = k_ref[...]
    v = v_ref[...]
    qe = q * eg
    eye = lax.broadcasted_iota(jnp.int32, (GLA_DK, GLA_DK), 0) == lax.broadcasted_iota(jnp.int32, (GLA_DK, GLA_DK), 1)

    def col(row):
        return jnp.sum(jnp.where(eye, jnp.broadcast_to(row, (GLA_DK, GLA_DK)), 0.0), axis=-1, keepdims=True)

    o = jnp.sum(q * k, axis=-1, keepdims=True) * v
    for hd in range(GLA_H):
        s0 = s_ref[hd]
        o = o + jnp.dot(jnp.where(rowk == hd, qe, 0.0).astype(BF16), s0.astype(BF16), preferred_element_type=F32)
        so_ref[hd] = s0 * col(eg[hd:hd + 1, :]) + col(k[hd:hd + 1, :]) * v[hd:hd + 1, :]
    on = o * lax.rsqrt(jnp.mean(o * o, axis=-1, keepdims=True) + EPS) * gn_ref[...]
    gg = g_ref[...]
    o_ref[...] = on * (gg * _sigmoid(gg))


def gla_sample(q, k, v, g, lr, wg, bg, gn, state):
    bd = q.shape[0]
    kw = GLA_H * GLA_DK
    return pl.pallas_call(
        _gla_sample_kernel,
        grid=(bd,),
        in_specs=[
            pl.BlockSpec((None, 8, GLA_DK), lambda b: (b, 0, 0)),
            pl.BlockSpec((None, 8, GLA_DK), lambda b: (b, 0, 0)),
            pl.BlockSpec((None, 8, GLA_DV), lambda b: (b, 0, 0)),
            pl.BlockSpec((None, 8, GLA_DV), lambda b: (b, 0, 0)),
            pl.BlockSpec((None, 1, LANES), lambda b: (b, 0, 0)),
            pl.BlockSpec((LANES, kw), lambda b: (0, 0)),
            pl.BlockSpec((1, kw), lambda b: (0, 0)),
            pl.BlockSpec((8, GLA_DV), lambda b: (0, 0)),
            pl.BlockSpec((None, GLA_H, GLA_DK, GLA_DV), lambda b: (b, 0, 0, 0)),
        ],
        out_specs=[
            pl.BlockSpec((None, 8, GLA_DV), lambda b: (b, 0, 0)),
            pl.BlockSpec((None, GLA_H, GLA_DK, GLA_DV), lambda b: (b, 0, 0, 0)),
        ],
        out_shape=[
            jax.ShapeDtypeStruct((bd, 8, GLA_DV), F32),
            jax.ShapeDtypeStruct(state.shape, F32),
        ],
        compiler_params=_cparams(("parallel",)),
        name="gla_sample",
    )(q, k, v, g, lr, wg, bg, gn, state)


def _permute_w_in(w):
    cols = [w[:, _SRC[name][0]:_SRC[name][0] + _SRC[name][1]] for name in _ORDER]
    pad = jnp.zeros((w.shape[0], N_PAD - sum(c.shape[1] for c in cols)), w.dtype)
    return jnp.concatenate(cols + [pad], axis=1).astype(BF16)


def _heads8(x, width):
    bd = x.shape[0]
    x = x.reshape(bd, -1, width)
    return jnp.pad(x, ((0, 0), (0, 8 - x.shape[1]), (0, 0)))


def kernel(x_prompt, mem_prompt, x_sample, cache_swa_k, cache_swa_v, state_gla, cache_mem_k, cache_mem_v,
           norm_g, w_in, attn_sinks, gla_w_gate, gla_b_gate, gla_norm_g, mem_norm_g, w_mem_kv, w_out,
           final_norm_g):
    batch, seq, d = x_prompt.shape
    bd = x_sample.shape[0]
    m = batch * seq
    kvw = SWA_KV * SWA_HD
    memw = MEM_H * MEM_HD

    cos_p, sin_p = _rope_tables(jnp.arange(seq))
    cos_s, sin_s = _rope_tables(PAST_LEN + jnp.arange(1))
    perm = np.array([2 * (r % 16) + r // 16 for r in range(32)])

    xp = x_prompt.reshape(m, d)
    xs = x_sample.reshape(bd, d)
    memx = mem_prompt.reshape(batch * MEM_LEN, d)

    outs = {k: [] for k in ("kp", "vp", "sp", "mkp", "mvp", "ks", "vs", "ss")}
    for l in range(DEPTH):
        w_in_l = _permute_w_in(w_in[l])
        w_out_l = w_out[l].astype(BF16)
        w_kv_l = w_mem_kv[l].astype(BF16)
        wg = jnp.pad(gla_w_gate[l], ((0, LANES - GLA_RANK), (0, 0))).astype(BF16)
        bg = gla_b_gate[l].reshape(1, -1)
        gn = gla_norm_g[l].reshape(1, -1)

        kv = matmul(rmsnorm(memx, mem_norm_g[l], BF16, 256), w_kv_l, 512, 1024, name="mem_kv")
        h = matmul(rmsnorm(xp, norm_g[l], BF16, 256), w_in_l, 1024, 768, name="in_proj")
        mix, kb, vb = swa_prompt(h, attn_sinks[l], cos_p, sin_p, batch, seq)
        mix, sp = gla_prompt(h, wg, bg, gn, mix, batch, seq, 256)
        mix = mem_attn_prompt(h, kv, mix, batch, seq, 512)
        xp = matmul(mix, w_out_l, 1024, 512, res=xp, name="out_proj")
        outs["kp"].append(kb.reshape(batch, WINDOW, SWA_KV, SWA_HD))
        outs["vp"].append(vb.reshape(batch, WINDOW, SWA_KV, SWA_HD))
        outs["sp"].append(sp)
        outs["mkp"].append(kv[:, :memw].reshape(batch, MEM_LEN, MEM_H, MEM_HD))
        outs["mvp"].append(kv[:, memw:].reshape(batch, MEM_LEN, MEM_H, MEM_HD))

        hs = matmul(rmsnorm(xs, norm_g[l], BF16, bd), w_in_l, bd, 768, name="in_proj_s")
        wb = cache_swa_k.shape[2]
        o_swa, kbs, vbs = swa_sample(
            hs[:, SQ:SQ + 2048].reshape(bd, 16, LANES), hs[:, SG:SG + 2048].reshape(bd, 16, LANES),
            hs[:, SK:SK + kvw].reshape(bd, 1, kvw), hs[:, SV:SV + kvw].reshape(bd, 1, kvw),
            cache_swa_k[l].reshape(bd, wb, kvw), cache_swa_v[l].reshape(bd, wb, kvw),
            cos_s, sin_s, jnp.broadcast_to(attn_sinks[l][perm][:, None], (32, LANES)))
        o_gla, ss = gla_sample(
            _heads8(hs[:, GQ:GQ + 512], GLA_DK), _heads8(hs[:, GK:GK + 512], GLA_DK),
            _heads8(hs[:, GV:GV + 1024], GLA_DV), _heads8(hs[:, GG:GG + 1024], GLA_DV),
            hs[:, GLR:GLR + LANES].reshape(bd, 1, LANES), wg, bg,
            jnp.pad(gla_norm_g[l].reshape(GLA_H, GLA_DV), ((0, 4), (0, 0))), state_gla[l])
        o_mem = mem_sample(
            _heads8(hs[:, MQ:MQ + memw], MEM_HD), _heads8(hs[:, MG:MG + memw], MEM_HD),
            cache_mem_k[l].reshape(bd, MEM_LEN, memw), cache_mem_v[l].reshape(bd, MEM_LEN, memw))
        mix_s = jnp.concatenate([o_swa.reshape(bd, 2048), o_gla[:, :GLA_H].reshape(bd, 1024),
                                 o_mem[:, :MEM_H].reshape(bd, memw)], axis=1).astype(BF16)
        xs = matmul(mix_s, w_out_l, bd, 1024, res=xs, name="out_proj_s")
        outs["ks"].append(kbs.reshape(bd, wb, SWA_KV, SWA_HD))
        outs["vs"].append(vbs.reshape(bd, wb, SWA_KV, SWA_HD))
        outs["ss"].append(ss)

    y_prompt = rmsnorm(xp, final_norm_g, F32, 256).reshape(batch, seq, d)
    y_sample = rmsnorm(xs, final_norm_g, F32, bd).reshape(bd, 1, d)
    st = lambda k: jnp.stack(outs[k])
    return (y_prompt, y_sample, st("kp"), st("vp"), st("sp"), st("mkp"), st("mvp"), st("ks"), st("vs"), st("ss"))
```

```python
import functools
import math

import jax
import jax.numpy as jnp
import numpy as np
from jax import lax
from jax.experimental import pallas as pl
from jax.experimental.pallas import tpu as pltpu

F32 = jnp.float32
BF16 = jnp.bfloat16

D_MODEL = 4096
DEPTH = 2
EPS = 1e-6
PAST_LEN = 16384
WINDOW = 128
ROPE_THETA = 10000.0
SWA_HD = 64
SWA_HQ = 32
SWA_KV = 4
GLA_H = 4
GLA_DK = 128
GLA_DV = 256
GLA_RANK = 16
GLA_TAU = 16.0
MEM_LEN = 256
MEM_H = 4
MEM_HD = 256
LANES = 128
SUB = 16
LOG2E = math.log2(math.e)

A_SQ, A_SK, A_SV, A_SG, A_GQ, A_GK, A_GV, A_END = 0, 2048, 2304, 2560, 4608, 5120, 5632, 6656
SRC_GLR, SRC_TAIL, SRC_END = 6656, 6672, 9744
B_GG, B_MQ, B_MG, B_GLR, B_END = 0, 1024, 2048, 3072, 3200
TM = 1024
TN = 512
VMEM_LIMIT = 56 * 1024 * 1024


def _cparams(sem):
    return pltpu.CompilerParams(dimension_semantics=sem, vmem_limit_bytes=VMEM_LIMIT)


def _sigmoid(x):
    return 1.0 / (1.0 + jnp.exp(-x))


def _rmsnorm_kernel(x_ref, g_ref, o_ref):
    x = x_ref[...]
    ms = jnp.mean(x * x, axis=-1, keepdims=True)
    o_ref[...] = (x * lax.rsqrt(ms + EPS) * g_ref[...]).astype(o_ref.dtype)


def rmsnorm(x, g, out_dtype, tm):
    m, d = x.shape
    return pl.pallas_call(
        _rmsnorm_kernel,
        grid=(m // tm,),
        in_specs=[pl.BlockSpec((tm, d), lambda i: (i, 0)), pl.BlockSpec((1, d), lambda i: (0, 0))],
        out_specs=pl.BlockSpec((tm, d), lambda i: (i, 0)),
        out_shape=jax.ShapeDtypeStruct((m, d), out_dtype),
        compiler_params=_cparams(("parallel",)),
        name="rmsnorm",
    )(x, g.reshape(1, d))


def _proj_kernel(*refs, has_small, has_res, tn, last_valid):
    refs = list(refs)
    a_ref = refs.pop(0)
    as_ref = refs.pop(0) if has_small else None
    w_ref = refs.pop(0)
    r_ref = refs.pop(0) if has_res else None
    rs_ref = refs.pop(0) if (has_res and has_small) else None
    o_ref = refs.pop(0)
    os_ref = refs.pop(0) if has_small else None
    i = pl.program_id(0)
    j = pl.program_id(1)
    nj = pl.num_programs(1)

    def compute(width):
        w = w_ref[:, :width].astype(BF16)
        acc = jnp.dot(a_ref[...], w, preferred_element_type=F32)
        if has_res:
            acc = acc + r_ref[:, :width]
        o_ref[:, :width] = acc
        if has_small:
            @pl.when(i == 0)
            def _():
                accs = jnp.dot(as_ref[...], w, preferred_element_type=F32)
                if has_res:
                    accs = accs + rs_ref[:, :width]
                os_ref[:, :width] = accs

    if last_valid == tn:
        compute(tn)
    else:
        pl.when(j < nj - 1)(lambda: compute(tn))
        pl.when(j == nj - 1)(lambda: compute(last_valid))


def project(a, w3, layer, n=None, a_small=None, res=None, res_small=None, name="proj"):
    m, k = a.shape
    n = w3.shape[2] if n is None else n
    tm = min(TM, m)
    nj = pl.cdiv(n, TN)
    last_valid = n - (nj - 1) * TN
    has_small = a_small is not None
    has_res = res is not None
    small_idx = lambda i, j: (0, jnp.where(i == 0, j, nj - 1))
    in_specs = [pl.BlockSpec((tm, k), lambda i, j: (i, 0))]
    args = [a]
    if has_small:
        ms = a_small.shape[0]
        in_specs.append(pl.BlockSpec((ms, k), lambda i, j: (0, 0)))
        args.append(a_small)
    in_specs.append(pl.BlockSpec((None, k, TN), lambda i, j: (layer, 0, j)))
    args.append(w3)
    if has_res:
        in_specs.append(pl.BlockSpec((tm, TN), lambda i, j: (i, j)))
        args.append(res)
        if has_small:
            in_specs.append(pl.BlockSpec((ms, TN), small_idx))
            args.append(res_small)
    out_specs = [pl.BlockSpec((tm, TN), lambda i, j: (i, j))]
    out_shape = [jax.ShapeDtypeStruct((m, n), F32)]
    if has_small:
        out_specs.append(pl.BlockSpec((ms, TN), small_idx))
        out_shape.append(jax.ShapeDtypeStruct((ms, n), F32))
    outs = pl.pallas_call(
        functools.partial(_proj_kernel, has_small=has_small, has_res=has_res, tn=TN, last_valid=last_valid),
        grid=(m // tm, nj),
        in_specs=in_specs,
        out_specs=out_specs,
        out_shape=out_shape,
        compiler_params=_cparams(("arbitrary", "arbitrary")),
        name=name,
    )(*args)
    return outs if has_small else outs[0]


def _rope_tables(pos):
    half = SWA_HD // 2
    inv = ROPE_THETA ** (-jnp.arange(half, dtype=F32) / half)
    ang = pos.astype(F32)[:, None] * inv[None, :]
    cos, sin = jnp.cos(ang), jnp.sin(ang)
    return jnp.concatenate([cos, cos, cos, cos], axis=-1), jnp.concatenate([-sin, sin, -sin, sin], axis=-1)


def _rope128(x, cos, sin, first_half):
    partner = jnp.where(first_half, pltpu.roll(x, LANES - 32, 1), pltpu.roll(x, 32, 1))
    return x * cos + partner * sin


def _dup_halves(x, lo):
    r = pltpu.roll(x, 64, 1)
    return jnp.where(lo, x, r), jnp.where(lo, r, x)


def _swa_prompt_kernel(sink_ref, q_ref, g0_ref, g1_ref, g2_ref, g3_ref, k_ref, v_ref, cos_ref, sin_ref,
                       o_ref, ko_ref, vo_ref, kprev, vprev):
    n = pl.program_id(1)

    @pl.when(n == 0)
    def _():
        kprev[...] = jnp.zeros_like(kprev)
        vprev[...] = jnp.zeros_like(vprev)

    w = WINDOW
    g_refs = (g0_ref, g1_ref, g2_ref, g3_ref)
    cos = cos_ref[...]
    sin = sin_ref[...]
    lane = lax.broadcasted_iota(jnp.int32, (w, LANES), 1)
    rowi = lax.broadcasted_iota(jnp.int32, (w, LANES), 0)
    first_half = (lane & 32) == 0
    lo = lane < 64
    own = lane <= rowi
    keep = jnp.where(own, 1, jnp.where(n > 0, 1, 0)) > 0
    ones = jnp.ones((2 * w, LANES), BF16)

    kall, vall = [], []
    kcur, vcur = [], []
    for p in range(2):
        kp = _rope128(k_ref[:, p * LANES:(p + 1) * LANES], cos, sin, first_half)
        vp = v_ref[:, p * LANES:(p + 1) * LANES]
        ko_ref[:, p * LANES:(p + 1) * LANES] = kp
        vo_ref[:, p * LANES:(p + 1) * LANES] = vp
        kcur += [x.astype(BF16) for x in _dup_halves(kp, lo)]
        vcur += [x.astype(BF16) for x in _dup_halves(vp, lo)]
    for kh in range(SWA_KV):
        kall.append(jnp.concatenate([kprev[kh], kcur[kh]], axis=0))
        vv = jnp.concatenate([vprev[kh], vcur[kh]], axis=0)
        vall.append(jnp.concatenate([vv, ones], axis=1))
    for kh in range(SWA_KV):
        kprev[kh] = kcur[kh]
        vprev[kh] = vcur[kh]

    scale = (SWA_HD ** -0.5) * LOG2E
    for kh in range(SWA_KV):
        rows = []
        for pr in range(4):
            c = 4 * kh + pr
            qc = _rope128(q_ref[:, c * LANES:(c + 1) * LANES], cos, sin, first_half) * scale
            rows.append(jnp.where(lo, qc, 0.0).astype(BF16))
            rows.append(jnp.where(lo, 0.0, qc).astype(BF16))
        lhs = jnp.concatenate(rows, axis=0)
        s = lax.dot_general(lhs, kall[kh], (((1,), (1,)), ((), ())), preferred_element_type=F32)
        ps, sinks2 = [], []
        for hl in range(8):
            sh = s[hl * w:(hl + 1) * w]
            f = jnp.where(keep, jnp.where(own, sh[:, w:], sh[:, :w]), -1e30)
            m = jnp.max(f, axis=-1, keepdims=True)
            p = jnp.exp2(f - m)
            ps.append(jnp.concatenate([jnp.where(own, 0.0, p), jnp.where(own, p, 0.0)], axis=1).astype(BF16))
            sinks2.append(jnp.exp2(sink_ref[8 * kh + hl] * LOG2E - m))
        o = jnp.dot(jnp.concatenate(ps, axis=0), vall[kh], preferred_element_type=F32)
        for pr in range(4):
            c = 4 * kh + pr
            halves = []
            for hl in (2 * pr, 2 * pr + 1):
                oh = o[hl * w:(hl + 1) * w]
                halves.append(oh[:, :LANES] * (1.0 / (oh[:, LANES:] + sinks2[hl])))
            gt = g_refs[c // 4][:, (c % 4) * LANES:(c % 4 + 1) * LANES]
            o_ref[:, c * LANES:(c + 1) * LANES] = (
                jnp.where(lo, halves[0], halves[1]) * (gt * _sigmoid(gt))).astype(o_ref.dtype)


def swa_prompt(ha, sinks, cos, sin, batch, seq):
    m = ha.shape[0]
    nb = seq // WINDOW
    kvw = SWA_KV * SWA_HD
    row = lambda b, n: b * nb + n
    gate_spec = lambda c: pl.BlockSpec((WINDOW, 512), lambda b, n: (row(b, n), A_SG // 512 + c))
    return pl.pallas_call(
        _swa_prompt_kernel,
        grid=(batch, nb),
        in_specs=[
            pl.BlockSpec(memory_space=pltpu.SMEM),
            pl.BlockSpec((WINDOW, 2048), lambda b, n: (row(b, n), A_SQ // 2048)),
            gate_spec(0), gate_spec(1), gate_spec(2), gate_spec(3),
            pl.BlockSpec((WINDOW, kvw), lambda b, n: (row(b, n), A_SK // kvw)),
            pl.BlockSpec((WINDOW, kvw), lambda b, n: (row(b, n), A_SV // kvw)),
            pl.BlockSpec((WINDOW, LANES), lambda b, n: (n, 0)),
            pl.BlockSpec((WINDOW, LANES), lambda b, n: (n, 0)),
        ],
        out_specs=[
            pl.BlockSpec((WINDOW, 2048), lambda b, n: (row(b, n), 0)),
            pl.BlockSpec((None, WINDOW, kvw), lambda b, n: (b, 0, 0)),
            pl.BlockSpec((None, WINDOW, kvw), lambda b, n: (b, 0, 0)),
        ],
        out_shape=[
            jax.ShapeDtypeStruct((m, D_MODEL), BF16),
            jax.ShapeDtypeStruct((batch, WINDOW, kvw), F32),
            jax.ShapeDtypeStruct((batch, WINDOW, kvw), F32),
        ],
        scratch_shapes=[pltpu.VMEM((SWA_KV, WINDOW, LANES), BF16), pltpu.VMEM((SWA_KV, WINDOW, LANES), BF16)],
        compiler_params=_cparams(("arbitrary", "arbitrary")),
        name="swa_prompt",
    )(sinks, ha, ha, ha, ha, ha, ha, ha, cos, sin)


def _mem_prompt_kernel(q_ref, g_ref, k_ref, v_ref, mix_ref, o_ref):
    del mix_ref
    scale = MEM_HD ** -0.5
    for h in range(MEM_H):
        sl = slice(h * MEM_HD, (h + 1) * MEM_HD)
        q = (q_ref[:, sl] * scale).astype(BF16)
        k = k_ref[:, sl].astype(BF16)
        v = v_ref[:, sl].astype(BF16)
        s = lax.dot_general(q, k, (((1,), (1,)), ((), ())), preferred_element_type=F32)
        m = jnp.max(s, axis=-1, keepdims=True)
        p = jnp.exp(s - m)
        den = jnp.sum(p, axis=-1, keepdims=True)
        o = jnp.dot(p.astype(BF16), v, preferred_element_type=F32) * (1.0 / den)
        g = g_ref[:, sl]
        o_ref[:, sl] = (o * (g * _sigmoid(g))).astype(o_ref.dtype)


def mem_attn_prompt(hb, kv, mix, batch, seq, tq):
    nt = seq // tq
    w = MEM_H * MEM_HD
    row = lambda b, t: b * nt + t
    return pl.pallas_call(
        _mem_prompt_kernel,
        grid=(batch, nt),
        in_specs=[
            pl.BlockSpec((tq, w), lambda b, t: (row(b, t), B_MQ // w)),
            pl.BlockSpec((tq, w), lambda b, t: (row(b, t), B_MG // w)),
            pl.BlockSpec((MEM_LEN, w), lambda b, t: (b, 0)),
            pl.BlockSpec((MEM_LEN, w), lambda b, t: (b, 1)),
            pl.BlockSpec(memory_space=pl.ANY),
        ],
        out_specs=pl.BlockSpec((tq, w), lambda b, t: (row(b, t), 3)),
        out_shape=jax.ShapeDtypeStruct(mix.shape, mix.dtype),
        input_output_aliases={4: 0},
        compiler_params=_cparams(("parallel", "parallel")),
        name="mem_prompt",
    )(hb, hb, kv, kv, mix)


def _log_sigmoid(x):
    return jnp.minimum(x, 0.0) - jnp.log1p(jnp.exp(-jnp.abs(x)))


def _gla_prompt_kernel(q_ref, k_ref, v0_ref, v1_ref, g_ref, lr_ref, wg_ref, bg_ref, gn_ref, mix_ref,
                       o_ref, so_ref, st, qt, kt, eb, am, ob):
    del mix_ref
    t = pl.program_id(1)
    tb = q_ref.shape[0]
    ng = tb // SUB
    v_refs = (v0_ref, v1_ref)

    @pl.when(t == 0)
    def _():
        st[...] = jnp.zeros_like(st)

    r = lax.broadcasted_iota(jnp.int32, (tb, tb), 0)
    c = lax.broadcasted_iota(jnp.int32, (tb, tb), 1)
    same = (r // SUB) == (c // SUB)
    ltri = jnp.where(same, jnp.where(c <= r, 1.0, 0.0), 0.0)
    lone = jnp.where(same, 1.0, 0.0)
    lr = lr_ref[...].astype(BF16)
    ii = lax.broadcasted_iota(jnp.int32, (ng, SUB, GLA_DK), 1)
    jj = lax.broadcasted_iota(jnp.int32, (ng, SUB, SUB), 2)

    for hd in range(GLA_H):
        ks = slice(hd * GLA_DK, (hd + 1) * GLA_DK)
        x = jnp.dot(lr, wg_ref[:, ks], preferred_element_type=F32) + bg_ref[:, ks]
        g = _log_sigmoid(x) * (1.0 / GLA_TAU)
        b = jnp.dot(ltri, g, precision=lax.Precision.HIGHEST, preferred_element_type=F32)
        bl = jnp.dot(lone, g, precision=lax.Precision.HIGHEST, preferred_element_type=F32)
        q = q_ref[:, ks] * (GLA_DK ** -0.5)
        k = k_ref[:, ks]
        qt[hd] = (q * jnp.exp(b)).astype(BF16)
        kt[hd] = (k * jnp.exp(bl - b)).astype(BF16)
        eb[hd] = jnp.exp(bl)
        q3 = q.reshape(ng, SUB, GLA_DK)
        k3 = k.reshape(ng, SUB, GLA_DK)
        b3 = b.reshape(ng, SUB, GLA_DK)
        a3 = jnp.zeros((ng, SUB, SUB), F32)
        for j in range(SUB):
            dec = jnp.exp(jnp.where(ii >= j, b3 - b3[:, j:j + 1, :], -jnp.inf))
            col = jnp.sum(q3 * k3[:, j:j + 1, :] * dec, axis=-1, keepdims=True)
            a3 = jnp.where(jj == j, col, a3)
        am[hd] = a3.reshape(tb, SUB).astype(BF16)

    def body(s, carry):
        r0 = pl.multiple_of(s * SUB, SUB)
        rows = pl.ds(r0, SUB)
        for hd in range(GLA_H):
            vv = v_refs[hd // 2][rows, (hd % 2) * GLA_DV:(hd % 2 + 1) * GLA_DV].astype(BF16)
            sb = st[hd]
            o = lax.dot_general(qt[hd, rows, :], sb.astype(BF16), (((1,), (1,)), ((), ())),
                                preferred_element_type=F32)
            o = o + jnp.dot(am[hd, rows, :], vv, preferred_element_type=F32)
            ob[hd, rows, :] = o
            upd = lax.dot_general(vv, kt[hd, rows, :], (((0,), (0,)), ((), ())), preferred_element_type=F32)
            st[hd] = sb * eb[hd, pl.ds(r0, 1), :] + upd
        return carry

    lax.fori_loop(0, ng, body, 0)

    for hd in range(GLA_H):
        vs = slice(hd * GLA_DV, (hd + 1) * GLA_DV)
        o = ob[hd]
        on = o * lax.rsqrt(jnp.mean(o * o, axis=-1, keepdims=True) + EPS) * gn_ref[:, vs]
        gg = g_ref[:, vs]
        o_ref[:, vs] = (on * (gg * _sigmoid(gg))).astype(o_ref.dtype)

    @pl.when(t == pl.num_programs(1) - 1)
    def _():
        for hd in range(GLA_H):
            so_ref[hd] = st[hd].T


def gla_prompt(ha, hb, wg, bg, gn, mix, batch, seq, tb):
    nt = seq // tb
    kw = GLA_H * GLA_DK
    vw = GLA_H * GLA_DV
    row = lambda b, t: b * nt + t
    return pl.pallas_call(
        _gla_prompt_kernel,
        grid=(batch, nt),
        in_specs=[
            pl.BlockSpec((tb, kw), lambda b, t: (row(b, t), A_GQ // kw)),
            pl.BlockSpec((tb, kw), lambda b, t: (row(b, t), A_GK // kw)),
            pl.BlockSpec((tb, 512), lambda b, t: (row(b, t), A_GV // 512)),
            pl.BlockSpec((tb, 512), lambda b, t: (row(b, t), A_GV // 512 + 1)),
            pl.BlockSpec((tb, vw), lambda b, t: (row(b, t), B_GG // vw)),
            pl.BlockSpec((tb, LANES), lambda b, t: (row(b, t), B_GLR // LANES)),
            pl.BlockSpec((LANES, kw), lambda b, t: (0, 0)),
            pl.BlockSpec((1, kw), lambda b, t: (0, 0)),
            pl.BlockSpec((1, vw), lambda b, t: (0, 0)),
            pl.BlockSpec(memory_space=pl.ANY),
        ],
        out_specs=[
            pl.BlockSpec((tb, vw), lambda b, t: (row(b, t), 2)),
            pl.BlockSpec((None, GLA_H, GLA_DK, GLA_DV), lambda b, t: (b, 0, 0, 0)),
        ],
        out_shape=[
            jax.ShapeDtypeStruct(mix.shape, mix.dtype),
            jax.ShapeDtypeStruct((batch, GLA_H, GLA_DK, GLA_DV), F32),
        ],
        input_output_aliases={9: 0},
        scratch_shapes=[
            pltpu.VMEM((GLA_H, GLA_DV, GLA_DK), F32),
            pltpu.VMEM((GLA_H, tb, GLA_DK), BF16),
            pltpu.VMEM((GLA_H, tb, GLA_DK), BF16),
            pltpu.VMEM((GLA_H, tb, GLA_DK), F32),
            pltpu.VMEM((GLA_H, tb, SUB), BF16),
            pltpu.VMEM((GLA_H, tb, GLA_DV), F32),
        ],
        compiler_params=_cparams(("arbitrary", "arbitrary")),
        name="gla_prompt",
    )(ha, ha, ha, ha, hb, hb, wg, bg, gn, mix)


def _swa_sample_kernel(q_ref, g_ref, k_ref, v_ref, kb_ref, vb_ref, cos_ref, sin_ref, sink_ref,
                       o_ref, ko_ref, vo_ref):
    w = kb_ref.shape[0]
    cos = cos_ref[...]
    sin = sin_ref[...]
    lane1 = lax.broadcasted_iota(jnp.int32, (1, LANES), 1)
    lane = lax.broadcasted_iota(jnp.int32, (16, LANES), 1)
    rowi = lax.broadcasted_iota(jnp.int32, (w, LANES), 0)
    lo_w = lax.broadcasted_iota(jnp.int32, (w, LANES), 1) < 64
    lo = lane < 64

    kk, vv = [], []
    for p in range(2):
        sl = slice(p * LANES, (p + 1) * LANES)
        knew = _rope128(k_ref[:, sl], cos, sin, (lane1 & 32) == 0)
        vnew = v_ref[:, sl]
        kwin = jnp.where(rowi == w - 1, knew, pltpu.roll(kb_ref[:, sl], w - 1, 0))
        vwin = jnp.where(rowi == w - 1, vnew, pltpu.roll(vb_ref[:, sl], w - 1, 0))
        ko_ref[:, sl] = kwin
        vo_ref[:, sl] = vwin
        kk += [x.astype(BF16) for x in _dup_halves(kwin, lo_w)]
        vv += [x.astype(BF16) for x in _dup_halves(vwin, lo_w)]

    q = _rope128(q_ref[...], cos, sin, (lane & 32) == 0) * (SWA_HD ** -0.5)
    qq = jnp.concatenate([jnp.where(lo, q, 0.0), jnp.where(lo, 0.0, q)], axis=0)
    grp = (lax.broadcasted_iota(jnp.int32, (32, LANES), 0) % 16) // 4
    s = jnp.zeros((32, w), F32)
    for kh in range(SWA_KV):
        s = s + lax.dot_general(jnp.where(grp == kh, qq, 0.0).astype(BF16), kk[kh],
                                (((1,), (1,)), ((), ())), preferred_element_type=F32)
    sk = sink_ref[...][:, 0:1]
    m = jnp.maximum(jnp.max(s, axis=-1, keepdims=True), sk)
    p = jnp.exp(s - m)
    den = jnp.sum(p, axis=-1, keepdims=True) + jnp.exp(sk - m)
    o = jnp.zeros((32, LANES), F32)
    for kh in range(SWA_KV):
        o = o + jnp.dot(jnp.where(grp == kh, p, 0.0).astype(BF16), vv[kh], preferred_element_type=F32)
    o = o * (1.0 / den)
    g = g_ref[...]
    o_ref[...] = jnp.where(lo, o[0:16], o[16:32]) * (g * _sigmoid(g))


def swa_sample(q, g, k, v, kbuf, vbuf, cos, sin, sink_rows):
    bd, wb, kvw = kbuf.shape
    return pl.pallas_call(
        _swa_sample_kernel,
        grid=(bd,),
        in_specs=[
            pl.BlockSpec((None, 16, LANES), lambda b: (b, 0, 0)),
            pl.BlockSpec((None, 16, LANES), lambda b: (b, 0, 0)),
            pl.BlockSpec((None, 1, kvw), lambda b: (b, 0, 0)),
            pl.BlockSpec((None, 1, kvw), lambda b: (b, 0, 0)),
            pl.BlockSpec((None, wb, kvw), lambda b: (b, 0, 0)),
            pl.BlockSpec((None, wb, kvw), lambda b: (b, 0, 0)),
            pl.BlockSpec((1, LANES), lambda b: (0, 0)),
            pl.BlockSpec((1, LANES), lambda b: (0, 0)),
            pl.BlockSpec((32, LANES), lambda b: (0, 0)),
        ],
        out_specs=[
            pl.BlockSpec((None, 16, LANES), lambda b: (b, 0, 0)),
            pl.BlockSpec((None, wb, kvw), lambda b: (b, 0, 0)),
            pl.BlockSpec((None, wb, kvw), lambda b: (b, 0, 0)),
        ],
        out_shape=[
            jax.ShapeDtypeStruct((bd, 16, LANES), F32),
            jax.ShapeDtypeStruct((bd, wb, kvw), F32),
            jax.ShapeDtypeStruct((bd, wb, kvw), F32),
        ],
        compiler_params=_cparams(("parallel",)),
        name="swa_sample",
    )(q, g, k, v, kbuf, vbuf, cos, sin, sink_rows)


def _mem_sample_kernel(q_ref, g_ref, k_ref, v_ref, o_ref):
    q = q_ref[...] * (MEM_HD ** -0.5)
    rowh = lax.broadcasted_iota(jnp.int32, (8, MEM_HD), 0)
    s = jnp.zeros((8, MEM_LEN), F32)
    for h in range(MEM_H):
        sl = slice(h * MEM_HD, (h + 1) * MEM_HD)
        s = s + lax.dot_general(jnp.where(rowh == h, q, 0.0).astype(BF16), k_ref[:, sl].astype(BF16),
                                (((1,), (1,)), ((), ())), preferred_element_type=F32)
    m = jnp.max(s, axis=-1, keepdims=True)
    p = jnp.exp(s - m)
    den = jnp.sum(p, axis=-1, keepdims=True)
    o = jnp.zeros((8, MEM_HD), F32)
    for h in range(MEM_H):
        sl = slice(h * MEM_HD, (h + 1) * MEM_HD)
        o = o + jnp.dot(jnp.where(rowh == h, p, 0.0).astype(BF16), v_ref[:, sl].astype(BF16),
                        preferred_element_type=F32)
    g = g_ref[...]
    o_ref[...] = o * (1.0 / den) * (g * _sigmoid(g))


def mem_sample(q, g, k, v):
    bd = q.shape[0]
    w = MEM_H * MEM_HD
    return pl.pallas_call(
        _mem_sample_kernel,
        grid=(bd,),
        in_specs=[
            pl.BlockSpec((None, 8, MEM_HD), lambda b: (b, 0, 0)),
            pl.BlockSpec((None, 8, MEM_HD), lambda b: (b, 0, 0)),
            pl.BlockSpec((None, MEM_LEN, w), lambda b: (b, 0, 0)),
            pl.BlockSpec((None, MEM_LEN, w), lambda b: (b, 0, 0)),
        ],
        out_specs=pl.BlockSpec((None, 8, MEM_HD), lambda b: (b, 0, 0)),
        out_shape=jax.ShapeDtypeStruct((bd, 8, MEM_HD), F32),
        compiler_params=_cparams(("parallel",)),
        name="mem_sample",
    )(q, g, k, v)


def _gla_sample_kernel(q_ref, k_ref, v_ref, g_ref, lr_ref, wg_ref, bg_ref, gn_ref, s_ref, o_ref, so_ref):
    rowk = lax.broadcasted_iota(jnp.int32, (8, GLA_DK), 0)
    lr = jnp.broadcast_to(lr_ref[...], (8, LANES)).astype(BF16)
    xa = jnp.dot(lr, wg_ref[...], preferred_element_type=F32) + bg_ref[...]
    x = jnp.zeros((8, GLA_DK), F32)
    for hd in range(GLA_H):
        x = jnp.where(rowk == hd, xa[:, hd * GLA_DK:(hd + 1) * GLA_DK], x)
    gl = _log_sigmoid(x) * (1.0 / GLA_TAU)
    eg = jnp.exp(gl)
    q = q_ref[...] * (GLA_DK ** -0.5)
    k = k_ref[...]
    v = v_ref[...]
    qe = q * eg
    eye = lax.broadcasted_iota(jnp.int32, (GLA_DK, GLA_DK), 0) == lax.broadcasted_iota(jnp.int32, (GLA_DK, GLA_DK), 1)

    def col(row):
        return jnp.sum(jnp.where(eye, jnp.broadcast_to(row, (GLA_DK, GLA_DK)), 0.0), axis=-1, keepdims=True)

    o = jnp.sum(q * k, axis=-1, keepdims=True) * v
    for hd in range(GLA_H):
        s0 = s_ref[hd]
        o = o + jnp.dot(jnp.where(rowk == hd, qe, 0.0).astype(BF16), s0.astype(BF16), preferred_element_type=F32)
        so_ref[hd] = s0 * col(eg[hd:hd + 1, :]) + col(k[hd:hd + 1, :]) * v[hd:hd + 1, :]
    on = o * lax.rsqrt(jnp.mean(o * o, axis=-1, keepdims=True) + EPS) * gn_ref[...]
    gg = g_ref[...]
    o_ref[...] = on * (gg * _sigmoid(gg))


def gla_sample(q, k, v, g, lr, wg, bg, gn, state):
    bd = q.shape[0]
    kw = GLA_H * GLA_DK
    return pl.pallas_call(
        _gla_sample_kernel,
        grid=(bd,),
        in_specs=[
            pl.BlockSpec((None, 8, GLA_DK), lambda b: (b, 0, 0)),
            pl.BlockSpec((None, 8, GLA_DK), lambda b: (b, 0, 0)),
            pl.BlockSpec((None, 8, GLA_DV), lambda b: (b, 0, 0)),
            pl.BlockSpec((None, 8, GLA_DV), lambda b: (b, 0, 0)),
            pl.BlockSpec((None, 1, LANES), lambda b: (b, 0, 0)),
            pl.BlockSpec((LANES, kw), lambda b: (0, 0)),
            pl.BlockSpec((1, kw), lambda b: (0, 0)),
            pl.BlockSpec((8, GLA_DV), lambda b: (0, 0)),
            pl.BlockSpec((None, GLA_H, GLA_DK, GLA_DV), lambda b: (b, 0, 0, 0)),
        ],
        out_specs=[
            pl.BlockSpec((None, 8, GLA_DV), lambda b: (b, 0, 0)),
            pl.BlockSpec((None, GLA_H, GLA_DK, GLA_DV), lambda b: (b, 0, 0, 0)),
        ],
        out_shape=[
            jax.ShapeDtypeStruct((bd, 8, GLA_DV), F32),
            jax.ShapeDtypeStruct(state.shape, F32),
        ],
        compiler_params=_cparams(("parallel",)),
        name="gla_sample",
    )(q, k, v, g, lr, wg, bg, gn, state)


def _heads8(x, width):
    bd = x.shape[0]
    x = x.reshape(bd, -1, width)
    return jnp.pad(x, ((0, 0), (0, 8 - x.shape[1]), (0, 0)))


def kernel(x_prompt, mem_prompt, x_sample, cache_swa_k, cache_swa_v, state_gla, cache_mem_k, cache_mem_v,
           norm_g, w_in, attn_sinks, gla_w_gate, gla_b_gate, gla_norm_g, mem_norm_g, w_mem_kv, w_out,
           final_norm_g):
    batch, seq, d = x_prompt.shape
    bd = x_sample.shape[0]
    m = batch * seq
    kvw = SWA_KV * SWA_HD
    memw = MEM_H * MEM_HD

    cos_p, sin_p = _rope_tables(jnp.arange(seq))
    cos_s, sin_s = _rope_tables(PAST_LEN + jnp.arange(1))
    perm = np.array([2 * (r % 16) + r // 16 for r in range(32)])
    w_tail = jnp.concatenate([w_in[:, :, SRC_TAIL:SRC_END], w_in[:, :, SRC_GLR:SRC_TAIL],
                              jnp.zeros((DEPTH, d, B_END - (SRC_END - SRC_GLR)), w_in.dtype)], axis=2)

    xp = x_prompt.reshape(m, d)
    xs = x_sample.reshape(bd, d)
    memx = mem_prompt.reshape(batch * MEM_LEN, d)

    outs = {k: [] for k in ("kp", "vp", "sp", "mkp", "mvp", "ks", "vs", "ss")}
    for l in range(DEPTH):
        wg = jnp.pad(gla_w_gate[l], ((0, LANES - GLA_RANK), (0, 0))).astype(BF16)
        bg = gla_b_gate[l].reshape(1, -1)
        gn = gla_norm_g[l].reshape(1, -1)

        kv = project(rmsnorm(memx, mem_norm_g[l], BF16, 256), w_mem_kv, l, name="mem_kv")
        xn = rmsnorm(xp, norm_g[l], BF16, 256)
        xns = rmsnorm(xs, norm_g[l], BF16, bd)
        ha, hsa = project(xn, w_in, l, n=A_END, a_small=xns, name="in_proj_a")
        hb, hsb = project(xn, w_tail, l, a_small=xns, name="in_proj_b")

        mix, kb, vb = swa_prompt(ha, attn_sinks[l], cos_p, sin_p, batch, seq)
        mix, sp = gla_prompt(ha, hb, wg, bg, gn, mix, batch, seq, 256)
        mix = mem_attn_prompt(hb, kv, mix, batch, seq, 512)
        outs["kp"].append(kb.reshape(batch, WINDOW, SWA_KV, SWA_HD))
        outs["vp"].append(vb.reshape(batch, WINDOW, SWA_KV, SWA_HD))
        outs["sp"].append(sp)
        outs["mkp"].append(kv[:, :memw].reshape(batch, MEM_LEN, MEM_H, MEM_HD))
        outs["mvp"].append(kv[:, memw:].reshape(batch, MEM_LEN, MEM_H, MEM_HD))

        wb = cache_swa_k.shape[2]
        o_swa, kbs, vbs = swa_sample(
            hsa[:, A_SQ:A_SQ + 2048].reshape(bd, 16, LANES), hsa[:, A_SG:A_SG + 2048].reshape(bd, 16, LANES),
            hsa[:, A_SK:A_SK + kvw].reshape(bd, 1, kvw), hsa[:, A_SV:A_SV + kvw].reshape(bd, 1, kvw),
            cache_swa_k[l].reshape(bd, wb, kvw), cache_swa_v[l].reshape(bd, wb, kvw),
            cos_s, sin_s, jnp.broadcast_to(attn_sinks[l][perm][:, None], (32, LANES)))
        o_gla, ss = gla_sample(
            _heads8(hsa[:, A_GQ:A_GQ + 512], GLA_DK), _heads8(hsa[:, A_GK:A_GK + 512], GLA_DK),
            _heads8(hsa[:, A_GV:A_GV + 1024], GLA_DV), _heads8(hsb[:, B_GG:B_GG + 1024], GLA_DV),
            hsb[:, B_GLR:B_GLR + LANES].reshape(bd, 1, LANES), wg, bg,
            jnp.pad(gla_norm_g[l].reshape(GLA_H, GLA_DV), ((0, 4), (0, 0))), state_gla[l])
        o_mem = mem_sample(
            _heads8(hsb[:, B_MQ:B_MQ + memw], MEM_HD), _heads8(hsb[:, B_MG:B_MG + memw], MEM_HD),
            cache_mem_k[l].reshape(bd, MEM_LEN, memw), cache_mem_v[l].reshape(bd, MEM_LEN, memw))
        mix_s = jnp.concatenate([o_swa.reshape(bd, 2048), o_gla[:, :GLA_H].reshape(bd, 1024),
                                 o_mem[:, :MEM_H].reshape(bd, memw)], axis=1).astype(BF16)
        outs["ks"].append(kbs.reshape(bd, wb, SWA_KV, SWA_HD))
        outs["vs"].append(vbs.reshape(bd, wb, SWA_KV, SWA_HD))
        outs["ss"].append(ss)

        xp, xs = project(mix, w_out, l, a_small=mix_s, res=xp, res_small=xs, name="out_proj")

    y_prompt = rmsnorm(xp, final_norm_g, F32, 256).reshape(batch, seq, d)
    y_sample = rmsnorm(xs, final_norm_g, F32, bd).reshape(bd, 1, d)
    st = lambda k: jnp.stack(outs[k])
    return (y_prompt, y_sample, st("kp"), st("vp"), st("sp"), st("mkp"), st("mvp"), st("ks"), st("vs"), st("ss"))
```

```python
import functools
import math

import jax
import jax.numpy as jnp
import numpy as np
from jax import lax
from jax.experimental import pallas as pl
from jax.experimental.pallas import tpu as pltpu

F32 = jnp.float32
BF16 = jnp.bfloat16

D_MODEL = 4096
DEPTH = 2
EPS = 1e-6
PAST_LEN = 16384
WINDOW = 128
ROPE_THETA = 10000.0
SWA_HD = 64
SWA_HQ = 32
SWA_KV = 4
GLA_H = 4
GLA_DK = 128
GLA_DV = 256
GLA_RANK = 16
GLA_TAU = 16.0
MEM_LEN = 256
MEM_H = 4
MEM_HD = 256
LANES = 128
GLA_CHUNK = 64
SUB = 16
LOG2E = math.log2(math.e)

TM = 1024
TN = 512
H_SQ, H_SG, H_GV, H_GG, H_MQ, H_MG, H_GQ, H_GK, H_SK, H_SV, H_GLR, H_END = (
    0, 2048, 4096, 5120, 6144, 7168, 8192, 8704, 9216, 9472, 9728, 9856)
IN_TILE_SRC = (0, 512, 1024, 1536, 2560, 3072, 3584, 4096, 5632, 6144, 6672, 7184, 7696, 8208, 8720, 9232,
               4608, 5120, 2048, 6656)
VMEM_LIMIT = 56 * 1024 * 1024


def _cparams(sem):
    return pltpu.CompilerParams(dimension_semantics=sem, vmem_limit_bytes=VMEM_LIMIT)


def _sigmoid(x):
    return 1.0 / (1.0 + jnp.exp(-x))


def _rmsnorm_kernel(x_ref, g_ref, o_ref):
    x = x_ref[...]
    ms = jnp.mean(x * x, axis=-1, keepdims=True)
    o_ref[...] = (x * lax.rsqrt(ms + EPS) * g_ref[...]).astype(o_ref.dtype)


def rmsnorm(x, g, out_dtype, tm):
    m, d = x.shape
    return pl.pallas_call(
        _rmsnorm_kernel,
        grid=(m // tm,),
        in_specs=[pl.BlockSpec((tm, d), lambda i: (i, 0)), pl.BlockSpec((1, d), lambda i: (0, 0))],
        out_specs=pl.BlockSpec((tm, d), lambda i: (i, 0)),
        out_shape=jax.ShapeDtypeStruct((m, d), out_dtype),
        compiler_params=_cparams(("parallel",)),
        name="rmsnorm",
    )(x, g.reshape(1, d))


def _proj_kernel(*refs, has_small, has_res, tn, last_valid, w_rows):
    refs = list(refs)
    if w_rows:
        refs.pop(0)
    a_ref = refs.pop(0)
    as_ref = refs.pop(0) if has_small else None
    w_ref = refs.pop(0)
    r_ref = refs.pop(0) if has_res else None
    rs_ref = refs.pop(0) if (has_res and has_small) else None
    o_ref = refs.pop(0)
    os_ref = refs.pop(0) if has_small else None
    i = pl.program_id(0)
    j = pl.program_id(1)
    nj = pl.num_programs(1)

    def compute(width):
        if w_rows:
            w = w_ref[:width, :].astype(BF16)
            mm = lambda a: lax.dot_general(a, w, (((1,), (1,)), ((), ())), preferred_element_type=F32)
        else:
            w = w_ref[:, :width].astype(BF16)
            mm = lambda a: jnp.dot(a, w, preferred_element_type=F32)
        acc = mm(a_ref[...])
        if has_res:
            acc = acc + r_ref[:, :width]
        o_ref[:, :width] = acc
        if has_small:
            @pl.when(i == 0)
            def _():
                accs = mm(as_ref[...])
                if has_res:
                    accs = accs + rs_ref[:, :width]
                os_ref[:, :width] = accs

    if last_valid == tn:
        compute(tn)
    else:
        pl.when(j < nj - 1)(lambda: compute(tn))
        pl.when(j == nj - 1)(lambda: compute(last_valid))


def project(a, w3, layer, n=None, tile_rows=None, a_small=None, res=None, res_small=None, name="proj"):
    m, k = a.shape
    w_rows = tile_rows is not None
    if n is None:
        n = w3.shape[2]
    tm = min(TM, m)
    nj = pl.cdiv(n, TN)
    last_valid = n - (nj - 1) * TN
    has_small = a_small is not None
    has_res = res is not None
    small_idx = lambda i, j, *_: (0, jnp.where(i == 0, j, nj - 1))
    in_specs = [pl.BlockSpec((tm, k), lambda i, j, *_: (i, 0))]
    args = [a]
    if has_small:
        ms = a_small.shape[0]
        in_specs.append(pl.BlockSpec((ms, k), lambda i, j, *_: (0, 0)))
        args.append(a_small)
    if w_rows:
        in_specs.append(pl.BlockSpec((None, pl.Element(TN), pl.Element(k)),
                                     lambda i, j, offs: (layer, pl.multiple_of(offs[j], 16), 0)))
    else:
        in_specs.append(pl.BlockSpec((None, k, TN), lambda i, j: (layer, 0, j)))
    args.append(w3)
    if has_res:
        in_specs.append(pl.BlockSpec((tm, TN), lambda i, j, *_: (i, j)))
        args.append(res)
        if has_small:
            in_specs.append(pl.BlockSpec((ms, TN), small_idx))
            args.append(res_small)
    out_specs = [pl.BlockSpec((tm, TN), lambda i, j, *_: (i, j))]
    out_shape = [jax.ShapeDtypeStruct((m, n), F32)]
    if has_small:
        out_specs.append(pl.BlockSpec((ms, TN), small_idx))
        out_shape.append(jax.ShapeDtypeStruct((ms, n), F32))
    kern = functools.partial(_proj_kernel, has_small=has_small, has_res=has_res, tn=TN, last_valid=last_valid,
                             w_rows=w_rows)
    if w_rows:
        grid_spec = pltpu.PrefetchScalarGridSpec(num_scalar_prefetch=1, grid=(m // tm, nj), in_specs=in_specs,
                                                 out_specs=out_specs)
        args = [jnp.asarray(tile_rows, jnp.int32)] + args
    else:
        grid_spec = pl.GridSpec(grid=(m // tm, nj), in_specs=in_specs, out_specs=out_specs)
    outs = pl.pallas_call(
        kern,
        grid_spec=grid_spec,
        out_shape=out_shape,
        compiler_params=_cparams(("arbitrary", "arbitrary")),
        name=name,
    )(*args)
    return outs if has_small else outs[0]


def _rope_tables(pos):
    half = SWA_HD // 2
    inv = ROPE_THETA ** (-jnp.arange(half, dtype=F32) / half)
    ang = pos.astype(F32)[:, None] * inv[None, :]
    cos, sin = jnp.cos(ang), jnp.sin(ang)
    return jnp.concatenate([cos, cos, cos, cos], axis=-1), jnp.concatenate([-sin, sin, -sin, sin], axis=-1)


def _rope128(x, cos, sin, first_half):
    partner = jnp.where(first_half, pltpu.roll(x, LANES - 32, 1), pltpu.roll(x, 32, 1))
    return x * cos + partner * sin


def _dup_halves(x, lo):
    r = pltpu.roll(x, 64, 1)
    return jnp.where(lo, x, r), jnp.where(lo, r, x)


def _swa_prompt_kernel(sink_ref, q_ref, g_ref, k_ref, v_ref, cos_ref, sin_ref,
                       o_ref, ko_ref, vo_ref, kprev, vprev):
    n = pl.program_id(1)

    @pl.when(n == 0)
    def _():
        kprev[...] = jnp.zeros_like(kprev)
        vprev[...] = jnp.zeros_like(vprev)

    w = WINDOW
    cos = cos_ref[...]
    sin = sin_ref[...]
    lane = lax.broadcasted_iota(jnp.int32, (w, LANES), 1)
    rowi = lax.broadcasted_iota(jnp.int32, (w, LANES), 0)
    first_half = (lane & 32) == 0
    lo = lane < 64
    own = lane <= rowi
    keep = jnp.where(own, 1, jnp.where(n > 0, 1, 0)) > 0
    ones = jnp.ones((2 * w, LANES), BF16)

    kall, vall = [], []
    kcur, vcur = [], []
    for p in range(2):
        kp = _rope128(k_ref[:, p * LANES:(p + 1) * LANES], cos, sin, first_half)
        vp = v_ref[:, p * LANES:(p + 1) * LANES]
        ko_ref[:, p * LANES:(p + 1) * LANES] = kp
        vo_ref[:, p * LANES:(p + 1) * LANES] = vp
        kcur += [x.astype(BF16) for x in _dup_halves(kp, lo)]
        vcur += [x.astype(BF16) for x in _dup_halves(vp, lo)]
    for kh in range(SWA_KV):
        kall.append(jnp.concatenate([kprev[kh], kcur[kh]], axis=0))
        vv = jnp.concatenate([vprev[kh], vcur[kh]], axis=0)
        vall.append(jnp.concatenate([vv, ones], axis=1))
    for kh in range(SWA_KV):
        kprev[kh] = kcur[kh]
        vprev[kh] = vcur[kh]

    scale = (SWA_HD ** -0.5) * LOG2E
    for kh in range(SWA_KV):
        rows = []
        for pr in range(4):
            c = 4 * kh + pr
            qc = _rope128(q_ref[:, c * LANES:(c + 1) * LANES], cos, sin, first_half) * scale
            rows.append(jnp.where(lo, qc, 0.0).astype(BF16))
            rows.append(jnp.where(lo, 0.0, qc).astype(BF16))
        lhs = jnp.concatenate(rows, axis=0)
        s = lax.dot_general(lhs, kall[kh], (((1,), (1,)), ((), ())), preferred_element_type=F32)
        ps, sinks2 = [], []
        for hl in range(8):
            sh = s[hl * w:(hl + 1) * w]
            f = jnp.where(keep, jnp.where(own, sh[:, w:], sh[:, :w]), -1e30)
            m = jnp.max(f, axis=-1, keepdims=True)
            p = jnp.exp2(f - m)
            ps.append(jnp.concatenate([jnp.where(own, 0.0, p), jnp.where(own, p, 0.0)], axis=1).astype(BF16))
            sinks2.append(jnp.exp2(sink_ref[8 * kh + hl] * LOG2E - m))
        o = jnp.dot(jnp.concatenate(ps, axis=0), vall[kh], preferred_element_type=F32)
        for pr in range(4):
            c = 4 * kh + pr
            halves = []
            for hl in (2 * pr, 2 * pr + 1):
                oh = o[hl * w:(hl + 1) * w]
                halves.append(oh[:, :LANES] * (1.0 / (oh[:, LANES:] + sinks2[hl])))
            gt = g_ref[:, c * LANES:(c + 1) * LANES]
            o_ref[:, c * LANES:(c + 1) * LANES] = (
                jnp.where(lo, halves[0], halves[1]) * (gt * _sigmoid(gt))).astype(o_ref.dtype)


def swa_prompt(h, sinks, cos, sin, batch, seq):
    m = h.shape[0]
    nb = seq // WINDOW
    kvw = SWA_KV * SWA_HD
    row = lambda b, n: b * nb + n
    return pl.pallas_call(
        _swa_prompt_kernel,
        grid=(batch, nb),
        in_specs=[
            pl.BlockSpec(memory_space=pltpu.SMEM),
            pl.BlockSpec((WINDOW, 2048), lambda b, n: (row(b, n), H_SQ // 2048)),
            pl.BlockSpec((WINDOW, 2048), lambda b, n: (row(b, n), H_SG // 2048)),
            pl.BlockSpec((WINDOW, kvw), lambda b, n: (row(b, n), H_SK // kvw)),
            pl.BlockSpec((WINDOW, kvw), lambda b, n: (row(b, n), H_SV // kvw)),
            pl.BlockSpec((WINDOW, LANES), lambda b, n: (n, 0)),
            pl.BlockSpec((WINDOW, LANES), lambda b, n: (n, 0)),
        ],
        out_specs=[
            pl.BlockSpec((WINDOW, 2048), lambda b, n: (row(b, n), 0)),
            pl.BlockSpec((None, WINDOW, kvw), lambda b, n: (b, 0, 0)),
            pl.BlockSpec((None, WINDOW, kvw), lambda b, n: (b, 0, 0)),
        ],
        out_shape=[
            jax.ShapeDtypeStruct((m, D_MODEL), BF16),
            jax.ShapeDtypeStruct((batch, WINDOW, kvw), F32),
            jax.ShapeDtypeStruct((batch, WINDOW, kvw), F32),
        ],
        scratch_shapes=[pltpu.VMEM((SWA_KV, WINDOW, LANES), BF16), pltpu.VMEM((SWA_KV, WINDOW, LANES), BF16)],
        compiler_params=_cparams(("arbitrary", "arbitrary")),
        name="swa_prompt",
    )(sinks, h, h, h, h, cos, sin)


def _mem_prompt_kernel(q_ref, g_ref, k_ref, v_ref, mix_ref, o_ref):
    del mix_ref
    scale = MEM_HD ** -0.5
    for h in range(MEM_H):
        sl = slice(h * MEM_HD, (h + 1) * MEM_HD)
        q = (q_ref[:, sl] * scale).astype(BF16)
        k = k_ref[:, sl].astype(BF16)
        v = v_ref[:, sl].astype(BF16)
        s = lax.dot_general(q, k, (((1,), (1,)), ((), ())), preferred_element_type=F32)
        m = jnp.max(s, axis=-1, keepdims=True)
        p = jnp.exp(s - m)
        den = jnp.sum(p, axis=-1, keepdims=True)
        o = jnp.dot(p.astype(BF16), v, preferred_element_type=F32) * (1.0 / den)
        g = g_ref[:, sl]
        o_ref[:, sl] = (o * (g * _sigmoid(g))).astype(o_ref.dtype)


def mem_attn_prompt(h, kv, mix, batch, seq, tq):
    nt = seq // tq
    w = MEM_H * MEM_HD
    row = lambda b, t: b * nt + t
    return pl.pallas_call(
        _mem_prompt_kernel,
        grid=(batch, nt),
        in_specs=[
            pl.BlockSpec((tq, w), lambda b, t: (row(b, t), H_MQ // w)),
            pl.BlockSpec((tq, w), lambda b, t: (row(b, t), H_MG // w)),
            pl.BlockSpec((MEM_LEN, w), lambda b, t: (b, 0)),
            pl.BlockSpec((MEM_LEN, w), lambda b, t: (b, 1)),
            pl.BlockSpec(memory_space=pl.ANY),
        ],
        out_specs=pl.BlockSpec((tq, w), lambda b, t: (row(b, t), 3)),
        out_shape=jax.ShapeDtypeStruct(mix.shape, mix.dtype),
        input_output_aliases={4: 0},
        compiler_params=_cparams(("parallel", "parallel")),
        name="mem_prompt",
    )(h, h, kv, kv, mix)


def _log_sigmoid(x):
    return jnp.minimum(x, 0.0) - jnp.log1p(jnp.exp(-jnp.abs(x)))


def _gla_prompt_kernel(q_ref, k_ref, v_ref, g_ref, lr_ref, wg_ref, bg_ref, gn_ref, mix_ref,
                       o_ref, so_ref, st, qs, ks_, bs_, am, qt, kt, eb, ob):
    del mix_ref
    t = pl.program_id(1)
    tb = q_ref.shape[0]
    ch = GLA_CHUNK
    nc = tb // ch
    ng = tb // SUB
    nsub = ch // SUB

    @pl.when(t == 0)
    def _():
        st[...] = jnp.zeros_like(st)

    r = lax.broadcasted_iota(jnp.int32, (tb, tb), 0)
    c = lax.broadcasted_iota(jnp.int32, (tb, tb), 1)
    same = (r // ch) == (c // ch)
    lcat = jnp.concatenate([
        jnp.where(same, jnp.where(c <= r, 1.0, 0.0), 0.0),
        jnp.where(same, 1.0, 0.0),
        jnp.where(same, jnp.where(c < (r // SUB) * SUB, 1.0, 0.0), 0.0)], axis=0)
    lr = lr_ref[...].astype(BF16)
    lane = lax.broadcasted_iota(jnp.int32, (ng, LANES), 1)
    grp = lax.broadcasted_iota(jnp.int32, (ng, LANES), 0)
    rel = lane - SUB * (grp % nsub)
    rr = lax.broadcasted_iota(jnp.int32, (ch, LANES), 0) // SUB
    cc = lax.broadcasted_iota(jnp.int32, (ch, LANES), 1)
    off_mask = jnp.where(cc >= 8 * rr * (rr - 1), jnp.where(cc < 8 * rr * (rr + 1), 1, 0), 0) > 0

    for hd in range(GLA_H):
        ks = slice(hd * GLA_DK, (hd + 1) * GLA_DK)
        vs = slice(hd * GLA_DV, (hd + 1) * GLA_DV)
        x = jnp.dot(lr, wg_ref[:, ks], preferred_element_type=F32) + bg_ref[:, ks]
        g2 = _log_sigmoid(x) * (LOG2E / GLA_TAU)
        cum = jnp.dot(lcat, g2, precision=lax.Precision.HIGHEST, preferred_element_type=F32)
        b, bl, bsb = cum[:tb], cum[tb:2 * tb], cum[2 * tb:]
        q = q_ref[:, ks] * (GLA_DK ** -0.5)
        k = k_ref[:, ks]
        qs[hd] = q
        ks_[hd] = k
        bs_[hd] = b
        qt[hd] = (q * jnp.exp2(b)).astype(BF16)
        kt[hd] = (k * jnp.exp2(bl - b)).astype(BF16)
        eb[hd] = jnp.exp2(bl)
        qh = (q * jnp.exp2(b - bsb)).astype(BF16)

        qd = [qs[hd, pl.ds(i, ng, stride=SUB), :] for i in range(SUB)]
        kd = [ks_[hd, pl.ds(i, ng, stride=SUB), :] for i in range(SUB)]
        bd = [bs_[hd, pl.ds(i, ng, stride=SUB), :] for i in range(SUB)]
        for i in range(SUB):
            a_i = jnp.zeros((ng, LANES), F32)
            for j in range(i + 1):
                col = jnp.sum(qd[i] * kd[j] * jnp.exp2(bd[i] - bd[j]), axis=-1, keepdims=True)
                a_i = jnp.where(rel == j, col, a_i)
            am[hd, pl.ds(i, ng, stride=SUB), :] = a_i

        for cidx in range(nc):
            rows = slice(cidx * ch, (cidx + 1) * ch)
            kc, bc = k[rows], b[rows]
            parts = []
            for sub in range(1, nsub):
                s_i = bsb[cidx * ch + sub * SUB:cidx * ch + sub * SUB + 1, :]
                parts.append((kc[:sub * SUB] * jnp.exp2(s_i - bc[:sub * SUB])).astype(BF16))
            parts.append(jnp.zeros((LANES - sum(p.shape[0] for p in parts), GLA_DK), BF16))
            kall = jnp.concatenate(parts, axis=0)
            sc = lax.dot_general(qh[rows], kall, (((1,), (1,)), ((), ())), preferred_element_type=F32)
            lhs = jnp.concatenate([jnp.where(off_mask, sc, 0.0).astype(BF16), am[hd, rows, :].astype(BF16)], axis=1)
            vc = v_ref[rows, vs].astype(BF16)
            rhs = jnp.concatenate(
                [vc[:s * SUB] for s in range(1, nsub)]
                + [jnp.zeros((LANES - SUB * nsub * (nsub - 1) // 2, GLA_DV), BF16), vc,
                   jnp.zeros((LANES - ch, GLA_DV), BF16)], axis=0)
            ob[hd, rows, :] = jnp.dot(lhs, rhs, preferred_element_type=F32)

    for cidx in range(nc):
        rows = slice(cidx * ch, (cidx + 1) * ch)
        for hd in range(GLA_H):
            vc = v_ref[rows, hd * GLA_DV:(hd + 1) * GLA_DV].astype(BF16)
            sb = st[hd]
            ob[hd, rows, :] += lax.dot_general(qt[hd, rows, :], sb.astype(BF16), (((1,), (1,)), ((), ())),
                                               preferred_element_type=F32)
            upd = lax.dot_general(vc, kt[hd, rows, :], (((0,), (0,)), ((), ())), preferred_element_type=F32)
            st[hd] = sb * eb[hd, cidx * ch:cidx * ch + 1, :] + upd

    for hd in range(GLA_H):
        vs = slice(hd * GLA_DV, (hd + 1) * GLA_DV)
        o = ob[hd]
        on = o * lax.rsqrt(jnp.mean(o * o, axis=-1, keepdims=True) + EPS) * gn_ref[:, vs]
        gg = g_ref[:, vs]
        o_ref[:, vs] = (on * (gg * _sigmoid(gg))).astype(o_ref.dtype)

    @pl.when(t == pl.num_programs(1) - 1)
    def _():
        for hd in range(GLA_H):
            so_ref[hd] = st[hd].T


def gla_prompt(h, wg, bg, gn, mix, batch, seq, tb):
    nt = seq // tb
    kw = GLA_H * GLA_DK
    vw = GLA_H * GLA_DV
    row = lambda b, t: b * nt + t
    return pl.pallas_call(
        _gla_prompt_kernel,
        grid=(batch, nt),
        in_specs=[
            pl.BlockSpec((tb, kw), lambda b, t: (row(b, t), H_GQ // kw)),
            pl.BlockSpec((tb, kw), lambda b, t: (row(b, t), H_GK // kw)),
            pl.BlockSpec((tb, vw), lambda b, t: (row(b, t), H_GV // vw)),
            pl.BlockSpec((tb, vw), lambda b, t: (row(b, t), H_GG // vw)),
            pl.BlockSpec((tb, LANES), lambda b, t: (row(b, t), H_GLR // LANES)),
            pl.BlockSpec((LANES, kw), lambda b, t: (0, 0)),
            pl.BlockSpec((1, kw), lambda b, t: (0, 0)),
            pl.BlockSpec((1, vw), lambda b, t: (0, 0)),
            pl.BlockSpec(memory_space=pl.ANY),
        ],
        out_specs=[
            pl.BlockSpec((tb, vw), lambda b, t: (row(b, t), 2)),
            pl.BlockSpec((None, GLA_H, GLA_DK, GLA_DV), lambda b, t: (b, 0, 0, 0)),
        ],
        out_shape=[
            jax.ShapeDtypeStruct(mix.shape, mix.dtype),
            jax.ShapeDtypeStruct((batch, GLA_H, GLA_DK, GLA_DV), F32),
        ],
        input_output_aliases={8: 0},
        scratch_shapes=[
            pltpu.VMEM((GLA_H, GLA_DV, GLA_DK), F32),
            pltpu.VMEM((GLA_H, tb, GLA_DK), F32),
            pltpu.VMEM((GLA_H, tb, GLA_DK), F32),
            pltpu.VMEM((GLA_H, tb, GLA_DK), F32),
            pltpu.VMEM((GLA_H, tb, LANES), F32),
            pltpu.VMEM((GLA_H, tb, GLA_DK), BF16),
            pltpu.VMEM((GLA_H, tb, GLA_DK), BF16),
            pltpu.VMEM((GLA_H, tb, GLA_DK), F32),
            pltpu.VMEM((GLA_H, tb, GLA_DV), F32),
        ],
        compiler_params=_cparams(("arbitrary", "arbitrary")),
        name="gla_prompt",
    )(h, h, h, h, h, wg, bg, gn, mix)


def _swa_sample_kernel(q_ref, g_ref, k_ref, v_ref, kb_ref, vb_ref, cos_ref, sin_ref, sink_ref,
                       o_ref, ko_ref, vo_ref):
    w = kb_ref.shape[0]
    cos = cos_ref[...]
    sin = sin_ref[...]
    lane1 = lax.broadcasted_iota(jnp.int32, (1, LANES), 1)
    lane = lax.broadcasted_iota(jnp.int32, (16, LANES), 1)
    rowi = lax.broadcasted_iota(jnp.int32, (w, LANES), 0)
    lo_w = lax.broadcasted_iota(jnp.int32, (w, LANES), 1) < 64
    lo = lane < 64

    kk, vv = [], []
    for p in range(2):
        sl = slice(p * LANES, (p + 1) * LANES)
        knew = _rope128(k_ref[:, sl], cos, sin, (lane1 & 32) == 0)
        vnew = v_ref[:, sl]
        kwin = jnp.where(rowi == w - 1, knew, pltpu.roll(kb_ref[:, sl], w - 1, 0))
        vwin = jnp.where(rowi == w - 1, vnew, pltpu.roll(vb_ref[:, sl], w - 1, 0))
        ko_ref[:, sl] = kwin
        vo_ref[:, sl] = vwin
        kk += [x.astype(BF16) for x in _dup_halves(kwin, lo_w)]
        vv += [x.astype(BF16) for x in _dup_halves(vwin, lo_w)]

    q = _rope128(q_ref[...], cos, sin, (lane & 32) == 0) * (SWA_HD ** -0.5)
    qq = jnp.concatenate([jnp.where(lo, q, 0.0), jnp.where(lo, 0.0, q)], axis=0)
    grp = (lax.broadcasted_iota(jnp.int32, (32, LANES), 0) % 16) // 4
    s = jnp.zeros((32, w), F32)
    for kh in range(SWA_KV):
        s = s + lax.dot_general(jnp.where(grp == kh, qq, 0.0).astype(BF16), kk[kh],
                                (((1,), (1,)), ((), ())), preferred_element_type=F32)
    sk = sink_ref[...][:, 0:1]
    m = jnp.maximum(jnp.max(s, axis=-1, keepdims=True), sk)
    p = jnp.exp(s - m)
    den = jnp.sum(p, axis=-1, keepdims=True) + jnp.exp(sk - m)
    o = jnp.zeros((32, LANES), F32)
    for kh in range(SWA_KV):
        o = o + jnp.dot(jnp.where(grp == kh, p, 0.0).astype(BF16), vv[kh], preferred_element_type=F32)
    o = o * (1.0 / den)
    g = g_ref[...]
    o_ref[...] = jnp.where(lo, o[0:16], o[16:32]) * (g * _sigmoid(g))


def swa_sample(q, g, k, v, kbuf, vbuf, cos, sin, sink_rows):
    bd, wb, kvw = kbuf.shape
    return pl.pallas_call(
        _swa_sample_kernel,
        grid=(bd,),
        in_specs=[
            pl.BlockSpec((None, 16, LANES), lambda b: (b, 0, 0)),
            pl.BlockSpec((None, 16, LANES), lambda b: (b, 0, 0)),
            pl.BlockSpec((None, 1, kvw), lambda b: (b, 0, 0)),
            pl.BlockSpec((None, 1, kvw), lambda b: (b, 0, 0)),
            pl.BlockSpec((None, wb, kvw), lambda b: (b, 0, 0)),
            pl.BlockSpec((None, wb, kvw), lambda b: (b, 0, 0)),
            pl.BlockSpec((1, LANES), lambda b: (0, 0)),
            pl.BlockSpec((1, LANES), lambda b: (0, 0)),
            pl.BlockSpec((32, LANES), lambda b: (0, 0)),
        ],
        out_specs=[
            pl.BlockSpec((None, 16, LANES), lambda b: (b, 0, 0)),
            pl.BlockSpec((None, wb, kvw), lambda b: (b, 0, 0)),
            pl.BlockSpec((None, wb, kvw), lambda b: (b, 0, 0)),
        ],
        out_shape=[
            jax.ShapeDtypeStruct((bd, 16, LANES), F32),
            jax.ShapeDtypeStruct((bd, wb, kvw), F32),
            jax.ShapeDtypeStruct((bd, wb, kvw), F32),
        ],
        compiler_params=_cparams(("parallel",)),
        name="swa_sample",
    )(q, g, k, v, kbuf, vbuf, cos, sin, sink_rows)


def _mem_sample_kernel(q_ref, g_ref, k_ref, v_ref, o_ref):
    q = q_ref[...] * (MEM_HD ** -0.5)
    rowh = lax.broadcasted_iota(jnp.int32, (8, MEM_HD), 0)
    s = jnp.zeros((8, MEM_LEN), F32)
    for h in range(MEM_H):
        sl = slice(h * MEM_HD, (h + 1) * MEM_HD)
        s = s + lax.dot_general(jnp.where(rowh == h, q, 0.0).astype(BF16), k_ref[:, sl].astype(BF16),
                                (((1,), (1,)), ((), ())), preferred_element_type=F32)
    m = jnp.max(s, axis=-1, keepdims=True)
    p = jnp.exp(s - m)
    den = jnp.sum(p, axis=-1, keepdims=True)
    o = jnp.zeros((8, MEM_HD), F32)
    for h in range(MEM_H):
        sl = slice(h * MEM_HD, (h + 1) * MEM_HD)
        o = o + jnp.dot(jnp.where(rowh == h, p, 0.0).astype(BF16), v_ref[:, sl].astype(BF16),
                        preferred_element_type=F32)
    g = g_ref[...]
    o_ref[...] = o * (1.0 / den) * (g * _sigmoid(g))


def mem_sample(q, g, k, v):
    bd = q.shape[0]
    w = MEM_H * MEM_HD
    return pl.pallas_call(
        _mem_sample_kernel,
        grid=(bd,),
        in_specs=[
            pl.BlockSpec((None, 8, MEM_HD), lambda b: (b, 0, 0)),
            pl.BlockSpec((None, 8, MEM_HD), lambda b: (b, 0, 0)),
            pl.BlockSpec((None, MEM_LEN, w), lambda b: (b, 0, 0)),
            pl.BlockSpec((None, MEM_LEN, w), lambda b: (b, 0, 0)),
        ],
        out_specs=pl.BlockSpec((None, 8, MEM_HD), lambda b: (b, 0, 0)),
        out_shape=jax.ShapeDtypeStruct((bd, 8, MEM_HD), F32),
        compiler_params=_cparams(("parallel",)),
        name="mem_sample",
    )(q, g, k, v)


def _gla_sample_kernel(q_ref, k_ref, v_ref, g_ref, lr_ref, wg_ref, bg_ref, gn_ref, s_ref, o_ref, so_ref):
    rowk = lax.broadcasted_iota(jnp.int32, (8, GLA_DK), 0)
    lr = jnp.broadcast_to(lr_ref[...], (8, LANES)).astype(BF16)
    xa = jnp.dot(lr, wg_ref[...], preferred_element_type=F32) + bg_ref[...]
    x = jnp.zeros((8, GLA_DK), F32)
    for hd in range(GLA_H):
        x = jnp.where(rowk == hd, xa[:, hd * GLA_DK:(hd + 1) * GLA_DK], x)
    gl = _log_sigmoid(x) * (1.0 / GLA_TAU)
    eg = jnp.exp(gl)
    q = q_ref[...] * (GLA_DK ** -0.5)
    k = k_ref[...]
    v = v_ref[...]
    qe = q * eg
    eye = lax.broadcasted_iota(jnp.int32, (GLA_DK, GLA_DK), 0) == lax.broadcasted_iota(jnp.int32, (GLA_DK, GLA_DK), 1)

    def col(row):
        return jnp.sum(jnp.where(eye, jnp.broadcast_to(row, (GLA_DK, GLA_DK)), 0.0), axis=-1, keepdims=True)

    o = jnp.sum(q * k, axis=-1, keepdims=True) * v
    for hd in range(GLA_H):
        s0 = s_ref[hd]
        o = o + jnp.dot(jnp.where(rowk == hd, qe, 0.0).astype(BF16), s0.astype(BF16), preferred_element_type=F32)
        so_ref[hd] = s0 * col(eg[hd:hd + 1, :]) + col(k[hd:hd + 1, :]) * v[hd:hd + 1, :]
    on = o * lax.rsqrt(jnp.mean(o * o, axis=-1, keepdims=True) + EPS) * gn_ref[...]
    gg = g_ref[...]
    o_ref[...] = on * (gg * _sigmoid(gg))


def gla_sample(q, k, v, g, lr, wg, bg, gn, state):
    bd = q.shape[0]
    kw = GLA_H * GLA_DK
    return pl.pallas_call(
        _gla_sample_kernel,
        grid=(bd,),
        in_specs=[
            pl.BlockSpec((None, 8, GLA_DK), lambda b: (b, 0, 0)),
            pl.BlockSpec((None, 8, GLA_DK), lambda b: (b, 0, 0)),
            pl.BlockSpec((None, 8, GLA_DV), lambda b: (b, 0, 0)),
            pl.BlockSpec((None, 8, GLA_DV), lambda b: (b, 0, 0)),
            pl.BlockSpec((None, 1, LANES), lambda b: (b, 0, 0)),
            pl.BlockSpec((LANES, kw), lambda b: (0, 0)),
            pl.BlockSpec((1, kw), lambda b: (0, 0)),
            pl.BlockSpec((8, GLA_DV), lambda b: (0, 0)),
            pl.BlockSpec((None, GLA_H, GLA_DK, GLA_DV), lambda b: (b, 0, 0, 0)),
        ],
        out_specs=[
            pl.BlockSpec((None, 8, GLA_DV), lambda b: (b, 0, 0)),
            pl.BlockSpec((None, GLA_H, GLA_DK, GLA_DV), lambda b: (b, 0, 0, 0)),
        ],
        out_shape=[
            jax.ShapeDtypeStruct((bd, 8, GLA_DV), F32),
            jax.ShapeDtypeStruct(state.shape, F32),
        ],
        compiler_params=_cparams(("parallel",)),
        name="gla_sample",
    )(q, k, v, g, lr, wg, bg, gn, state)


def _heads8(x, width):
    bd = x.shape[0]
    x = x.reshape(bd, -1, width)
    return jnp.pad(x, ((0, 0), (0, 8 - x.shape[1]), (0, 0)))


def kernel(x_prompt, mem_prompt, x_sample, cache_swa_k, cache_swa_v, state_gla, cache_mem_k, cache_mem_v,
           norm_g, w_in, attn_sinks, gla_w_gate, gla_b_gate, gla_norm_g, mem_norm_g, w_mem_kv, w_out,
           final_norm_g):
    batch, seq, d = x_prompt.shape
    bd = x_sample.shape[0]
    m = batch * seq
    kvw = SWA_KV * SWA_HD
    memw = MEM_H * MEM_HD

    cos_p, sin_p = _rope_tables(jnp.arange(seq))
    cos_s, sin_s = _rope_tables(PAST_LEN + jnp.arange(1))
    perm = np.array([2 * (r % 16) + r // 16 for r in range(32)])
    w_in_t = jnp.transpose(w_in, (0, 2, 1))

    xp = x_prompt.reshape(m, d)
    xs = x_sample.reshape(bd, d)
    memx = mem_prompt.reshape(batch * MEM_LEN, d)

    outs = {k: [] for k in ("kp", "vp", "sp", "mkp", "mvp", "ks", "vs", "ss")}
    for l in range(DEPTH):
        wg = jnp.pad(gla_w_gate[l], ((0, LANES - GLA_RANK), (0, 0))).astype(BF16)
        bg = gla_b_gate[l].reshape(1, -1)
        gn = gla_norm_g[l].reshape(1, -1)

        kv = project(rmsnorm(memx, mem_norm_g[l], BF16, 256), w_mem_kv, l, name="mem_kv")
        xn = rmsnorm(xp, norm_g[l], BF16, 256)
        xns = rmsnorm(xs, norm_g[l], BF16, bd)
        h, hs = project(xn, w_in_t, l, n=H_END, tile_rows=IN_TILE_SRC, a_small=xns, name="in_proj")

        mix, kb, vb = swa_prompt(h, attn_sinks[l], cos_p, sin_p, batch, seq)
        mix, sp = gla_prompt(h, wg, bg, gn, mix, batch, seq, 256)
        mix = mem_attn_prompt(h, kv, mix, batch, seq, 512)
        outs["kp"].append(kb.reshape(batch, WINDOW, SWA_KV, SWA_HD))
        outs["vp"].append(vb.reshape(batch, WINDOW, SWA_KV, SWA_HD))
        outs["sp"].append(sp)
        outs["mkp"].append(kv[:, :memw].reshape(batch, MEM_LEN, MEM_H, MEM_HD))
        outs["mvp"].append(kv[:, memw:].reshape(batch, MEM_LEN, MEM_H, MEM_HD))

        wb = cache_swa_k.shape[2]
        o_swa, kbs, vbs = swa_sample(
            hs[:, H_SQ:H_SQ + 2048].reshape(bd, 16, LANES), hs[:, H_SG:H_SG + 2048].reshape(bd, 16, LANES),
            hs[:, H_SK:H_SK + kvw].reshape(bd, 1, kvw), hs[:, H_SV:H_SV + kvw].reshape(bd, 1, kvw),
            cache_swa_k[l].reshape(bd, wb, kvw), cache_swa_v[l].reshape(bd, wb, kvw),
            cos_s, sin_s, jnp.broadcast_to(attn_sinks[l][perm][:, None], (32, LANES)))
        o_gla, ss = gla_sample(
            _heads8(hs[:, H_GQ:H_GQ + 512], GLA_DK), _heads8(hs[:, H_GK:H_GK + 512], GLA_DK),
            _heads8(hs[:, H_GV:H_GV + 1024], GLA_DV), _heads8(hs[:, H_GG:H_GG + 1024], GLA_DV),
            hs[:, H_GLR:H_GLR + LANES].reshape(bd, 1, LANES), wg, bg,
            jnp.pad(gla_norm_g[l].reshape(GLA_H, GLA_DV), ((0, 4), (0, 0))), state_gla[l])
        o_mem = mem_sample(
            _heads8(hs[:, H_MQ:H_MQ + memw], MEM_HD), _heads8(hs[:, H_MG:H_MG + memw], MEM_HD),
            cache_mem_k[l].reshape(bd, MEM_LEN, memw), cache_mem_v[l].reshape(bd, MEM_LEN, memw))
        mix_s = jnp.concatenate([o_swa.reshape(bd, 2048), o_gla[:, :GLA_H].reshape(bd, 1024),
                                 o_mem[:, :MEM_H].reshape(bd, memw)], axis=1).astype(BF16)
        outs["ks"].append(kbs.reshape(bd, wb, SWA_KV, SWA_HD))
        outs["vs"].append(vbs.reshape(bd, wb, SWA_KV, SWA_HD))
        outs["ss"].append(ss)

        xp, xs = project(mix, w_out, l, a_small=mix_s, res=xp, res_small=xs, name="out_proj")

    y_prompt = rmsnorm(xp, final_norm_g, F32, 256).reshape(batch, seq, d)
    y_sample = rmsnorm(xs, final_norm_g, F32, bd).reshape(bd, 1, d)
    st = lambda k: jnp.stack(outs[k])
    return (y_prompt, y_sample, st("kp"), st("vp"), st("sp"), st("mkp"), st("mvp"), st("ks"), st("vs"), st("ss"))
```

```python
import functools
import math

import jax
import jax.numpy as jnp
import numpy as np
from jax import lax
from jax.experimental import pallas as pl
from jax.experimental.pallas import tpu as pltpu

F32 = jnp.float32
BF16 = jnp.bfloat16

D_MODEL = 4096
DEPTH = 2
EPS = 1e-6
PAST_LEN = 16384
WINDOW = 128
ROPE_THETA = 10000.0
SWA_HD = 64
SWA_HQ = 32
SWA_KV = 4
GLA_H = 4
GLA_DK = 128
GLA_DV = 256
GLA_RANK = 16
GLA_TAU = 16.0
MEM_LEN = 256
MEM_H = 4
MEM_HD = 256
LANES = 128
GLA_CHUNK = 64
SUB = 16
LOG2E = math.log2(math.e)

TM = 1024
TN = 512
H_SQ, H_SG, H_GV, H_GG, H_MQ, H_MG, H_GQ, H_GK, H_SK, H_SV, H_GLR, H_END = (
    0, 2048, 4096, 5120, 6144, 7168, 8192, 8704, 9216, 9472, 9728, 9856)
IN_TILE_SRC = (0, 512, 1024, 1536, 2560, 3072, 3584, 4096, 5632, 6144, 6672, 7184, 7696, 8208, 8720, 9232,
               4608, 5120, 2048, 6656)
VMEM_LIMIT = 56 * 1024 * 1024


def _cparams(sem):
    return pltpu.CompilerParams(dimension_semantics=sem, vmem_limit_bytes=VMEM_LIMIT)


def _sigmoid(x):
    return 1.0 / (1.0 + jnp.exp(-x))


def _rmsnorm_kernel(x_ref, g_ref, o_ref):
    x = x_ref[...]
    ms = jnp.mean(x * x, axis=-1, keepdims=True)
    o_ref[...] = (x * lax.rsqrt(ms + EPS) * g_ref[...]).astype(o_ref.dtype)


def rmsnorm(x, g, out_dtype, tm):
    m, d = x.shape
    return pl.pallas_call(
        _rmsnorm_kernel,
        grid=(m // tm,),
        in_specs=[pl.BlockSpec((tm, d), lambda i: (i, 0)), pl.BlockSpec((1, d), lambda i: (0, 0))],
        out_specs=pl.BlockSpec((tm, d), lambda i: (i, 0)),
        out_shape=jax.ShapeDtypeStruct((m, d), out_dtype),
        compiler_params=_cparams(("parallel",)),
        name="rmsnorm",
    )(x, g.reshape(1, d))


def _proj_kernel(*refs, has_small, has_res, tn, last_valid, w_rows):
    refs = list(refs)
    if w_rows:
        refs.pop(0)
    a_ref = refs.pop(0)
    as_ref = refs.pop(0) if has_small else None
    w_ref = refs.pop(0)
    r_ref = refs.pop(0) if has_res else None
    rs_ref = refs.pop(0) if (has_res and has_small) else None
    o_ref = refs.pop(0)
    os_ref = refs.pop(0) if has_small else None
    i = pl.program_id(0)
    j = pl.program_id(1)
    nj = pl.num_programs(1)

    def compute(width):
        if w_rows:
            w = w_ref[:width, :].astype(BF16)
            mm = lambda a: lax.dot_general(a, w, (((1,), (1,)), ((), ())), preferred_element_type=F32)
        else:
            w = w_ref[:, :width].astype(BF16)
            mm = lambda a: jnp.dot(a, w, preferred_element_type=F32)
        acc = mm(a_ref[...])
        if has_res:
            acc = acc + r_ref[:, :width]
        o_ref[:, :width] = acc
        if has_small:
            @pl.when(i == 0)
            def _():
                accs = mm(as_ref[...])
                if has_res:
                    accs = accs + rs_ref[:, :width]
                os_ref[:, :width] = accs

    if last_valid == tn:
        compute(tn)
    else:
        pl.when(j < nj - 1)(lambda: compute(tn))
        pl.when(j == nj - 1)(lambda: compute(last_valid))


def project(a, w3, layer, n=None, tile_rows=None, a_small=None, res=None, res_small=None, name="proj"):
    m, k = a.shape
    w_rows = tile_rows is not None
    if n is None:
        n = w3.shape[2]
    tm = min(TM, m)
    nj = pl.cdiv(n, TN)
    last_valid = n - (nj - 1) * TN
    has_small = a_small is not None
    has_res = res is not None
    small_idx = lambda i, j, *_: (0, jnp.where(i == 0, j, nj - 1))
    in_specs = [pl.BlockSpec((tm, k), lambda i, j, *_: (i, 0))]
    args = [a]
    if has_small:
        ms = a_small.shape[0]
        in_specs.append(pl.BlockSpec((ms, k), lambda i, j, *_: (0, 0)))
        args.append(a_small)
    if w_rows:
        in_specs.append(pl.BlockSpec((None, pl.Element(TN), pl.Element(k)),
                                     lambda i, j, offs: (layer, pl.multiple_of(offs[j], 16), 0)))
    else:
        in_specs.append(pl.BlockSpec((None, k, TN), lambda i, j: (layer, 0, j)))
    args.append(w3)
    if has_res:
        in_specs.append(pl.BlockSpec((tm, TN), lambda i, j, *_: (i, j)))
        args.append(res)
        if has_small:
            in_specs.append(pl.BlockSpec((ms, TN), small_idx))
            args.append(res_small)
    out_specs = [pl.BlockSpec((tm, TN), lambda i, j, *_: (i, j))]
    out_shape = [jax.ShapeDtypeStruct((m, n), F32)]
    if has_small:
        out_specs.append(pl.BlockSpec((ms, TN), small_idx))
        out_shape.append(jax.ShapeDtypeStruct((ms, n), F32))
    kern = functools.partial(_proj_kernel, has_small=has_small, has_res=has_res, tn=TN, last_valid=last_valid,
                             w_rows=w_rows)
    if w_rows:
        grid_spec = pltpu.PrefetchScalarGridSpec(num_scalar_prefetch=1, grid=(m // tm, nj), in_specs=in_specs,
                                                 out_specs=out_specs)
        args = [jnp.asarray(tile_rows, jnp.int32)] + args
    else:
        grid_spec = pl.GridSpec(grid=(m // tm, nj), in_specs=in_specs, out_specs=out_specs)
    outs = pl.pallas_call(
        kern,
        grid_spec=grid_spec,
        out_shape=out_shape,
        compiler_params=_cparams(("arbitrary", "arbitrary")),
        name=name,
    )(*args)
    return outs if has_small else outs[0]


def _rope_tables(pos):
    half = SWA_HD // 2
    inv = ROPE_THETA ** (-jnp.arange(half, dtype=F32) / half)
    ang = pos.astype(F32)[:, None] * inv[None, :]
    cos, sin = jnp.cos(ang), jnp.sin(ang)
    return jnp.concatenate([cos, cos, cos, cos], axis=-1), jnp.concatenate([-sin, sin, -sin, sin], axis=-1)


def _rope128(x, cos, sin, first_half):
    partner = jnp.where(first_half, pltpu.roll(x, LANES - 32, 1), pltpu.roll(x, 32, 1))
    return x * cos + partner * sin


def _dup_halves(x, lo):
    r = pltpu.roll(x, 64, 1)
    return jnp.where(lo, x, r), jnp.where(lo, r, x)


def _swa_prompt_kernel(sink_ref, q_ref, g_ref, k_ref, v_ref, cos_ref, sin_ref,
                       o_ref, ko_ref, vo_ref, kprev, vprev):
    n = pl.program_id(1)

    @pl.when(n == 0)
    def _():
        kprev[...] = jnp.zeros_like(kprev)
        vprev[...] = jnp.zeros_like(vprev)

    w = WINDOW
    cos = cos_ref[...]
    sin = sin_ref[...]
    lane = lax.broadcasted_iota(jnp.int32, (w, LANES), 1)
    rowi = lax.broadcasted_iota(jnp.int32, (w, LANES), 0)
    first_half = (lane & 32) == 0
    lo = lane < 64
    own = lane <= rowi
    keep = jnp.where(own, 1, jnp.where(n > 0, 1, 0)) > 0
    ones = jnp.ones((2 * w, LANES), BF16)

    kall, vall = [], []
    kcur, vcur = [], []
    for p in range(2):
        kp = _rope128(k_ref[:, p * LANES:(p + 1) * LANES], cos, sin, first_half)
        vp = v_ref[:, p * LANES:(p + 1) * LANES]
        ko_ref[:, p * LANES:(p + 1) * LANES] = kp
        vo_ref[:, p * LANES:(p + 1) * LANES] = vp
        kcur += [x.astype(BF16) for x in _dup_halves(kp, lo)]
        vcur += [x.astype(BF16) for x in _dup_halves(vp, lo)]
    for kh in range(SWA_KV):
        kall.append(jnp.concatenate([kprev[kh], kcur[kh]], axis=0))
        vv = jnp.concatenate([vprev[kh], vcur[kh]], axis=0)
        vall.append(jnp.concatenate([vv, ones], axis=1))
    for kh in range(SWA_KV):
        kprev[kh] = kcur[kh]
        vprev[kh] = vcur[kh]

    scale = (SWA_HD ** -0.5) * LOG2E
    slabs = []
    for c in range(SWA_HQ // 2):
        qc = _rope128(q_ref[:, c * LANES:(c + 1) * LANES], cos, sin, first_half) * scale
        slabs.append(jnp.where(lo, qc, 0.0).astype(BF16))
        slabs.append(jnp.where(lo, 0.0, qc).astype(BF16))
    scores = [lax.dot_general(jnp.concatenate(slabs[8 * kh:8 * kh + 8], axis=0), kall[kh],
                              (((1,), (1,)), ((), ())), preferred_element_type=F32)
              for kh in range(SWA_KV)]
    folded = []
    for hq in range(SWA_HQ):
        sh = scores[hq // 8][(hq % 8) * w:(hq % 8 + 1) * w]
        folded.append(jnp.where(keep, jnp.where(own, sh[:, w:], sh[:, :w]), -1e30))
    maxes = [jnp.max(f, axis=-1, keepdims=True) for f in folded]
    probs = [jnp.exp2(f - m) for f, m in zip(folded, maxes)]
    sinks2 = [jnp.exp2(sink_ref[hq] * LOG2E - maxes[hq]) for hq in range(SWA_HQ)]
    ps = [jnp.concatenate([jnp.where(own, 0.0, p), jnp.where(own, p, 0.0)], axis=1).astype(BF16) for p in probs]
    outs = [jnp.dot(jnp.concatenate(ps[8 * kh:8 * kh + 8], axis=0), vall[kh], preferred_element_type=F32)
            for kh in range(SWA_KV)]
    normed = []
    for hq in range(SWA_HQ):
        oh = outs[hq // 8][(hq % 8) * w:(hq % 8 + 1) * w]
        normed.append(oh[:, :LANES] * (1.0 / (oh[:, LANES:] + sinks2[hq])))
    for c in range(SWA_HQ // 2):
        gt = g_ref[:, c * LANES:(c + 1) * LANES]
        o_ref[:, c * LANES:(c + 1) * LANES] = (
            jnp.where(lo, normed[2 * c], normed[2 * c + 1]) * (gt * _sigmoid(gt))).astype(o_ref.dtype)


def swa_prompt(h, sinks, cos, sin, batch, seq):
    m = h.shape[0]
    nb = seq // WINDOW
    kvw = SWA_KV * SWA_HD
    row = lambda b, n: b * nb + n
    return pl.pallas_call(
        _swa_prompt_kernel,
        grid=(batch, nb),
        in_specs=[
            pl.BlockSpec(memory_space=pltpu.SMEM),
            pl.BlockSpec((WINDOW, 2048), lambda b, n: (row(b, n), H_SQ // 2048)),
            pl.BlockSpec((WINDOW, 2048), lambda b, n: (row(b, n), H_SG // 2048)),
            pl.BlockSpec((WINDOW, kvw), lambda b, n: (row(b, n), H_SK // kvw)),
            pl.BlockSpec((WINDOW, kvw), lambda b, n: (row(b, n), H_SV // kvw)),
            pl.BlockSpec((WINDOW, LANES), lambda b, n: (n, 0)),
            pl.BlockSpec((WINDOW, LANES), lambda b, n: (n, 0)),
        ],
        out_specs=[
            pl.BlockSpec((WINDOW, 2048), lambda b, n: (row(b, n), 0)),
            pl.BlockSpec((None, WINDOW, kvw), lambda b, n: (b, 0, 0)),
            pl.BlockSpec((None, WINDOW, kvw), lambda b, n: (b, 0, 0)),
        ],
        out_shape=[
            jax.ShapeDtypeStruct((m, D_MODEL), BF16),
            jax.ShapeDtypeStruct((batch, WINDOW, kvw), F32),
            jax.ShapeDtypeStruct((batch, WINDOW, kvw), F32),
        ],
        scratch_shapes=[pltpu.VMEM((SWA_KV, WINDOW, LANES), BF16), pltpu.VMEM((SWA_KV, WINDOW, LANES), BF16)],
        compiler_params=_cparams(("arbitrary", "arbitrary")),
        name="swa_prompt",
    )(sinks, h, h, h, h, cos, sin)


def _mem_prompt_kernel(q_ref, g_ref, k_ref, v_ref, mix_ref, o_ref):
    del mix_ref
    scale = MEM_HD ** -0.5
    for h in range(MEM_H):
        sl = slice(h * MEM_HD, (h + 1) * MEM_HD)
        q = (q_ref[:, sl] * scale).astype(BF16)
        k = k_ref[:, sl].astype(BF16)
        v = v_ref[:, sl].astype(BF16)
        s = lax.dot_general(q, k, (((1,), (1,)), ((), ())), preferred_element_type=F32)
        m = jnp.max(s, axis=-1, keepdims=True)
        p = jnp.exp(s - m)
        den = jnp.sum(p, axis=-1, keepdims=True)
        o = jnp.dot(p.astype(BF16), v, preferred_element_type=F32) * (1.0 / den)
        g = g_ref[:, sl]
        o_ref[:, sl] = (o * (g * _sigmoid(g))).astype(o_ref.dtype)


def mem_attn_prompt(h, kv, mix, batch, seq, tq):
    nt = seq // tq
    w = MEM_H * MEM_HD
    row = lambda b, t: b * nt + t
    return pl.pallas_call(
        _mem_prompt_kernel,
        grid=(batch, nt),
        in_specs=[
            pl.BlockSpec((tq, w), lambda b, t: (row(b, t), H_MQ // w)),
            pl.BlockSpec((tq, w), lambda b, t: (row(b, t), H_MG // w)),
            pl.BlockSpec((MEM_LEN, w), lambda b, t: (b, 0)),
            pl.BlockSpec((MEM_LEN, w), lambda b, t: (b, 1)),
            pl.BlockSpec(memory_space=pl.ANY),
        ],
        out_specs=pl.BlockSpec((tq, w), lambda b, t: (row(b, t), 3)),
        out_shape=jax.ShapeDtypeStruct(mix.shape, mix.dtype),
        input_output_aliases={4: 0},
        compiler_params=_cparams(("parallel", "parallel")),
        name="mem_prompt",
    )(h, h, kv, kv, mix)


def _log_sigmoid(x):
    return jnp.minimum(x, 0.0) - jnp.log1p(jnp.exp(-jnp.abs(x)))


def _gla_prompt_kernel(q_ref, k_ref, v_ref, g_ref, lr_ref, wg_ref, bg_ref, gn_ref, lc_ref, mix_ref,
                       o_ref, so_ref, st, qs, ks_, bs_, am, qt, kt, qh, eb, ob):
    del mix_ref
    t = pl.program_id(1)
    tb = q_ref.shape[0]
    ch = GLA_CHUNK
    nc = tb // ch
    ng = tb // SUB
    nsub = ch // SUB
    kw = GLA_H * GLA_DK
    hk = lambda hd: slice(hd * GLA_DK, (hd + 1) * GLA_DK)
    hv = lambda hd: slice(hd * GLA_DV, (hd + 1) * GLA_DV)

    @pl.when(t == 0)
    def _():
        st[...] = jnp.zeros_like(st)

    lane = lax.broadcasted_iota(jnp.int32, (ng, LANES), 1)
    grp = lax.broadcasted_iota(jnp.int32, (ng, LANES), 0)
    rel = lane - SUB * (grp % nsub)
    rr = lax.broadcasted_iota(jnp.int32, (ch, LANES), 0) // SUB
    cc = lax.broadcasted_iota(jnp.int32, (ch, LANES), 1)
    off_mask = jnp.where(cc >= 8 * rr * (rr - 1), jnp.where(cc < 8 * rr * (rr + 1), 1, 0), 0) > 0

    x = jnp.dot(lr_ref[...].astype(BF16), wg_ref[...], preferred_element_type=F32) + bg_ref[...]
    g2 = _log_sigmoid(x) * (LOG2E / GLA_TAU)
    g_hi = g2.astype(BF16)
    r1 = g2 - g_hi.astype(F32)
    g_mid = r1.astype(BF16)
    g_lo = (r1 - g_mid.astype(F32)).astype(BF16)
    cum = jnp.dot(lc_ref[...], jnp.concatenate([g_hi, g_mid, g_lo], axis=1), preferred_element_type=F32)
    cum = cum[:, :kw] + cum[:, kw:2 * kw] + cum[:, 2 * kw:]
    b, bl, bsb = cum[:tb], cum[tb:2 * tb], cum[2 * tb:]
    q = q_ref[...] * (GLA_DK ** -0.5)
    k = k_ref[...]
    for hd in range(GLA_H):
        qs[hd] = q[:, hk(hd)]
        ks_[hd] = k[:, hk(hd)]
        bs_[hd] = b[:, hk(hd)]
    qt[...] = (q * jnp.exp2(b)).astype(BF16)
    kt[...] = (k * jnp.exp2(bl - b)).astype(BF16)
    eb[...] = jnp.exp2(bl)
    qh[...] = (q * jnp.exp2(b - bsb)).astype(BF16)

    def rows_i(ref, i):
        return jnp.concatenate([ref[hd, pl.ds(i, ng, stride=SUB), :] for hd in range(GLA_H)], axis=1)

    qd = [rows_i(qs, i) for i in range(SUB)]
    kd = [rows_i(ks_, i) for i in range(SUB)]
    bd = [rows_i(bs_, i) for i in range(SUB)]
    for i in range(SUB):
        a_i = [jnp.zeros((ng, LANES), F32) for _ in range(GLA_H)]
        for j in range(i + 1):
            tt = qd[i] * kd[j] * jnp.exp2(bd[i] - bd[j])
            for hd in range(GLA_H):
                col = jnp.sum(tt[:, hk(hd)], axis=-1, keepdims=True)
                a_i[hd] = jnp.where(rel == j, col, a_i[hd])
        for hd in range(GLA_H):
            am[hd, pl.ds(i, ng, stride=SUB), :] = a_i[hd]

    kalls = []
    for cidx in range(nc):
        r0 = cidx * ch
        kc, bc = k[r0:r0 + ch], b[r0:r0 + ch]
        parts = []
        for sub in range(1, nsub):
            s_i = bsb[r0 + sub * SUB:r0 + sub * SUB + 1, :]
            parts.append((kc[:sub * SUB] * jnp.exp2(s_i - bc[:sub * SUB])).astype(BF16))
        parts.append(jnp.zeros((LANES - sum(p.shape[0] for p in parts), kw), BF16))
        kalls.append(jnp.concatenate(parts, axis=0))
    scs = {}
    for cidx in range(nc):
        for hd in range(GLA_H):
            scs[cidx, hd] = lax.dot_general(qh[cidx * ch:(cidx + 1) * ch, hk(hd)], kalls[cidx][:, hk(hd)],
                                            (((1,), (1,)), ((), ())), preferred_element_type=F32)
    for cidx in range(nc):
        rows = slice(cidx * ch, (cidx + 1) * ch)
        for hd in range(GLA_H):
            lhs = jnp.concatenate([jnp.where(off_mask, scs[cidx, hd], 0.0).astype(BF16),
                                   am[hd, rows, :].astype(BF16)], axis=1)
            vc = v_ref[rows, hv(hd)].astype(BF16)
            rhs = jnp.concatenate(
                [vc[:s * SUB] for s in range(1, nsub)]
                + [jnp.zeros((LANES - SUB * nsub * (nsub - 1) // 2, GLA_DV), BF16), vc,
                   jnp.zeros((LANES - ch, GLA_DV), BF16)], axis=0)
            ob[hd, rows, :] = jnp.dot(lhs, rhs, preferred_element_type=F32)

    for cidx in range(nc):
        rows = slice(cidx * ch, (cidx + 1) * ch)
        for hd in range(GLA_H):
            vc = v_ref[rows, hv(hd)].astype(BF16)
            sb = st[hd]
            ob[hd, rows, :] += lax.dot_general(qt[rows, hk(hd)], sb.astype(BF16), (((1,), (1,)), ((), ())),
                                               preferred_element_type=F32)
            upd = lax.dot_general(vc, kt[rows, hk(hd)], (((0,), (0,)), ((), ())), preferred_element_type=F32)
            st[hd] = sb * eb[cidx * ch:cidx * ch + 1, hk(hd)] + upd

    for hd in range(GLA_H):
        vs = slice(hd * GLA_DV, (hd + 1) * GLA_DV)
        o = ob[hd]
        on = o * lax.rsqrt(jnp.mean(o * o, axis=-1, keepdims=True) + EPS) * gn_ref[:, vs]
        gg = g_ref[:, vs]
        o_ref[:, vs] = (on * (gg * _sigmoid(gg))).astype(o_ref.dtype)

    @pl.when(t == pl.num_programs(1) - 1)
    def _():
        for hd in range(GLA_H):
            so_ref[hd] = st[hd].T


def gla_prompt(h, wg, bg, gn, mix, batch, seq, tb):
    nt = seq // tb
    kw = GLA_H * GLA_DK
    vw = GLA_H * GLA_DV
    row = lambda b, t: b * nt + t
    r = np.arange(tb)[:, None]
    c = np.arange(tb)[None, :]
    same = (r // GLA_CHUNK) == (c // GLA_CHUNK)
    lcat = jnp.asarray(np.concatenate([same & (c <= r), same & (c >= 0), same & (c < (r // SUB) * SUB)], axis=0),
                       BF16)
    return pl.pallas_call(
        _gla_prompt_kernel,
        grid=(batch, nt),
        in_specs=[
            pl.BlockSpec((tb, kw), lambda b, t: (row(b, t), H_GQ // kw)),
            pl.BlockSpec((tb, kw), lambda b, t: (row(b, t), H_GK // kw)),
            pl.BlockSpec((tb, vw), lambda b, t: (row(b, t), H_GV // vw)),
            pl.BlockSpec((tb, vw), lambda b, t: (row(b, t), H_GG // vw)),
            pl.BlockSpec((tb, LANES), lambda b, t: (row(b, t), H_GLR // LANES)),
            pl.BlockSpec((LANES, kw), lambda b, t: (0, 0)),
            pl.BlockSpec((1, kw), lambda b, t: (0, 0)),
            pl.BlockSpec((1, vw), lambda b, t: (0, 0)),
            pl.BlockSpec((3 * tb, tb), lambda b, t: (0, 0)),
            pl.BlockSpec(memory_space=pl.ANY),
        ],
        out_specs=[
            pl.BlockSpec((tb, vw), lambda b, t: (row(b, t), 2)),
            pl.BlockSpec((None, GLA_H, GLA_DK, GLA_DV), lambda b, t: (b, 0, 0, 0)),
        ],
        out_shape=[
            jax.ShapeDtypeStruct(mix.shape, mix.dtype),
            jax.ShapeDtypeStruct((batch, GLA_H, GLA_DK, GLA_DV), F32),
        ],
        input_output_aliases={9: 0},
        scratch_shapes=[
            pltpu.VMEM((GLA_H, GLA_DV, GLA_DK), F32),
            pltpu.VMEM((GLA_H, tb, GLA_DK), F32),
            pltpu.VMEM((GLA_H, tb, GLA_DK), F32),
            pltpu.VMEM((GLA_H, tb, GLA_DK), F32),
            pltpu.VMEM((GLA_H, tb, LANES), F32),
            pltpu.VMEM((tb, kw), BF16),
            pltpu.VMEM((tb, kw), BF16),
            pltpu.VMEM((tb, kw), BF16),
            pltpu.VMEM((tb, kw), F32),
            pltpu.VMEM((GLA_H, tb, GLA_DV), F32),
        ],
        compiler_params=_cparams(("arbitrary", "arbitrary")),
        name="gla_prompt",
    )(h, h, h, h, h, wg, bg, gn, lcat, mix)


def _swa_sample_kernel(q_ref, g_ref, k_ref, v_ref, kb_ref, vb_ref, cos_ref, sin_ref, sink_ref,
                       o_ref, ko_ref, vo_ref):
    w = kb_ref.shape[0]
    cos = cos_ref[...]
    sin = sin_ref[...]
    lane1 = lax.broadcasted_iota(jnp.int32, (1, LANES), 1)
    lane = lax.broadcasted_iota(jnp.int32, (16, LANES), 1)
    rowi = lax.broadcasted_iota(jnp.int32, (w, LANES), 0)
    lo_w = lax.broadcasted_iota(jnp.int32, (w, LANES), 1) < 64
    lo = lane < 64

    kk, vv = [], []
    for p in range(2):
        sl = slice(p * LANES, (p + 1) * LANES)
        knew = _rope128(k_ref[:, sl], cos, sin, (lane1 & 32) == 0)
        vnew = v_ref[:, sl]
        kwin = jnp.where(rowi == w - 1, knew, pltpu.roll(kb_ref[:, sl], w - 1, 0))
        vwin = jnp.where(rowi == w - 1, vnew, pltpu.roll(vb_ref[:, sl], w - 1, 0))
        ko_ref[:, sl] = kwin
        vo_ref[:, sl] = vwin
        kk += [x.astype(BF16) for x in _dup_halves(kwin, lo_w)]
        vv += [x.astype(BF16) for x in _dup_halves(vwin, lo_w)]

    q = _rope128(q_ref[...], cos, sin, (lane & 32) == 0) * (SWA_HD ** -0.5)
    qq = jnp.concatenate([jnp.where(lo, q, 0.0), jnp.where(lo, 0.0, q)], axis=0)
    grp = (lax.broadcasted_iota(jnp.int32, (32, LANES), 0) % 16) // 4
    s = jnp.zeros((32, w), F32)
    for kh in range(SWA_KV):
        s = s + lax.dot_general(jnp.where(grp == kh, qq, 0.0).astype(BF16), kk[kh],
                                (((1,), (1,)), ((), ())), preferred_element_type=F32)
    sk = sink_ref[...][:, 0:1]
    m = jnp.maximum(jnp.max(s, axis=-1, keepdims=True), sk)
    p = jnp.exp(s - m)
    den = jnp.sum(p, axis=-1, keepdims=True) + jnp.exp(sk - m)
    o = jnp.zeros((32, LANES), F32)
    for kh in range(SWA_KV):
        o = o + jnp.dot(jnp.where(grp == kh, p, 0.0).astype(BF16), vv[kh], preferred_element_type=F32)
    o = o * (1.0 / den)
    g = g_ref[...]
    o_ref[...] = jnp.where(lo, o[0:16], o[16:32]) * (g * _sigmoid(g))


def swa_sample(q, g, k, v, kbuf, vbuf, cos, sin, sink_rows):
    bd, wb, kvw = kbuf.shape
    return pl.pallas_call(
        _swa_sample_kernel,
        grid=(bd,),
        in_specs=[
            pl.BlockSpec((None, 16, LANES), lambda b: (b, 0, 0)),
            pl.BlockSpec((None, 16, LANES), lambda b: (b, 0, 0)),
            pl.BlockSpec((None, 1, kvw), lambda b: (b, 0, 0)),
            pl.BlockSpec((None, 1, kvw), lambda b: (b, 0, 0)),
            pl.BlockSpec((None, wb, kvw), lambda b: (b, 0, 0)),
            pl.BlockSpec((None, wb, kvw), lambda b: (b, 0, 0)),
            pl.BlockSpec((1, LANES), lambda b: (0, 0)),
            pl.BlockSpec((1, LANES), lambda b: (0, 0)),
            pl.BlockSpec((32, LANES), lambda b: (0, 0)),
        ],
        out_specs=[
            pl.BlockSpec((None, 16, LANES), lambda b: (b, 0, 0)),
            pl.BlockSpec((None, wb, kvw), lambda b: (b, 0, 0)),
            pl.BlockSpec((None, wb, kvw), lambda b: (b, 0, 0)),
        ],
        out_shape=[
            jax.ShapeDtypeStruct((bd, 16, LANES), F32),
            jax.ShapeDtypeStruct((bd, wb, kvw), F32),
            jax.ShapeDtypeStruct((bd, wb, kvw), F32),
        ],
        compiler_params=_cparams(("parallel",)),
        name="swa_sample",
    )(q, g, k, v, kbuf, vbuf, cos, sin, sink_rows)


def _mem_sample_kernel(q_ref, g_ref, k_ref, v_ref, o_ref):
    q = q_ref[...] * (MEM_HD ** -0.5)
    rowh = lax.broadcasted_iota(jnp.int32, (8, MEM_HD), 0)
    s = jnp.zeros((8, MEM_LEN), F32)
    for h in range(MEM_H):
        sl = slice(h * MEM_HD, (h + 1) * MEM_HD)
        s = s + lax.dot_general(jnp.where(rowh == h, q, 0.0).astype(BF16), k_ref[:, sl].astype(BF16),
                                (((1,), (1,)), ((), ())), preferred_element_type=F32)
    m = jnp.max(s, axis=-1, keepdims=True)
    p = jnp.exp(s - m)
    den = jnp.sum(p, axis=-1, keepdims=True)
    o = jnp.zeros((8, MEM_HD), F32)
    for h in range(MEM_H):
        sl = slice(h * MEM_HD, (h + 1) * MEM_HD)
        o = o + jnp.dot(jnp.where(rowh == h, p, 0.0).astype(BF16), v_ref[:, sl].astype(BF16),
                        preferred_element_type=F32)
    g = g_ref[...]
    o_ref[...] = o * (1.0 / den) * (g * _sigmoid(g))


def mem_sample(q, g, k, v):
    bd = q.shape[0]
    w = MEM_H * MEM_HD
    return pl.pallas_call(
        _mem_sample_kernel,
        grid=(bd,),
        in_specs=[
            pl.BlockSpec((None, 8, MEM_HD), lambda b: (b, 0, 0)),
            pl.BlockSpec((None, 8, MEM_HD), lambda b: (b, 0, 0)),
            pl.BlockSpec((None, MEM_LEN, w), lambda b: (b, 0, 0)),
            pl.BlockSpec((None, MEM_LEN, w), lambda b: (b, 0, 0)),
        ],
        out_specs=pl.BlockSpec((None, 8, MEM_HD), lambda b: (b, 0, 0)),
        out_shape=jax.ShapeDtypeStruct((bd, 8, MEM_HD), F32),
        compiler_params=_cparams(("parallel",)),
        name="mem_sample",
    )(q, g, k, v)


def _gla_sample_kernel(q_ref, k_ref, v_ref, g_ref, lr_ref, wg_ref, bg_ref, gn_ref, s_ref, o_ref, so_ref):
    rowk = lax.broadcasted_iota(jnp.int32, (8, GLA_DK), 0)
    lr = jnp.broadcast_to(lr_ref[...], (8, LANES)).astype(BF16)
    xa = jnp.dot(lr, wg_ref[...], preferred_element_type=F32) + bg_ref[...]
    x = jnp.zeros((8, GLA_DK), F32)
    for hd in range(GLA_H):
        x = jnp.where(rowk == hd, xa[:, hd * GLA_DK:(hd + 1) * GLA_DK], x)
    gl = _log_sigmoid(x) * (1.0 / GLA_TAU)
    eg = jnp.exp(gl)
    q = q_ref[...] * (GLA_DK ** -0.5)
    k = k_ref[...]
    v = v_ref[...]
    qe = q * eg
    eye = lax.broadcasted_iota(jnp.int32, (GLA_DK, GLA_DK), 0) == lax.broadcasted_iota(jnp.int32, (GLA_DK, GLA_DK), 1)

    def col(row):
        return jnp.sum(jnp.where(eye, jnp.broadcast_to(row, (GLA_DK, GLA_DK)), 0.0), axis=-1, keepdims=True)

    o = jnp.sum(q * k, axis=-1, keepdims=True) * v
    for hd in range(GLA_H):
        s0 = s_ref[hd]
        o = o + jnp.dot(jnp.where(rowk == hd, qe, 0.0).astype(BF16), s0.astype(BF16), preferred_element_type=F32)
        so_ref[hd] = s0 * col(eg[hd:hd + 1, :]) + col(k[hd:hd + 1, :]) * v[hd:hd + 1, :]
    on = o * lax.rsqrt(jnp.mean(o * o, axis=-1, keepdims=True) + EPS) * gn_ref[...]
    gg = g_ref[...]
    o_ref[...] = on * (gg * _sigmoid(gg))


def gla_sample(q, k, v, g, lr, wg, bg, gn, state):
    bd = q.shape[0]
    kw = GLA_H * GLA_DK
    return pl.pallas_call(
        _gla_sample_kernel,
        grid=(bd,),
        in_specs=[
            pl.BlockSpec((None, 8, GLA_DK), lambda b: (b, 0, 0)),
            pl.BlockSpec((None, 8, GLA_DK), lambda b: (b, 0, 0)),
            pl.BlockSpec((None, 8, GLA_DV), lambda b: (b, 0, 0)),
            pl.BlockSpec((None, 8, GLA_DV), lambda b: (b, 0, 0)),
            pl.BlockSpec((None, 1, LANES), lambda b: (b, 0, 0)),
            pl.BlockSpec((LANES, kw), lambda b: (0, 0)),
            pl.BlockSpec((1, kw), lambda b: (0, 0)),
            pl.BlockSpec((8, GLA_DV), lambda b: (0, 0)),
            pl.BlockSpec((None, GLA_H, GLA_DK, GLA_DV), lambda b: (b, 0, 0, 0)),
        ],
        out_specs=[
            pl.BlockSpec((None, 8, GLA_DV), lambda b: (b, 0, 0)),
            pl.BlockSpec((None, GLA_H, GLA_DK, GLA_DV), lambda b: (b, 0, 0, 0)),
        ],
        out_shape=[
            jax.ShapeDtypeStruct((bd, 8, GLA_DV), F32),
            jax.ShapeDtypeStruct(state.shape, F32),
        ],
        compiler_params=_cparams(("parallel",)),
        name="gla_sample",
    )(q, k, v, g, lr, wg, bg, gn, state)


def _heads8(x, width):
    bd = x.shape[0]
    x = x.reshape(bd, -1, width)
    return jnp.pad(x, ((0, 0), (0, 8 - x.shape[1]), (0, 0)))


def kernel(x_prompt, mem_prompt, x_sample, cache_swa_k, cache_swa_v, state_gla, cache_mem_k, cache_mem_v,
           norm_g, w_in, attn_sinks, gla_w_gate, gla_b_gate, gla_norm_g, mem_norm_g, w_mem_kv, w_out,
           final_norm_g):
    batch, seq, d = x_prompt.shape
    bd = x_sample.shape[0]
    m = batch * seq
    kvw = SWA_KV * SWA_HD
    memw = MEM_H * MEM_HD

    cos_p, sin_p = _rope_tables(jnp.arange(seq))
    cos_s, sin_s = _rope_tables(PAST_LEN + jnp.arange(1))
    perm = np.array([2 * (r % 16) + r // 16 for r in range(32)])
    w_in_t = jnp.transpose(w_in, (0, 2, 1))

    xp = x_prompt.reshape(m, d)
    xs = x_sample.reshape(bd, d)
    memx = mem_prompt.reshape(batch * MEM_LEN, d)

    outs = {k: [] for k in ("kp", "vp", "sp", "mkp", "mvp", "ks", "vs", "ss")}
    for l in range(DEPTH):
        wg = jnp.pad(gla_w_gate[l], ((0, LANES - GLA_RANK), (0, 0))).astype(BF16)
        bg = gla_b_gate[l].reshape(1, -1)
        gn = gla_norm_g[l].reshape(1, -1)

        kv = project(rmsnorm(memx, mem_norm_g[l], BF16, 256), w_mem_kv, l, name="mem_kv")
        xn = rmsnorm(xp, norm_g[l], BF16, 256)
        xns = rmsnorm(xs, norm_g[l], BF16, bd)
        h, hs = project(xn, w_in_t, l, n=H_END, tile_rows=IN_TILE_SRC, a_small=xns, name="in_proj")

        mix, kb, vb = swa_prompt(h, attn_sinks[l], cos_p, sin_p, batch, seq)
        mix, sp = gla_prompt(h, wg, bg, gn, mix, batch, seq, 256)
        mix = mem_attn_prompt(h, kv, mix, batch, seq, 512)
        outs["kp"].append(kb.reshape(batch, WINDOW, SWA_KV, SWA_HD))
        outs["vp"].append(vb.reshape(batch, WINDOW, SWA_KV, SWA_HD))
        outs["sp"].append(sp)
        outs["mkp"].append(kv[:, :memw].reshape(batch, MEM_LEN, MEM_H, MEM_HD))
        outs["mvp"].append(kv[:, memw:].reshape(batch, MEM_LEN, MEM_H, MEM_HD))

        wb = cache_swa_k.shape[2]
        o_swa, kbs, vbs = swa_sample(
            hs[:, H_SQ:H_SQ + 2048].reshape(bd, 16, LANES), hs[:, H_SG:H_SG + 2048].reshape(bd, 16, LANES),
            hs[:, H_SK:H_SK + kvw].reshape(bd, 1, kvw), hs[:, H_SV:H_SV + kvw].reshape(bd, 1, kvw),
            cache_swa_k[l].reshape(bd, wb, kvw), cache_swa_v[l].reshape(bd, wb, kvw),
            cos_s, sin_s, jnp.broadcast_to(attn_sinks[l][perm][:, None], (32, LANES)))
        o_gla, ss = gla_sample(
            _heads8(hs[:, H_GQ:H_GQ + 512], GLA_DK), _heads8(hs[:, H_GK:H_GK + 512], GLA_DK),
            _heads8(hs[:, H_GV:H_GV + 1024], GLA_DV), _heads8(hs[:, H_GG:H_GG + 1024], GLA_DV),
            hs[:, H_GLR:H_GLR + LANES].reshape(bd, 1, LANES), wg, bg,
            jnp.pad(gla_norm_g[l].reshape(GLA_H, GLA_DV), ((0, 4), (0, 0))), state_gla[l])
        o_mem = mem_sample(
            _heads8(hs[:, H_MQ:H_MQ + memw], MEM_HD), _heads8(hs[:, H_MG:H_MG + memw], MEM_HD),
            cache_mem_k[l].reshape(bd, MEM_LEN, memw), cache_mem_v[l].reshape(bd, MEM_LEN, memw))
        mix_s = jnp.concatenate([o_swa.reshape(bd, 2048), o_gla[:, :GLA_H].reshape(bd, 1024),
                                 o_mem[:, :MEM_H].reshape(bd, memw)], axis=1).astype(BF16)
        outs["ks"].append(kbs.reshape(bd, wb, SWA_KV, SWA_HD))
        outs["vs"].append(vbs.reshape(bd, wb, SWA_KV, SWA_HD))
        outs["ss"].append(ss)

        xp, xs = project(mix, w_out, l, a_small=mix_s, res=xp, res_small=xs, name="out_proj")

    y_prompt = rmsnorm(xp, final_norm_g, F32, 256).reshape(batch, seq, d)
    y_sample = rmsnorm(xs, final_norm_g, F32, bd).reshape(bd, 1, d)
    st = lambda k: jnp.stack(outs[k])
    return (y_prompt, y_sample, st("kp"), st("vp"), st("sp"), st("mkp"), st("mvp"), st("ks"), st("vs"), st("ss"))
```

```python
import functools
import math

import jax
import jax.numpy as jnp
import numpy as np
from jax import lax
from jax.experimental import pallas as pl
from jax.experimental.pallas import tpu as pltpu

F32 = jnp.float32
BF16 = jnp.bfloat16

D_MODEL = 4096
DEPTH = 2
EPS = 1e-6
PAST_LEN = 16384
WINDOW = 128
ROPE_THETA = 10000.0
SWA_HD = 64
SWA_HQ = 32
SWA_KV = 4
GLA_H = 4
GLA_DK = 128
GLA_DV = 256
GLA_RANK = 16
GLA_TAU = 16.0
MEM_LEN = 256
MEM_H = 4
MEM_HD = 256
LANES = 128
GLA_CHUNK = 64
SUB = 16
LOG2E = math.log2(math.e)

TM = 2048
TN = 256
H_SQ, H_SG, H_GV, H_GG, H_MQ, H_MG, H_GQ, H_GK, H_SK, H_SV, H_GLR, H_END = (
    0, 2048, 4096, 5120, 6144, 7168, 8192, 8704, 9216, 9472, 9728, 9856)
_IN_PIECES = ((0, 2048), (2560, 2048), (5632, 1024), (6672, 1024), (7696, 1024), (8720, 1024), (4608, 512),
              (5120, 512), (2048, 512), (6656, 128))
IN_TILE_SRC = tuple(src + TN * t for src, width in _IN_PIECES for t in range(-(-width // TN)))
VMEM_LIMIT = 56 * 1024 * 1024


def _cparams(sem):
    return pltpu.CompilerParams(dimension_semantics=sem, vmem_limit_bytes=VMEM_LIMIT)


def _sigmoid(x):
    return 1.0 / (1.0 + jnp.exp(-x))


def _rmsnorm_kernel(x_ref, g_ref, o_ref):
    x = x_ref[...]
    ms = jnp.mean(x * x, axis=-1, keepdims=True)
    o_ref[...] = (x * lax.rsqrt(ms + EPS) * g_ref[...]).astype(o_ref.dtype)


def rmsnorm(x, g, out_dtype, tm):
    m, d = x.shape
    return pl.pallas_call(
        _rmsnorm_kernel,
        grid=(m // tm,),
        in_specs=[pl.BlockSpec((tm, d), lambda i: (i, 0)), pl.BlockSpec((1, d), lambda i: (0, 0))],
        out_specs=pl.BlockSpec((tm, d), lambda i: (i, 0)),
        out_shape=jax.ShapeDtypeStruct((m, d), out_dtype),
        compiler_params=_cparams(("parallel",)),
        name="rmsnorm",
    )(x, g.reshape(1, d))


def _proj_kernel(*refs, n_parts, has_small, has_res, tn, last_valid, w_rows):
    refs = list(refs)
    if w_rows:
        refs.pop(0)
    a_refs = [refs.pop(0) for _ in range(n_parts)]
    as_ref = refs.pop(0) if has_small else None
    w_ref = refs.pop(0)
    r_ref = refs.pop(0) if has_res else None
    rs_ref = refs.pop(0) if (has_res and has_small) else None
    o_ref = refs.pop(0)
    os_ref = refs.pop(0) if has_small else None
    i = pl.program_id(0)
    j = pl.program_id(1)
    nj = pl.num_programs(1)

    def compute(width):
        if w_rows:
            w = w_ref[:width, :].astype(BF16)
            mm = lambda a: lax.dot_general(a, w, (((1,), (1,)), ((), ())), preferred_element_type=F32)
            acc = mm(a_refs[0][...])
        else:
            w = w_ref[:, :width].astype(BF16)
            mm = lambda a: jnp.dot(a, w, preferred_element_type=F32)
            acc, k0 = None, 0
            for a_ref in a_refs:
                kp = a_ref.shape[1]
                part = jnp.dot(a_ref[...], w[k0:k0 + kp], preferred_element_type=F32)
                acc = part if acc is None else acc + part
                k0 += kp
        if has_res:
            acc = acc + r_ref[:, :width]
        o_ref[:, :width] = acc
        if has_small:
            @pl.when(i == 0)
            def _():
                accs = mm(as_ref[...])
                if has_res:
                    accs = accs + rs_ref[:, :width]
                os_ref[:, :width] = accs

    if last_valid == tn:
        compute(tn)
    else:
        pl.when(j < nj - 1)(lambda: compute(tn))
        pl.when(j == nj - 1)(lambda: compute(last_valid))


def project(a, w3, layer, n=None, tile_rows=None, a_small=None, res=None, res_small=None, name="proj"):
    a_parts = list(a) if isinstance(a, (tuple, list)) else [a]
    m = a_parts[0].shape[0]
    k = sum(p.shape[1] for p in a_parts)
    w_rows = tile_rows is not None
    if n is None:
        n = w3.shape[2]
    tm = min(TM, m)
    nj = pl.cdiv(n, TN)
    last_valid = n - (nj - 1) * TN
    has_small = a_small is not None
    has_res = res is not None
    small_idx = lambda i, j, *_: (0, jnp.where(i == 0, j, nj - 1))
    in_specs = [pl.BlockSpec((tm, p.shape[1]), lambda i, j, *_: (i, 0), pipeline_mode=pl.Buffered(1))
                for p in a_parts]
    args = list(a_parts)
    if has_small:
        ms = a_small.shape[0]
        in_specs.append(pl.BlockSpec((ms, k), lambda i, j, *_: (0, 0)))
        args.append(a_small)
    if w_rows:
        in_specs.append(pl.BlockSpec((None, pl.Element(TN), pl.Element(k)),
                                     lambda i, j, offs: (layer, pl.multiple_of(offs[j], 16), 0)))
    else:
        in_specs.append(pl.BlockSpec((None, k, TN), lambda i, j: (layer, 0, j)))
    args.append(w3)
    if has_res:
        in_specs.append(pl.BlockSpec((tm, TN), lambda i, j, *_: (i, j)))
        args.append(res)
        if has_small:
            in_specs.append(pl.BlockSpec((ms, TN), small_idx))
            args.append(res_small)
    out_specs = [pl.BlockSpec((tm, TN), lambda i, j, *_: (i, j))]
    out_shape = [jax.ShapeDtypeStruct((m, n), F32)]
    if has_small:
        out_specs.append(pl.BlockSpec((ms, TN), small_idx))
        out_shape.append(jax.ShapeDtypeStruct((ms, n), F32))
    kern = functools.partial(_proj_kernel, n_parts=len(a_parts), has_small=has_small, has_res=has_res, tn=TN,
                             last_valid=last_valid, w_rows=w_rows)
    if w_rows:
        grid_spec = pltpu.PrefetchScalarGridSpec(num_scalar_prefetch=1, grid=(m // tm, nj), in_specs=in_specs,
                                                 out_specs=out_specs)
        args = [jnp.asarray(tile_rows, jnp.int32)] + args
    else:
        grid_spec = pl.GridSpec(grid=(m // tm, nj), in_specs=in_specs, out_specs=out_specs)
    outs = pl.pallas_call(
        kern,
        grid_spec=grid_spec,
        out_shape=out_shape,
        compiler_params=_cparams(("arbitrary", "arbitrary")),
        name=name,
    )(*args)
    return outs if has_small else outs[0]


def _rope_tables(pos):
    half = SWA_HD // 2
    inv = ROPE_THETA ** (-jnp.arange(half, dtype=F32) / half)
    ang = pos.astype(F32)[:, None] * inv[None, :]
    cos, sin = jnp.cos(ang), jnp.sin(ang)
    return jnp.concatenate([cos, cos, cos, cos], axis=-1), jnp.concatenate([-sin, sin, -sin, sin], axis=-1)


def _rope128(x, cos, sin, first_half):
    partner = jnp.where(first_half, pltpu.roll(x, LANES - 32, 1), pltpu.roll(x, 32, 1))
    return x * cos + partner * sin


def _dup_halves(x, lo):
    r = pltpu.roll(x, 64, 1)
    return jnp.where(lo, x, r), jnp.where(lo, r, x)


def _swa_prompt_kernel(sink_ref, q_ref, g_ref, k_ref, v_ref, cos_ref, sin_ref,
                       o_ref, ko_ref, vo_ref, kprev, vprev):
    n = pl.program_id(1)

    @pl.when(n == 0)
    def _():
        kprev[...] = jnp.zeros_like(kprev)
        vprev[...] = jnp.zeros_like(vprev)

    w = WINDOW
    cos = cos_ref[...]
    sin = sin_ref[...]
    lane = lax.broadcasted_iota(jnp.int32, (w, LANES), 1)
    rowi = lax.broadcasted_iota(jnp.int32, (w, LANES), 0)
    first_half = (lane & 32) == 0
    lo = lane < 64
    own = lane <= rowi
    keep = jnp.where(own, 1, jnp.where(n > 0, 1, 0)) > 0
    ones = jnp.ones((2 * w, LANES), BF16)

    kall, vall = [], []
    kcur, vcur = [], []
    for p in range(2):
        kp = _rope128(k_ref[:, p * LANES:(p + 1) * LANES], cos, sin, first_half)
        vp = v_ref[:, p * LANES:(p + 1) * LANES]
        ko_ref[:, p * LANES:(p + 1) * LANES] = kp
        vo_ref[:, p * LANES:(p + 1) * LANES] = vp
        kcur += [x.astype(BF16) for x in _dup_halves(kp, lo)]
        vcur += [x.astype(BF16) for x in _dup_halves(vp, lo)]
    for kh in range(SWA_KV):
        kall.append(jnp.concatenate([kprev[kh], kcur[kh]], axis=0))
        vv = jnp.concatenate([vprev[kh], vcur[kh]], axis=0)
        vall.append(jnp.concatenate([vv, ones], axis=1))
    for kh in range(SWA_KV):
        kprev[kh] = kcur[kh]
        vprev[kh] = vcur[kh]

    scale = (SWA_HD ** -0.5) * LOG2E
    slabs = []
    for c in range(SWA_HQ // 2):
        qc = _rope128(q_ref[:, c * LANES:(c + 1) * LANES], cos, sin, first_half) * scale
        slabs.append(jnp.where(lo, qc, 0.0).astype(BF16))
        slabs.append(jnp.where(lo, 0.0, qc).astype(BF16))
    scores = [lax.dot_general(jnp.concatenate(slabs[8 * kh:8 * kh + 8], axis=0), kall[kh],
                              (((1,), (1,)), ((), ())), preferred_element_type=F32)
              for kh in range(SWA_KV)]
    folded = []
    for hq in range(SWA_HQ):
        sh = scores[hq // 8][(hq % 8) * w:(hq % 8 + 1) * w]
        folded.append(jnp.where(keep, jnp.where(own, sh[:, w:], sh[:, :w]), -1e30))
    maxes = [jnp.max(f, axis=-1, keepdims=True) for f in folded]
    probs = [jnp.exp2(f - m) for f, m in zip(folded, maxes)]
    sinks2 = [jnp.exp2(sink_ref[hq] * LOG2E - maxes[hq]) for hq in range(SWA_HQ)]
    ps = [jnp.concatenate([jnp.where(own, 0.0, p), jnp.where(own, p, 0.0)], axis=1).astype(BF16) for p in probs]
    outs = [jnp.dot(jnp.concatenate(ps[8 * kh:8 * kh + 8], axis=0), vall[kh], preferred_element_type=F32)
            for kh in range(SWA_KV)]
    normed = []
    for hq in range(SWA_HQ):
        oh = outs[hq // 8][(hq % 8) * w:(hq % 8 + 1) * w]
        normed.append(oh[:, :LANES] * (1.0 / (oh[:, LANES:] + sinks2[hq])))
    for c in range(SWA_HQ // 2):
        gt = g_ref[:, c * LANES:(c + 1) * LANES]
        o_ref[:, c * LANES:(c + 1) * LANES] = (
            jnp.where(lo, normed[2 * c], normed[2 * c + 1]) * (gt * _sigmoid(gt))).astype(o_ref.dtype)


def swa_prompt(h, sinks, cos, sin, batch, seq):
    m = h.shape[0]
    nb = seq // WINDOW
    kvw = SWA_KV * SWA_HD
    row = lambda b, n: b * nb + n
    return pl.pallas_call(
        _swa_prompt_kernel,
        grid=(batch, nb),
        in_specs=[
            pl.BlockSpec(memory_space=pltpu.SMEM),
            pl.BlockSpec((WINDOW, 2048), lambda b, n: (row(b, n), H_SQ // 2048)),
            pl.BlockSpec((WINDOW, 2048), lambda b, n: (row(b, n), H_SG // 2048)),
            pl.BlockSpec((WINDOW, kvw), lambda b, n: (row(b, n), H_SK // kvw)),
            pl.BlockSpec((WINDOW, kvw), lambda b, n: (row(b, n), H_SV // kvw)),
            pl.BlockSpec((WINDOW, LANES), lambda b, n: (n, 0)),
            pl.BlockSpec((WINDOW, LANES), lambda b, n: (n, 0)),
        ],
        out_specs=[
            pl.BlockSpec((WINDOW, 2048), lambda b, n: (row(b, n), 0)),
            pl.BlockSpec((None, WINDOW, kvw), lambda b, n: (b, 0, 0)),
            pl.BlockSpec((None, WINDOW, kvw), lambda b, n: (b, 0, 0)),
        ],
        out_shape=[
            jax.ShapeDtypeStruct((m, SWA_HQ * SWA_HD), BF16),
            jax.ShapeDtypeStruct((batch, WINDOW, kvw), F32),
            jax.ShapeDtypeStruct((batch, WINDOW, kvw), F32),
        ],
        scratch_shapes=[pltpu.VMEM((SWA_KV, WINDOW, LANES), BF16), pltpu.VMEM((SWA_KV, WINDOW, LANES), BF16)],
        compiler_params=_cparams(("arbitrary", "arbitrary")),
        name="swa_prompt",
    )(sinks, h, h, h, h, cos, sin)


def _mem_prompt_kernel(q_ref, g_ref, k_ref, v_ref, o_ref):
    scale = MEM_HD ** -0.5
    for h in range(MEM_H):
        sl = slice(h * MEM_HD, (h + 1) * MEM_HD)
        q = (q_ref[:, sl] * scale).astype(BF16)
        k = k_ref[:, sl].astype(BF16)
        v = v_ref[:, sl].astype(BF16)
        s = lax.dot_general(q, k, (((1,), (1,)), ((), ())), preferred_element_type=F32)
        m = jnp.max(s, axis=-1, keepdims=True)
        p = jnp.exp(s - m)
        den = jnp.sum(p, axis=-1, keepdims=True)
        o = jnp.dot(p.astype(BF16), v, preferred_element_type=F32) * (1.0 / den)
        g = g_ref[:, sl]
        o_ref[:, sl] = (o * (g * _sigmoid(g))).astype(o_ref.dtype)


def mem_attn_prompt(h, kv, batch, seq, tq):
    nt = seq // tq
    w = MEM_H * MEM_HD
    row = lambda b, t: b * nt + t
    return pl.pallas_call(
        _mem_prompt_kernel,
        grid=(batch, nt),
        in_specs=[
            pl.BlockSpec((tq, w), lambda b, t: (row(b, t), H_MQ // w)),
            pl.BlockSpec((tq, w), lambda b, t: (row(b, t), H_MG // w)),
            pl.BlockSpec((MEM_LEN, w), lambda b, t: (b, 0)),
            pl.BlockSpec((MEM_LEN, w), lambda b, t: (b, 1)),
        ],
        out_specs=pl.BlockSpec((tq, w), lambda b, t: (row(b, t), 0)),
        out_shape=jax.ShapeDtypeStruct((h.shape[0], w), BF16),
        compiler_params=_cparams(("parallel", "parallel")),
        name="mem_prompt",
    )(h, h, kv, kv)


def _log_sigmoid(x):
    return jnp.minimum(x, 0.0) - jnp.log1p(jnp.exp(-jnp.abs(x)))


def _gla_prompt_kernel(q_ref, k_ref, v_ref, g_ref, lr_ref, wg_ref, bg_ref, gn_ref, lc_ref,
                       o_ref, so_ref, st, qs, ks_, bs_, am, qt, kt, qh, eb, ob):
    t = pl.program_id(1)
    tb = q_ref.shape[0]
    ch = GLA_CHUNK
    nc = tb // ch
    ng = tb // SUB
    nsub = ch // SUB
    kw = GLA_H * GLA_DK
    hk = lambda hd: slice(hd * GLA_DK, (hd + 1) * GLA_DK)
    hv = lambda hd: slice(hd * GLA_DV, (hd + 1) * GLA_DV)

    @pl.when(t == 0)
    def _():
        st[...] = jnp.zeros_like(st)

    lane = lax.broadcasted_iota(jnp.int32, (ng, LANES), 1)
    grp = lax.broadcasted_iota(jnp.int32, (ng, LANES), 0)
    rel = lane - SUB * (grp % nsub)
    rr = lax.broadcasted_iota(jnp.int32, (ch, LANES), 0) // SUB
    cc = lax.broadcasted_iota(jnp.int32, (ch, LANES), 1)
    off_mask = jnp.where(cc >= 8 * rr * (rr - 1), jnp.where(cc < 8 * rr * (rr + 1), 1, 0), 0) > 0

    x = jnp.dot(lr_ref[...].astype(BF16), wg_ref[...], preferred_element_type=F32) + bg_ref[...]
    g2 = _log_sigmoid(x) * (LOG2E / GLA_TAU)
    g_hi = g2.astype(BF16)
    r1 = g2 - g_hi.astype(F32)
    g_mid = r1.astype(BF16)
    g_lo = (r1 - g_mid.astype(F32)).astype(BF16)
    cum = jnp.dot(lc_ref[...], jnp.concatenate([g_hi, g_mid, g_lo], axis=1), preferred_element_type=F32)
    cum = cum[:, :kw] + cum[:, kw:2 * kw] + cum[:, 2 * kw:]
    b, bl, bsb = cum[:tb], cum[tb:2 * tb], cum[2 * tb:]
    q = q_ref[...] * (GLA_DK ** -0.5)
    k = k_ref[...]
    for hd in range(GLA_H):
        qs[hd] = q[:, hk(hd)]
        ks_[hd] = k[:, hk(hd)]
        bs_[hd] = b[:, hk(hd)]
    qt[...] = (q * jnp.exp2(b)).astype(BF16)
    kt[...] = (k * jnp.exp2(bl - b)).astype(BF16)
    eb[...] = jnp.exp2(bl)
    qh[...] = (q * jnp.exp2(b - bsb)).astype(BF16)

    def rows_i(ref, i):
        return jnp.concatenate([ref[hd, pl.ds(i, ng, stride=SUB), :] for hd in range(GLA_H)], axis=1)

    qd = [rows_i(qs, i) for i in range(SUB)]
    kd = [rows_i(ks_, i) for i in range(SUB)]
    bd = [rows_i(bs_, i) for i in range(SUB)]
    for i in range(SUB):
        a_i = [jnp.zeros((ng, LANES), F32) for _ in range(GLA_H)]
        for j in range(i + 1):
            tt = qd[i] * kd[j] * jnp.exp2(bd[i] - bd[j])
            for hd in range(GLA_H):
                col = jnp.sum(tt[:, hk(hd)], axis=-1, keepdims=True)
                a_i[hd] = jnp.where(rel == j, col, a_i[hd])
        for hd in range(GLA_H):
            am[hd, pl.ds(i, ng, stride=SUB), :] = a_i[hd]

    kalls = []
    for cidx in range(nc):
        r0 = cidx * ch
        kc, bc = k[r0:r0 + ch], b[r0:r0 + ch]
        parts = []
        for sub in range(1, nsub):
            s_i = bsb[r0 + sub * SUB:r0 + sub * SUB + 1, :]
            parts.append((kc[:sub * SUB] * jnp.exp2(s_i - bc[:sub * SUB])).astype(BF16))
        parts.append(jnp.zeros((LANES - sum(p.shape[0] for p in parts), kw), BF16))
        kalls.append(jnp.concatenate(parts, axis=0))
    scs = {}
    for cidx in range(nc):
        for hd in range(GLA_H):
            scs[cidx, hd] = lax.dot_general(qh[cidx * ch:(cidx + 1) * ch, hk(hd)], kalls[cidx][:, hk(hd)],
                                            (((1,), (1,)), ((), ())), preferred_element_type=F32)
    for cidx in range(nc):
        rows = slice(cidx * ch, (cidx + 1) * ch)
        for hd in range(GLA_H):
            lhs = jnp.concatenate([jnp.where(off_mask, scs[cidx, hd], 0.0).astype(BF16),
                                   am[hd, rows, :].astype(BF16)], axis=1)
            vc = v_ref[rows, hv(hd)].astype(BF16)
            rhs = jnp.concatenate(
                [vc[:s * SUB] for s in range(1, nsub)]
                + [jnp.zeros((LANES - SUB * nsub * (nsub - 1) // 2, GLA_DV), BF16), vc,
                   jnp.zeros((LANES - ch, GLA_DV), BF16)], axis=0)
            ob[hd, rows, :] = jnp.dot(lhs, rhs, preferred_element_type=F32)

    for cidx in range(nc):
        rows = slice(cidx * ch, (cidx + 1) * ch)
        for hd in range(GLA_H):
            vc = v_ref[rows, hv(hd)].astype(BF16)
            sb = st[hd]
            ob[hd, rows, :] += lax.dot_general(qt[rows, hk(hd)], sb.astype(BF16), (((1,), (1,)), ((), ())),
                                               preferred_element_type=F32)
            upd = lax.dot_general(vc, kt[rows, hk(hd)], (((0,), (0,)), ((), ())), preferred_element_type=F32)
            st[hd] = sb * eb[cidx * ch:cidx * ch + 1, hk(hd)] + upd

    for hd in range(GLA_H):
        vs = slice(hd * GLA_DV, (hd + 1) * GLA_DV)
        o = ob[hd]
        on = o * lax.rsqrt(jnp.mean(o * o, axis=-1, keepdims=True) + EPS) * gn_ref[:, vs]
        gg = g_ref[:, vs]
        o_ref[:, vs] = (on * (gg * _sigmoid(gg))).astype(o_ref.dtype)

    @pl.when(t == pl.num_programs(1) - 1)
    def _():
        for hd in range(GLA_H):
            so_ref[hd] = st[hd].T


def gla_prompt(h, wg, bg, gn, batch, seq, tb):
    nt = seq // tb
    kw = GLA_H * GLA_DK
    vw = GLA_H * GLA_DV
    row = lambda b, t: b * nt + t
    r = np.arange(tb)[:, None]
    c = np.arange(tb)[None, :]
    same = (r // GLA_CHUNK) == (c // GLA_CHUNK)
    lcat = jnp.asarray(np.concatenate([same & (c <= r), same & (c >= 0), same & (c < (r // SUB) * SUB)], axis=0),
                       BF16)
    return pl.pallas_call(
        _gla_prompt_kernel,
        grid=(batch, nt),
        in_specs=[
            pl.BlockSpec((tb, kw), lambda b, t: (row(b, t), H_GQ // kw)),
            pl.BlockSpec((tb, kw), lambda b, t: (row(b, t), H_GK // kw)),
            pl.BlockSpec((tb, vw), lambda b, t: (row(b, t), H_GV // vw)),
            pl.BlockSpec((tb, vw), lambda b, t: (row(b, t), H_GG // vw)),
            pl.BlockSpec((tb, LANES), lambda b, t: (row(b, t), H_GLR // LANES)),
            pl.BlockSpec((LANES, kw), lambda b, t: (0, 0)),
            pl.BlockSpec((1, kw), lambda b, t: (0, 0)),
            pl.BlockSpec((1, vw), lambda b, t: (0, 0)),
            pl.BlockSpec((3 * tb, tb), lambda b, t: (0, 0)),
        ],
        out_specs=[
            pl.BlockSpec((tb, vw), lambda b, t: (row(b, t), 0)),
            pl.BlockSpec((None, GLA_H, GLA_DK, GLA_DV), lambda b, t: (b, 0, 0, 0)),
        ],
        out_shape=[
            jax.ShapeDtypeStruct((h.shape[0], vw), BF16),
            jax.ShapeDtypeStruct((batch, GLA_H, GLA_DK, GLA_DV), F32),
        ],
        scratch_shapes=[
            pltpu.VMEM((GLA_H, GLA_DV, GLA_DK), F32),
            pltpu.VMEM((GLA_H, tb, GLA_DK), F32),
            pltpu.VMEM((GLA_H, tb, GLA_DK), F32),
            pltpu.VMEM((GLA_H, tb, GLA_DK), F32),
            pltpu.VMEM((GLA_H, tb, LANES), F32),
            pltpu.VMEM((tb, kw), BF16),
            pltpu.VMEM((tb, kw), BF16),
            pltpu.VMEM((tb, kw), BF16),
            pltpu.VMEM((tb, kw), F32),
            pltpu.VMEM((GLA_H, tb, GLA_DV), F32),
        ],
        compiler_params=_cparams(("arbitrary", "arbitrary")),
        name="gla_prompt",
    )(h, h, h, h, h, wg, bg, gn, lcat)


def _swa_sample_kernel(q_ref, g_ref, k_ref, v_ref, kb_ref, vb_ref, cos_ref, sin_ref, sink_ref,
                       o_ref, ko_ref, vo_ref):
    w = kb_ref.shape[0]
    cos = cos_ref[...]
    sin = sin_ref[...]
    lane1 = lax.broadcasted_iota(jnp.int32, (1, LANES), 1)
    lane = lax.broadcasted_iota(jnp.int32, (16, LANES), 1)
    rowi = lax.broadcasted_iota(jnp.int32, (w, LANES), 0)
    lo_w = lax.broadcasted_iota(jnp.int32, (w, LANES), 1) < 64
    lo = lane < 64

    kk, vv = [], []
    for p in range(2):
        sl = slice(p * LANES, (p + 1) * LANES)
        knew = _rope128(k_ref[:, sl], cos, sin, (lane1 & 32) == 0)
        vnew = v_ref[:, sl]
        kwin = jnp.where(rowi == w - 1, knew, pltpu.roll(kb_ref[:, sl], w - 1, 0))
        vwin = jnp.where(rowi == w - 1, vnew, pltpu.roll(vb_ref[:, sl], w - 1, 0))
        ko_ref[:, sl] = kwin
        vo_ref[:, sl] = vwin
        kk += [x.astype(BF16) for x in _dup_halves(kwin, lo_w)]
        vv += [x.astype(BF16) for x in _dup_halves(vwin, lo_w)]

    q = _rope128(q_ref[...], cos, sin, (lane & 32) == 0) * (SWA_HD ** -0.5)
    qq = jnp.concatenate([jnp.where(lo, q, 0.0), jnp.where(lo, 0.0, q)], axis=0)
    grp = (lax.broadcasted_iota(jnp.int32, (32, LANES), 0) % 16) // 4
    s = jnp.zeros((32, w), F32)
    for kh in range(SWA_KV):
        s = s + lax.dot_general(jnp.where(grp == kh, qq, 0.0).astype(BF16), kk[kh],
                                (((1,), (1,)), ((), ())), preferred_element_type=F32)
    sk = sink_ref[...][:, 0:1]
    m = jnp.maximum(jnp.max(s, axis=-1, keepdims=True), sk)
    p = jnp.exp(s - m)
    den = jnp.sum(p, axis=-1, keepdims=True) + jnp.exp(sk - m)
    o = jnp.zeros((32, LANES), F32)
    for kh in range(SWA_KV):
        o = o + jnp.dot(jnp.where(grp == kh, p, 0.0).astype(BF16), vv[kh], preferred_element_type=F32)
    o = o * (1.0 / den)
    g = g_ref[...]
    o_ref[...] = jnp.where(lo, o[0:16], o[16:32]) * (g * _sigmoid(g))


def swa_sample(q, g, k, v, kbuf, vbuf, cos, sin, sink_rows):
    bd, wb, kvw = kbuf.shape
    return pl.pallas_call(
        _swa_sample_kernel,
        grid=(bd,),
        in_specs=[
            pl.BlockSpec((None, 16, LANES), lambda b: (b, 0, 0)),
            pl.BlockSpec((None, 16, LANES), lambda b: (b, 0, 0)),
            pl.BlockSpec((None, 1, kvw), lambda b: (b, 0, 0)),
            pl.BlockSpec((None, 1, kvw), lambda b: (b, 0, 0)),
            pl.BlockSpec((None, wb, kvw), lambda b: (b, 0, 0)),
            pl.BlockSpec((None, wb, kvw), lambda b: (b, 0, 0)),
            pl.BlockSpec((1, LANES), lambda b: (0, 0)),
            pl.BlockSpec((1, LANES), lambda b: (0, 0)),
            pl.BlockSpec((32, LANES), lambda b: (0, 0)),
        ],
        out_specs=[
            pl.BlockSpec((None, 16, LANES), lambda b: (b, 0, 0)),
            pl.BlockSpec((None, wb, kvw), lambda b: (b, 0, 0)),
            pl.BlockSpec((None, wb, kvw), lambda b: (b, 0, 0)),
        ],
        out_shape=[
            jax.ShapeDtypeStruct((bd, 16, LANES), F32),
            jax.ShapeDtypeStruct((bd, wb, kvw), F32),
            jax.ShapeDtypeStruct((bd, wb, kvw), F32),
        ],
        compiler_params=_cparams(("parallel",)),
        name="swa_sample",
    )(q, g, k, v, kbuf, vbuf, cos, sin, sink_rows)


def _mem_sample_kernel(q_ref, g_ref, k_ref, v_ref, o_ref):
    n = k_ref.shape[0]
    grp = 2 * MEM_H
    q8 = (q_ref[...] * (MEM_HD ** -0.5)).astype(BF16)
    r = lax.dot_general(q8, k_ref[...].astype(BF16), (((1,), (1,)), ((), ())), preferred_element_type=F32)
    row = lax.broadcasted_iota(jnp.int32, (grp, n), 0)
    col = lax.broadcasted_iota(jnp.int32, (grp, n), 1)
    own = (col % grp) == row
    a = jnp.where(own, r, 0.0)
    s = a + pltpu.roll(pltpu.roll(a, MEM_H, 0), n - MEM_H, 1)
    valid = jnp.where(own, jnp.where(row < MEM_H, 1, 0), 0) > 0
    sm = jnp.where(valid, s, -1e30)
    m = jnp.max(sm, axis=-1, keepdims=True)
    p = jnp.where(valid, jnp.exp(sm - m), 0.0)
    den = jnp.sum(p, axis=-1, keepdims=True)
    pn = p * (1.0 / jnp.where(den > 0.0, den, 1.0))
    p8 = pn + pltpu.roll(pltpu.roll(pn, MEM_H, 0), MEM_H, 1)
    o = jnp.dot(p8.astype(BF16), v_ref[...].astype(BF16), preferred_element_type=F32)
    g = g_ref[...]
    o_ref[...] = o * (g * _sigmoid(g))


def _cache_rows(c):
    depth, bd, mlen, nh, hd = c.shape
    return c.reshape(depth, bd, mlen, nh, hd // LANES, LANES).transpose(0, 1, 2, 4, 3, 5).reshape(
        depth, bd, mlen * nh * (hd // LANES), LANES)


def _half_head_rows(x):
    bd = x.shape[0]
    return x.reshape(bd, MEM_H, MEM_HD // LANES, LANES).transpose(0, 2, 1, 3).reshape(bd, -1, LANES)


def mem_sample(q, g, k, v, layer):
    bd = q.shape[0]
    rows = MEM_LEN * MEM_H * (MEM_HD // LANES)
    cache_spec = pl.BlockSpec((None, None, rows, LANES), lambda b: (layer, b, 0, 0))
    vec_spec = pl.BlockSpec((None, 2 * MEM_H, LANES), lambda b: (b, 0, 0))
    o = pl.pallas_call(
        _mem_sample_kernel,
        grid=(bd,),
        in_specs=[vec_spec, vec_spec, cache_spec, cache_spec],
        out_specs=vec_spec,
        out_shape=jax.ShapeDtypeStruct((bd, 2 * MEM_H, LANES), F32),
        compiler_params=_cparams(("parallel",)),
        name="mem_sample",
    )(_half_head_rows(q), _half_head_rows(g), _cache_rows(k), _cache_rows(v))
    return o.reshape(bd, MEM_HD // LANES, MEM_H, LANES).transpose(0, 2, 1, 3).reshape(bd, MEM_H * MEM_HD)


def _gla_sample_kernel(q_ref, k_ref, v_ref, g_ref, lr_ref, wg_ref, bg_ref, gn_ref, s_ref, o_ref, so_ref):
    rowk = lax.broadcasted_iota(jnp.int32, (8, GLA_DK), 0)
    lr = jnp.broadcast_to(lr_ref[...], (8, LANES)).astype(BF16)
    xa = jnp.dot(lr, wg_ref[...], preferred_element_type=F32) + bg_ref[...]
    x = jnp.zeros((8, GLA_DK), F32)
    for hd in range(GLA_H):
        x = jnp.where(rowk == hd, xa[:, hd * GLA_DK:(hd + 1) * GLA_DK], x)
    gl = _log_sigmoid(x) * (1.0 / GLA_TAU)
    eg = jnp.exp(gl)
    q = q_ref[...] * (GLA_DK ** -0.5)
    k = k_ref[...]
    v = v_ref[...]
    qe = q * eg
    eye = lax.broadcasted_iota(jnp.int32, (GLA_DK, GLA_DK), 0) == lax.broadcasted_iota(jnp.int32, (GLA_DK, GLA_DK), 1)

    def col(row):
        return jnp.sum(jnp.where(eye, jnp.broadcast_to(row, (GLA_DK, GLA_DK)), 0.0), axis=-1, keepdims=True)

    o = jnp.sum(q * k, axis=-1, keepdims=True) * v
    for hd in range(GLA_H):
        s0 = s_ref[hd]
        o = o + jnp.dot(jnp.where(rowk == hd, qe, 0.0).astype(BF16), s0.astype(BF16), preferred_element_type=F32)
        so_ref[hd] = s0 * col(eg[hd:hd + 1, :]) + col(k[hd:hd + 1, :]) * v[hd:hd + 1, :]
    on = o * lax.rsqrt(jnp.mean(o * o, axis=-1, keepdims=True) + EPS) * gn_ref[...]
    gg = g_ref[...]
    o_ref[...] = on * (gg * _sigmoid(gg))


def gla_sample(q, k, v, g, lr, wg, bg, gn, state, layer):
    bd = q.shape[0]
    kw = GLA_H * GLA_DK
    return pl.pallas_call(
        _gla_sample_kernel,
        grid=(bd,),
        in_specs=[
            pl.BlockSpec((None, 8, GLA_DK), lambda b: (b, 0, 0)),
            pl.BlockSpec((None, 8, GLA_DK), lambda b: (b, 0, 0)),
            pl.BlockSpec((None, 8, GLA_DV), lambda b: (b, 0, 0)),
            pl.BlockSpec((None, 8, GLA_DV), lambda b: (b, 0, 0)),
            pl.BlockSpec((None, 1, LANES), lambda b: (b, 0, 0)),
            pl.BlockSpec((LANES, kw), lambda b: (0, 0)),
            pl.BlockSpec((1, kw), lambda b: (0, 0)),
            pl.BlockSpec((8, GLA_DV), lambda b: (0, 0)),
            pl.BlockSpec((None, None, GLA_H, GLA_DK, GLA_DV), lambda b: (layer, b, 0, 0, 0)),
        ],
        out_specs=[
            pl.BlockSpec((None, 8, GLA_DV), lambda b: (b, 0, 0)),
            pl.BlockSpec((None, GLA_H, GLA_DK, GLA_DV), lambda b: (b, 0, 0, 0)),
        ],
        out_shape=[
            jax.ShapeDtypeStruct((bd, 8, GLA_DV), F32),
            jax.ShapeDtypeStruct(state.shape[1:], F32),
        ],
        compiler_params=_cparams(("parallel",)),
        name="gla_sample",
    )(q, k, v, g, lr, wg, bg, gn, state)


def _heads8(x, width):
    bd = x.shape[0]
    x = x.reshape(bd, -1, width)
    return jnp.pad(x, ((0, 0), (0, 8 - x.shape[1]), (0, 0)))


def kernel(x_prompt, mem_prompt, x_sample, cache_swa_k, cache_swa_v, state_gla, cache_mem_k, cache_mem_v,
           norm_g, w_in, attn_sinks, gla_w_gate, gla_b_gate, gla_norm_g, mem_norm_g, w_mem_kv, w_out,
           final_norm_g):
    batch, seq, d = x_prompt.shape
    bd = x_sample.shape[0]
    m = batch * seq
    kvw = SWA_KV * SWA_HD
    memw = MEM_H * MEM_HD

    cos_p, sin_p = _rope_tables(jnp.arange(seq))
    cos_s, sin_s = _rope_tables(PAST_LEN + jnp.arange(1))
    perm = np.array([2 * (r % 16) + r // 16 for r in range(32)])
    w_in_t = jnp.transpose(w_in, (0, 2, 1))

    xp = x_prompt.reshape(m, d)
    xs = x_sample.reshape(bd, d)
    memx = mem_prompt.reshape(batch * MEM_LEN, d)

    outs = {k: [] for k in ("kp", "vp", "sp", "mkp", "mvp", "ks", "vs", "ss")}
    for l in range(DEPTH):
        wg = jnp.pad(gla_w_gate[l], ((0, LANES - GLA_RANK), (0, 0))).astype(BF16)
        bg = gla_b_gate[l].reshape(1, -1)
        gn = gla_norm_g[l].reshape(1, -1)

        kv = project(rmsnorm(memx, mem_norm_g[l], BF16, 256), w_mem_kv, l, name="mem_kv")
        xn = rmsnorm(xp, norm_g[l], BF16, 256)
        xns = rmsnorm(xs, norm_g[l], BF16, bd)
        h, hs = project(xn, w_in_t, l, n=H_END, tile_rows=IN_TILE_SRC, a_small=xns, name="in_proj")

        o_swa_p, kb, vb = swa_prompt(h, attn_sinks[l], cos_p, sin_p, batch, seq)
        o_gla_p, sp = gla_prompt(h, wg, bg, gn, batch, seq, 256)
        o_mem_p = mem_attn_prompt(h, kv, batch, seq, 512)
        outs["kp"].append(kb.reshape(batch, WINDOW, SWA_KV, SWA_HD))
        outs["vp"].append(vb.reshape(batch, WINDOW, SWA_KV, SWA_HD))
        outs["sp"].append(sp)
        outs["mkp"].append(kv[:, :memw].reshape(batch, MEM_LEN, MEM_H, MEM_HD))
        outs["mvp"].append(kv[:, memw:].reshape(batch, MEM_LEN, MEM_H, MEM_HD))

        wb = cache_swa_k.shape[2]
        o_swa, kbs, vbs = swa_sample(
            hs[:, H_SQ:H_SQ + 2048].reshape(bd, 16, LANES), hs[:, H_SG:H_SG + 2048].reshape(bd, 16, LANES),
            hs[:, H_SK:H_SK + kvw].reshape(bd, 1, kvw), hs[:, H_SV:H_SV + kvw].reshape(bd, 1, kvw),
            cache_swa_k[l].reshape(bd, wb, kvw), cache_swa_v[l].reshape(bd, wb, kvw),
            cos_s, sin_s, jnp.broadcast_to(attn_sinks[l][perm][:, None], (32, LANES)))
        o_gla, ss = gla_sample(
            _heads8(hs[:, H_GQ:H_GQ + 512], GLA_DK), _heads8(hs[:, H_GK:H_GK + 512], GLA_DK),
            _heads8(hs[:, H_GV:H_GV + 1024], GLA_DV), _heads8(hs[:, H_GG:H_GG + 1024], GLA_DV),
            hs[:, H_GLR:H_GLR + LANES].reshape(bd, 1, LANES), wg, bg,
            jnp.pad(gla_norm_g[l].reshape(GLA_H, GLA_DV), ((0, 4), (0, 0))), state_gla, l)
        o_mem = mem_sample(hs[:, H_MQ:H_MQ + memw], hs[:, H_MG:H_MG + memw], cache_mem_k, cache_mem_v, l)
        mix_s = jnp.concatenate([o_swa.reshape(bd, 2048), o_gla[:, :GLA_H].reshape(bd, 1024), o_mem],
                                axis=1).astype(BF16)
        outs["ks"].append(kbs.reshape(bd, wb, SWA_KV, SWA_HD))
        outs["vs"].append(vbs.reshape(bd, wb, SWA_KV, SWA_HD))
        outs["ss"].append(ss)

        xp, xs = project([o_swa_p, o_gla_p, o_mem_p], w_out, l, a_small=mix_s, res=xp, res_small=xs,
                         name="out_proj")

    y_prompt = rmsnorm(xp, final_norm_g, F32, 256).reshape(batch, seq, d)
    y_sample = rmsnorm(xs, final_norm_g, F32, bd).reshape(bd, 1, d)
    st = lambda k: jnp.stack(outs[k])
    return (y_prompt, y_sample, st("kp"), st("vp"), st("sp"), st("mkp"), st("mvp"), st("ks"), st("vs"), st("ss"))
```

```python
import functools
import math

import jax
import jax.numpy as jnp
import numpy as np
from jax import lax
from jax.experimental import pallas as pl
from jax.experimental.pallas import tpu as pltpu

F32 = jnp.float32
BF16 = jnp.bfloat16

D_MODEL = 4096
DEPTH = 2
EPS = 1e-6
PAST_LEN = 16384
WINDOW = 128
ROPE_THETA = 10000.0
SWA_HD = 64
SWA_HQ = 32
SWA_KV = 4
GLA_H = 4
GLA_DK = 128
GLA_DV = 256
GLA_RANK = 16
GLA_TAU = 16.0
MEM_LEN = 256
MEM_H = 4
MEM_HD = 256
LANES = 128
GLA_CHUNK = 64
SUB = 16
LOG2E = math.log2(math.e)

TM = 2048
TN = 512
H_SQ, H_SG, H_GV, H_GG, H_MQ, H_MG, H_GQ, H_GK, H_SK, H_SV, H_GLR, H_END = (
    0, 2048, 4096, 5120, 6144, 7168, 8192, 8704, 9216, 9472, 9728, 9856)
_IN_PIECES = ((0, 2048, 2, 2, 0), (2560, 2048, 0, 0, 1), (5632, 1024, 0, 0, 0), (6672, 1024, 0, 0, 1),
              (7696, 1024, 0, 0, 0), (8720, 1024, 0, 0, 1), (4608, 512, 0, 0, 0), (5120, 512, 0, 0, 0),
              (2048, 512, 1, 0, 0), (6656, 128, 0, 0, 0))
_IN_TILES = tuple((src + TN * t, va, vb, gate) for src, width, va, vb, gate in _IN_PIECES
                  for t in range(-(-width // TN)))
IN_TILE_SRC = tuple(t[0] for t in _IN_TILES)
SWA_QSCALE = (SWA_HD ** -0.5) * LOG2E
EPI_ROWS = 512
VMEM_LIMIT = 58 * 1024 * 1024


def _cparams(sem):
    return pltpu.CompilerParams(dimension_semantics=sem, vmem_limit_bytes=VMEM_LIMIT)


def _sigmoid(x):
    return 1.0 / (1.0 + jnp.exp(-x))


def _rmsnorm_kernel(x_ref, g_ref, o_ref):
    x = x_ref[...]
    ms = jnp.mean(x * x, axis=-1, keepdims=True)
    o_ref[...] = (x * lax.rsqrt(ms + EPS) * g_ref[...]).astype(o_ref.dtype)


def rmsnorm(x, g, out_dtype, tm):
    m, d = x.shape
    return pl.pallas_call(
        _rmsnorm_kernel,
        grid=(m // tm,),
        in_specs=[pl.BlockSpec((tm, d), lambda i: (i, 0)), pl.BlockSpec((1, d), lambda i: (0, 0))],
        out_specs=pl.BlockSpec((tm, d), lambda i: (i, 0)),
        out_shape=jax.ShapeDtypeStruct((m, d), out_dtype),
        compiler_params=_cparams(("parallel",)),
        name="rmsnorm",
    )(x, g.reshape(1, d))


def _proj_kernel(*refs, n_parts, has_small, has_res, tn, last_valid, w_rows):
    refs = list(refs)
    i = pl.program_id(0)
    j = pl.program_id(1)
    nj = pl.num_programs(1)
    if w_rows:
        refs.pop(0)
        refs.pop(0)
        refs.pop(0)
        is_gate = refs.pop(0)[j] > 0
    a_refs = [refs.pop(0) for _ in range(n_parts)]
    as_ref = refs.pop(0) if has_small else None
    w_ref = refs.pop(0)
    if w_rows:
        tabs = [refs.pop(0) for _ in range(4)]
        tabs_s = [refs.pop(0) for _ in range(4)]
    r_ref = refs.pop(0) if has_res else None
    rs_ref = refs.pop(0) if (has_res and has_small) else None
    o_ref = refs.pop(0)
    os_ref = refs.pop(0) if has_small else None

    def finish(acc, tab_refs):
        lane = lax.broadcasted_iota(jnp.int32, (1, LANES), 1)
        first_half = (lane & 32) == 0
        outs = []
        for c in range(acc.shape[1] // LANES):
            cs, sn = tab_refs[0 if c < 2 else 2][...], tab_refs[1 if c < 2 else 3][...]
            x = acc[:, c * LANES:(c + 1) * LANES]
            partner = jnp.where(first_half, pltpu.roll(x, LANES - 32, 1), pltpu.roll(x, 32, 1))
            r = x * cs + partner * sn
            outs.append(jnp.where(is_gate, r * _sigmoid(r), r))
        return jnp.concatenate(outs, axis=1)

    def compute(width):
        if w_rows:
            w = w_ref[:width, :].astype(BF16)
            mm = lambda a: lax.dot_general(a, w, (((1,), (1,)), ((), ())), preferred_element_type=F32)
            tm = o_ref.shape[0]
            rc = min(EPI_ROWS, tm)
            for r0 in range(0, tm, rc):
                rows = slice(r0, r0 + rc)
                o_ref[rows, :width] = finish(mm(a_refs[0][rows, :]), [t.at[rows, :] for t in tabs])
            acc = None
        else:
            w = w_ref[:, :width].astype(BF16)
            mm = lambda a: jnp.dot(a, w, preferred_element_type=F32)
            acc, k0 = None, 0
            for a_ref in a_refs:
                kp = a_ref.shape[1]
                part = jnp.dot(a_ref[...], w[k0:k0 + kp], preferred_element_type=F32)
                acc = part if acc is None else acc + part
                k0 += kp
        if acc is not None:
            if has_res:
                acc = acc + r_ref[:, :width]
            o_ref[:, :width] = acc
        if has_small:
            @pl.when(i == 0)
            def _():
                accs = mm(as_ref[...])
                if w_rows:
                    accs = finish(accs, tabs_s)
                if has_res:
                    accs = accs + rs_ref[:, :width]
                os_ref[:, :width] = accs

    if last_valid == tn:
        compute(tn)
    else:
        pl.when(j < nj - 1)(lambda: compute(tn))
        pl.when(j == nj - 1)(lambda: compute(last_valid))


def project(a, w3, layer, n=None, tiles=None, tables=None, tables_small=None, a_small=None, res=None,
            res_small=None, name="proj"):
    a_parts = list(a) if isinstance(a, (tuple, list)) else [a]
    m = a_parts[0].shape[0]
    k = sum(p.shape[1] for p in a_parts)
    w_rows = tiles is not None
    if n is None:
        n = w3.shape[2]
    tm = min(TM, m)
    nj = pl.cdiv(n, TN)
    last_valid = n - (nj - 1) * TN
    has_small = a_small is not None
    has_res = res is not None
    small_idx = lambda i, j, *_: (0, jnp.where(i == 0, j, nj - 1))
    in_specs = [pl.BlockSpec((tm, p.shape[1]), lambda i, j, *_: (i, 0), pipeline_mode=pl.Buffered(1))
                for p in a_parts]
    args = list(a_parts)
    if has_small:
        ms = a_small.shape[0]
        in_specs.append(pl.BlockSpec((ms, k), lambda i, j, *_: (0, 0)))
        args.append(a_small)
    if w_rows:
        assert tm == tables[0].shape[1], "a row tile must cover exactly one sequence of positions"
        in_specs.append(pl.BlockSpec((None, pl.Element(TN), pl.Element(k)),
                                     lambda i, j, offs, va, vb, gt: (layer, pl.multiple_of(offs[j], 16), 0)))
        args.append(w3)
        for tabs, rows in ((tables, tm), (tables_small, a_small.shape[0])):
            for half in (0, 1):
                for tab in tabs:
                    pick = (lambda i, j, offs, va, vb, gt: (va[j], 0, 0)) if half == 0 else (
                        lambda i, j, offs, va, vb, gt: (vb[j], 0, 0))
                    in_specs.append(pl.BlockSpec((None, rows, LANES), pick, pipeline_mode=pl.Buffered(1)))
                    args.append(tab)
    else:
        in_specs.append(pl.BlockSpec((None, k, TN), lambda i, j: (layer, 0, j)))
        args.append(w3)
    if has_res:
        in_specs.append(pl.BlockSpec((tm, TN), lambda i, j, *_: (i, j)))
        args.append(res)
        if has_small:
            in_specs.append(pl.BlockSpec((ms, TN), small_idx))
            args.append(res_small)
    out_specs = [pl.BlockSpec((tm, TN), lambda i, j, *_: (i, j))]
    out_shape = [jax.ShapeDtypeStruct((m, n), F32)]
    if has_small:
        out_specs.append(pl.BlockSpec((ms, TN), small_idx))
        out_shape.append(jax.ShapeDtypeStruct((ms, n), F32))
    kern = functools.partial(_proj_kernel, n_parts=len(a_parts), has_small=has_small, has_res=has_res, tn=TN,
                             last_valid=last_valid, w_rows=w_rows)
    if w_rows:
        grid_spec = pltpu.PrefetchScalarGridSpec(num_scalar_prefetch=4, grid=(m // tm, nj), in_specs=in_specs,
                                                 out_specs=out_specs)
        args = [jnp.asarray([t[c] for t in tiles], jnp.int32) for c in range(4)] + args
    else:
        grid_spec = pl.GridSpec(grid=(m // tm, nj), in_specs=in_specs, out_specs=out_specs)
    outs = pl.pallas_call(
        kern,
        grid_spec=grid_spec,
        out_shape=out_shape,
        compiler_params=_cparams(("arbitrary", "arbitrary")),
        name=name,
    )(*args)
    return outs if has_small else outs[0]


def _rope_tables(pos):
    half = SWA_HD // 2
    inv = ROPE_THETA ** (-jnp.arange(half, dtype=F32) / half)
    ang = pos.astype(F32)[:, None] * inv[None, :]
    cos, sin = jnp.cos(ang), jnp.sin(ang)
    return jnp.concatenate([cos, cos, cos, cos], axis=-1), jnp.concatenate([-sin, sin, -sin, sin], axis=-1)


def _rope128(x, cos, sin, first_half):
    partner = jnp.where(first_half, pltpu.roll(x, LANES - 32, 1), pltpu.roll(x, 32, 1))
    return x * cos + partner * sin


def _dup_halves(x, lo):
    r = pltpu.roll(x, 64, 1)
    return jnp.where(lo, x, r), jnp.where(lo, r, x)


def _swa_prompt_kernel(sink_ref, q_ref, g_ref, k_ref, v_ref, o_ref, kprev, vprev):
    n = pl.program_id(1)

    @pl.when(n == 0)
    def _():
        kprev[...] = jnp.zeros_like(kprev)
        vprev[...] = jnp.zeros_like(vprev)

    w = WINDOW
    lane = lax.broadcasted_iota(jnp.int32, (w, LANES), 1)
    rowi = lax.broadcasted_iota(jnp.int32, (w, LANES), 0)
    lo = lane < 64
    own = lane <= rowi
    keep = jnp.where(own, 1, jnp.where(n > 0, 1, 0)) > 0
    ones = jnp.ones((2 * w, LANES), BF16)

    kall, vall = [], []
    kcur, vcur = [], []
    for p in range(2):
        kcur += [x.astype(BF16) for x in _dup_halves(k_ref[:, p * LANES:(p + 1) * LANES], lo)]
        vcur += [x.astype(BF16) for x in _dup_halves(v_ref[:, p * LANES:(p + 1) * LANES], lo)]
    for kh in range(SWA_KV):
        kall.append(jnp.concatenate([kprev[kh], kcur[kh]], axis=0))
        vv = jnp.concatenate([vprev[kh], vcur[kh]], axis=0)
        vall.append(jnp.concatenate([vv, ones], axis=1))
    for kh in range(SWA_KV):
        kprev[kh] = kcur[kh]
        vprev[kh] = vcur[kh]

    slabs = []
    for c in range(SWA_HQ // 2):
        qc = q_ref[:, c * LANES:(c + 1) * LANES]
        slabs.append(jnp.where(lo, qc, 0.0).astype(BF16))
        slabs.append(jnp.where(lo, 0.0, qc).astype(BF16))
    scores = [lax.dot_general(jnp.concatenate(slabs[8 * kh:8 * kh + 8], axis=0), kall[kh],
                              (((1,), (1,)), ((), ())), preferred_element_type=F32)
              for kh in range(SWA_KV)]
    folded = []
    for hq in range(SWA_HQ):
        sh = scores[hq // 8][(hq % 8) * w:(hq % 8 + 1) * w]
        folded.append(jnp.where(keep, jnp.where(own, sh[:, w:], sh[:, :w]), -1e30))
    maxes = [jnp.max(f, axis=-1, keepdims=True) for f in folded]
    probs = [jnp.exp2(f - m) for f, m in zip(folded, maxes)]
    sinks2 = [jnp.exp2(sink_ref[hq] * LOG2E - maxes[hq]) for hq in range(SWA_HQ)]
    ps = [jnp.concatenate([jnp.where(own, 0.0, p), jnp.where(own, p, 0.0)], axis=1).astype(BF16) for p in probs]
    outs = [jnp.dot(jnp.concatenate(ps[8 * kh:8 * kh + 8], axis=0), vall[kh], preferred_element_type=F32)
            for kh in range(SWA_KV)]
    normed = []
    for hq in range(SWA_HQ):
        oh = outs[hq // 8][(hq % 8) * w:(hq % 8 + 1) * w]
        normed.append(oh[:, :LANES] * (1.0 / (oh[:, LANES:] + sinks2[hq])))
    for c in range(SWA_HQ // 2):
        o_ref[:, c * LANES:(c + 1) * LANES] = (
            jnp.where(lo, normed[2 * c], normed[2 * c + 1]) * g_ref[:, c * LANES:(c + 1) * LANES]
        ).astype(o_ref.dtype)


def swa_prompt(h, sinks, batch, seq):
    m = h.shape[0]
    nb = seq // WINDOW
    kvw = SWA_KV * SWA_HD
    row = lambda b, n: b * nb + n
    return pl.pallas_call(
        _swa_prompt_kernel,
        grid=(batch, nb),
        in_specs=[
            pl.BlockSpec(memory_space=pltpu.SMEM),
            pl.BlockSpec((WINDOW, 2048), lambda b, n: (row(b, n), H_SQ // 2048)),
            pl.BlockSpec((WINDOW, 2048), lambda b, n: (row(b, n), H_SG // 2048)),
            pl.BlockSpec((WINDOW, kvw), lambda b, n: (row(b, n), H_SK // kvw)),
            pl.BlockSpec((WINDOW, kvw), lambda b, n: (row(b, n), H_SV // kvw)),
        ],
        out_specs=pl.BlockSpec((WINDOW, 2048), lambda b, n: (row(b, n), 0)),
        out_shape=jax.ShapeDtypeStruct((m, SWA_HQ * SWA_HD), BF16),
        scratch_shapes=[pltpu.VMEM((SWA_KV, WINDOW, LANES), BF16), pltpu.VMEM((SWA_KV, WINDOW, LANES), BF16)],
        compiler_params=_cparams(("arbitrary", "arbitrary")),
        name="swa_prompt",
    )(sinks, h, h, h, h)


def _mem_prompt_kernel(q_ref, g_ref, k_ref, v_ref, o_ref):
    scale = MEM_HD ** -0.5
    for h in range(MEM_H):
        sl = slice(h * MEM_HD, (h + 1) * MEM_HD)
        q = (q_ref[:, sl] * scale).astype(BF16)
        k = k_ref[:, sl].astype(BF16)
        v = v_ref[:, sl].astype(BF16)
        s = lax.dot_general(q, k, (((1,), (1,)), ((), ())), preferred_element_type=F32)
        m = jnp.max(s, axis=-1, keepdims=True)
        p = jnp.exp(s - m)
        den = jnp.sum(p, axis=-1, keepdims=True)
        o = jnp.dot(p.astype(BF16), v, preferred_element_type=F32) * (1.0 / den)
        o_ref[:, sl] = (o * g_ref[:, sl]).astype(o_ref.dtype)


def mem_attn_prompt(h, kv, batch, seq, tq):
    nt = seq // tq
    w = MEM_H * MEM_HD
    row = lambda b, t: b * nt + t
    return pl.pallas_call(
        _mem_prompt_kernel,
        grid=(batch, nt),
        in_specs=[
            pl.BlockSpec((tq, w), lambda b, t: (row(b, t), H_MQ // w)),
            pl.BlockSpec((tq, w), lambda b, t: (row(b, t), H_MG // w)),
            pl.BlockSpec((MEM_LEN, w), lambda b, t: (b, 0)),
            pl.BlockSpec((MEM_LEN, w), lambda b, t: (b, 1)),
        ],
        out_specs=pl.BlockSpec((tq, w), lambda b, t: (row(b, t), 0)),
        out_shape=jax.ShapeDtypeStruct((h.shape[0], w), BF16),
        compiler_params=_cparams(("parallel", "parallel")),
        name="mem_prompt",
    )(h, h, kv, kv)


def _log_sigmoid(x):
    return jnp.minimum(x, 0.0) - jnp.log1p(jnp.exp(-jnp.abs(x)))


def _gla_prompt_kernel(q_ref, k_ref, v_ref, g_ref, lr_ref, wg_ref, bg_ref, gn_ref, lc_ref,
                       o_ref, so_ref, st, qs, ks_, bs_, am, qt, kt, qh, eb, ob):
    t = pl.program_id(1)
    tb = q_ref.shape[0]
    ch = GLA_CHUNK
    nc = tb // ch
    ng = tb // SUB
    nsub = ch // SUB
    kw = GLA_H * GLA_DK
    hk = lambda hd: slice(hd * GLA_DK, (hd + 1) * GLA_DK)
    hv = lambda hd: slice(hd * GLA_DV, (hd + 1) * GLA_DV)

    @pl.when(t == 0)
    def _():
        st[...] = jnp.zeros_like(st)

    lane = lax.broadcasted_iota(jnp.int32, (ng, LANES), 1)
    grp = lax.broadcasted_iota(jnp.int32, (ng, LANES), 0)
    rel = lane - SUB * (grp % nsub)
    rr = lax.broadcasted_iota(jnp.int32, (ch, LANES), 0) // SUB
    cc = lax.broadcasted_iota(jnp.int32, (ch, LANES), 1)
    off_mask = jnp.where(cc >= 8 * rr * (rr - 1), jnp.where(cc < 8 * rr * (rr + 1), 1, 0), 0) > 0

    x = jnp.dot(lr_ref[...].astype(BF16), wg_ref[...], preferred_element_type=F32) + bg_ref[...]
    g2 = _log_sigmoid(x) * (LOG2E / GLA_TAU)
    g_hi = g2.astype(BF16)
    r1 = g2 - g_hi.astype(F32)
    g_mid = r1.astype(BF16)
    g_lo = (r1 - g_mid.astype(F32)).astype(BF16)
    cum = jnp.dot(lc_ref[...], jnp.concatenate([g_hi, g_mid, g_lo], axis=1), preferred_element_type=F32)
    cum = cum[:, :kw] + cum[:, kw:2 * kw] + cum[:, 2 * kw:]
    b, bl, bsb = cum[:tb], cum[tb:2 * tb], cum[2 * tb:]
    q = q_ref[...] * (GLA_DK ** -0.5)
    k = k_ref[...]
    for hd in range(GLA_H):
        qs[hd] = q[:, hk(hd)]
        ks_[hd] = k[:, hk(hd)]
        bs_[hd] = b[:, hk(hd)]
    qt[...] = (q * jnp.exp2(b)).astype(BF16)
    kt[...] = (k * jnp.exp2(bl - b)).astype(BF16)
    eb[...] = jnp.exp2(bl)
    qh[...] = (q * jnp.exp2(b - bsb)).astype(BF16)

    def rows_i(ref, i):
        return jnp.concatenate([ref[hd, pl.ds(i, ng, stride=SUB), :] for hd in range(GLA_H)], axis=1)

    qd = [rows_i(qs, i) for i in range(SUB)]
    kd = [rows_i(ks_, i) for i in range(SUB)]
    bd = [rows_i(bs_, i) for i in range(SUB)]
    for i in range(SUB):
        a_i = [jnp.zeros((ng, LANES), F32) for _ in range(GLA_H)]
        for j in range(i + 1):
            tt = qd[i] * kd[j] * jnp.exp2(bd[i] - bd[j])
            for hd in range(GLA_H):
                col = jnp.sum(tt[:, hk(hd)], axis=-1, keepdims=True)
                a_i[hd] = jnp.where(rel == j, col, a_i[hd])
        for hd in range(GLA_H):
            am[hd, pl.ds(i, ng, stride=SUB), :] = a_i[hd]

    kalls = []
    for cidx in range(nc):
        r0 = cidx * ch
        kc, bc = k[r0:r0 + ch], b[r0:r0 + ch]
        parts = []
        for sub in range(1, nsub):
            s_i = bsb[r0 + sub * SUB:r0 + sub * SUB + 1, :]
            parts.append((kc[:sub * SUB] * jnp.exp2(s_i - bc[:sub * SUB])).astype(BF16))
        parts.append(jnp.zeros((LANES - sum(p.shape[0] for p in parts), kw), BF16))
        kalls.append(jnp.concatenate(parts, axis=0))
    scs = {}
    for cidx in range(nc):
        for hd in range(GLA_H):
            scs[cidx, hd] = lax.dot_general(qh[cidx * ch:(cidx + 1) * ch, hk(hd)], kalls[cidx][:, hk(hd)],
                                            (((1,), (1,)), ((), ())), preferred_element_type=F32)
    for cidx in range(nc):
        rows = slice(cidx * ch, (cidx + 1) * ch)
        for hd in range(GLA_H):
            lhs = jnp.concatenate([jnp.where(off_mask, scs[cidx, hd], 0.0).astype(BF16),
                                   am[hd, rows, :].astype(BF16)], axis=1)
            vc = v_ref[rows, hv(hd)].astype(BF16)
            rhs = jnp.concatenate(
                [vc[:s * SUB] for s in range(1, nsub)]
                + [jnp.zeros((LANES - SUB * nsub * (nsub - 1) // 2, GLA_DV), BF16), vc,
                   jnp.zeros((LANES - ch, GLA_DV), BF16)], axis=0)
            ob[hd, rows, :] = jnp.dot(lhs, rhs, preferred_element_type=F32)

    for cidx in range(nc):
        rows = slice(cidx * ch, (cidx + 1) * ch)
        for hd in range(GLA_H):
            vc = v_ref[rows, hv(hd)].astype(BF16)
            sb = st[hd]
            ob[hd, rows, :] += lax.dot_general(qt[rows, hk(hd)], sb.astype(BF16), (((1,), (1,)), ((), ())),
                                               preferred_element_type=F32)
            upd = lax.dot_general(vc, kt[rows, hk(hd)], (((0,), (0,)), ((), ())), preferred_element_type=F32)
            st[hd] = sb * eb[cidx * ch:cidx * ch + 1, hk(hd)] + upd

    for hd in range(GLA_H):
        vs = slice(hd * GLA_DV, (hd + 1) * GLA_DV)
        o = ob[hd]
        on = o * lax.rsqrt(jnp.mean(o * o, axis=-1, keepdims=True) + EPS) * gn_ref[:, vs]
        o_ref[:, vs] = (on * g_ref[:, vs]).astype(o_ref.dtype)

    @pl.when(t == pl.num_programs(1) - 1)
    def _():
        for hd in range(GLA_H):
            so_ref[hd] = st[hd].T


def gla_prompt(h, wg, bg, gn, batch, seq, tb):
    nt = seq // tb
    kw = GLA_H * GLA_DK
    vw = GLA_H * GLA_DV
    row = lambda b, t: b * nt + t
    r = np.arange(tb)[:, None]
    c = np.arange(tb)[None, :]
    same = (r // GLA_CHUNK) == (c // GLA_CHUNK)
    lcat = jnp.asarray(np.concatenate([same & (c <= r), same & (c >= 0), same & (c < (r // SUB) * SUB)], axis=0),
                       BF16)
    return pl.pallas_call(
        _gla_prompt_kernel,
        grid=(batch, nt),
        in_specs=[
            pl.BlockSpec((tb, kw), lambda b, t: (row(b, t), H_GQ // kw)),
            pl.BlockSpec((tb, kw), lambda b, t: (row(b, t), H_GK // kw)),
            pl.BlockSpec((tb, vw), lambda b, t: (row(b, t), H_GV // vw)),
            pl.BlockSpec((tb, vw), lambda b, t: (row(b, t), H_GG // vw)),
            pl.BlockSpec((tb, LANES), lambda b, t: (row(b, t), H_GLR // LANES)),
            pl.BlockSpec((LANES, kw), lambda b, t: (0, 0)),
            pl.BlockSpec((1, kw), lambda b, t: (0, 0)),
            pl.BlockSpec((1, vw), lambda b, t: (0, 0)),
            pl.BlockSpec((3 * tb, tb), lambda b, t: (0, 0)),
        ],
        out_specs=[
            pl.BlockSpec((tb, vw), lambda b, t: (row(b, t), 0)),
            pl.BlockSpec((None, GLA_H, GLA_DK, GLA_DV), lambda b, t: (b, 0, 0, 0)),
        ],
        out_shape=[
            jax.ShapeDtypeStruct((h.shape[0], vw), BF16),
            jax.ShapeDtypeStruct((batch, GLA_H, GLA_DK, GLA_DV), F32),
        ],
        scratch_shapes=[
            pltpu.VMEM((GLA_H, GLA_DV, GLA_DK), F32),
            pltpu.VMEM((GLA_H, tb, GLA_DK), F32),
            pltpu.VMEM((GLA_H, tb, GLA_DK), F32),
            pltpu.VMEM((GLA_H, tb, GLA_DK), F32),
            pltpu.VMEM((GLA_H, tb, LANES), F32),
            pltpu.VMEM((tb, kw), BF16),
            pltpu.VMEM((tb, kw), BF16),
            pltpu.VMEM((tb, kw), BF16),
            pltpu.VMEM((tb, kw), F32),
            pltpu.VMEM((GLA_H, tb, GLA_DV), F32),
        ],
        compiler_params=_cparams(("arbitrary", "arbitrary")),
        name="gla_prompt",
    )(h, h, h, h, h, wg, bg, gn, lcat)


def _swa_sample_kernel(q_ref, g_ref, k_ref, v_ref, kb_ref, vb_ref, sink_ref, o_ref, ko_ref, vo_ref):
    w = kb_ref.shape[0]
    lane = lax.broadcasted_iota(jnp.int32, (16, LANES), 1)
    rowi = lax.broadcasted_iota(jnp.int32, (w, LANES), 0)
    lo_w = lax.broadcasted_iota(jnp.int32, (w, LANES), 1) < 64
    lo = lane < 64

    kk, vv = [], []
    for p in range(2):
        sl = slice(p * LANES, (p + 1) * LANES)
        knew = k_ref[:, sl]
        vnew = v_ref[:, sl]
        kwin = jnp.where(rowi == w - 1, knew, pltpu.roll(kb_ref[:, sl], w - 1, 0))
        vwin = jnp.where(rowi == w - 1, vnew, pltpu.roll(vb_ref[:, sl], w - 1, 0))
        ko_ref[:, sl] = kwin
        vo_ref[:, sl] = vwin
        kk += [x.astype(BF16) for x in _dup_halves(kwin, lo_w)]
        vv += [x.astype(BF16) for x in _dup_halves(vwin, lo_w)]

    q = q_ref[...]
    qq = jnp.concatenate([jnp.where(lo, q, 0.0), jnp.where(lo, 0.0, q)], axis=0)
    grp = (lax.broadcasted_iota(jnp.int32, (32, LANES), 0) % 16) // 4
    s = jnp.zeros((32, w), F32)
    for kh in range(SWA_KV):
        s = s + lax.dot_general(jnp.where(grp == kh, qq, 0.0).astype(BF16), kk[kh],
                                (((1,), (1,)), ((), ())), preferred_element_type=F32)
    sk = sink_ref[...][:, 0:1] * LOG2E
    m = jnp.maximum(jnp.max(s, axis=-1, keepdims=True), sk)
    p = jnp.exp2(s - m)
    den = jnp.sum(p, axis=-1, keepdims=True) + jnp.exp2(sk - m)
    o = jnp.zeros((32, LANES), F32)
    for kh in range(SWA_KV):
        o = o + jnp.dot(jnp.where(grp == kh, p, 0.0).astype(BF16), vv[kh], preferred_element_type=F32)
    o = o * (1.0 / den)
    o_ref[...] = jnp.where(lo, o[0:16], o[16:32]) * g_ref[...]


def swa_sample(q, g, k, v, kbuf, vbuf, sink_rows):
    bd, wb, kvw = kbuf.shape
    return pl.pallas_call(
        _swa_sample_kernel,
        grid=(bd,),
        in_specs=[
            pl.BlockSpec((None, 16, LANES), lambda b: (b, 0, 0)),
            pl.BlockSpec((None, 16, LANES), lambda b: (b, 0, 0)),
            pl.BlockSpec((None, 1, kvw), lambda b: (b, 0, 0)),
            pl.BlockSpec((None, 1, kvw), lambda b: (b, 0, 0)),
            pl.BlockSpec((None, wb, kvw), lambda b: (b, 0, 0)),
            pl.BlockSpec((None, wb, kvw), lambda b: (b, 0, 0)),
            pl.BlockSpec((32, LANES), lambda b: (0, 0)),
        ],
        out_specs=[
            pl.BlockSpec((None, 16, LANES), lambda b: (b, 0, 0)),
            pl.BlockSpec((None, wb, kvw), lambda b: (b, 0, 0)),
            pl.BlockSpec((None, wb, kvw), lambda b: (b, 0, 0)),
        ],
        out_shape=[
            jax.ShapeDtypeStruct((bd, 16, LANES), F32),
            jax.ShapeDtypeStruct((bd, wb, kvw), F32),
            jax.ShapeDtypeStruct((bd, wb, kvw), F32),
        ],
        compiler_params=_cparams(("parallel",)),
        name="swa_sample",
    )(q, g, k, v, kbuf, vbuf, sink_rows)


def _mem_sample_kernel(q_ref, g_ref, k_ref, v_ref, o_ref):
    n = k_ref.shape[0]
    grp = 2 * MEM_H
    q8 = (q_ref[...] * (MEM_HD ** -0.5)).astype(BF16)
    r = lax.dot_general(q8, k_ref[...].astype(BF16), (((1,), (1,)), ((), ())), preferred_element_type=F32)
    row = lax.broadcasted_iota(jnp.int32, (grp, n), 0)
    col = lax.broadcasted_iota(jnp.int32, (grp, n), 1)
    own = (col % grp) == row
    a = jnp.where(own, r, 0.0)
    s = a + pltpu.roll(pltpu.roll(a, MEM_H, 0), n - MEM_H, 1)
    valid = jnp.where(own, jnp.where(row < MEM_H, 1, 0), 0) > 0
    sm = jnp.where(valid, s, -1e30)
    m = jnp.max(sm, axis=-1, keepdims=True)
    p = jnp.where(valid, jnp.exp(sm - m), 0.0)
    den = jnp.sum(p, axis=-1, keepdims=True)
    pn = p * (1.0 / jnp.where(den > 0.0, den, 1.0))
    p8 = pn + pltpu.roll(pltpu.roll(pn, MEM_H, 0), MEM_H, 1)
    o = jnp.dot(p8.astype(BF16), v_ref[...].astype(BF16), preferred_element_type=F32)
    o_ref[...] = o * g_ref[...]


def _cache_rows(c):
    depth, bd, mlen, nh, hd = c.shape
    return c.reshape(depth, bd, mlen, nh, hd // LANES, LANES).transpose(0, 1, 2, 4, 3, 5).reshape(
        depth, bd, mlen * nh * (hd // LANES), LANES)


def _half_head_rows(x):
    bd = x.shape[0]
    return x.reshape(bd, MEM_H, MEM_HD // LANES, LANES).transpose(0, 2, 1, 3).reshape(bd, -1, LANES)


def mem_sample(q, g, k, v, layer):
    bd = q.shape[0]
    rows = MEM_LEN * MEM_H * (MEM_HD // LANES)
    cache_spec = pl.BlockSpec((None, None, rows, LANES), lambda b: (layer, b, 0, 0))
    vec_spec = pl.BlockSpec((None, 2 * MEM_H, LANES), lambda b: (b, 0, 0))
    o = pl.pallas_call(
        _mem_sample_kernel,
        grid=(bd,),
        in_specs=[vec_spec, vec_spec, cache_spec, cache_spec],
        out_specs=vec_spec,
        out_shape=jax.ShapeDtypeStruct((bd, 2 * MEM_H, LANES), F32),
        compiler_params=_cparams(("parallel",)),
        name="mem_sample",
    )(_half_head_rows(q), _half_head_rows(g), _cache_rows(k), _cache_rows(v))
    return o.reshape(bd, MEM_HD // LANES, MEM_H, LANES).transpose(0, 2, 1, 3).reshape(bd, MEM_H * MEM_HD)


def _gla_sample_kernel(q_ref, k_ref, v_ref, g_ref, lr_ref, wg_ref, bg_ref, gn_ref, s_ref, o_ref, so_ref):
    rowk = lax.broadcasted_iota(jnp.int32, (8, GLA_DK), 0)
    lr = jnp.broadcast_to(lr_ref[...], (8, LANES)).astype(BF16)
    xa = jnp.dot(lr, wg_ref[...], preferred_element_type=F32) + bg_ref[...]
    x = jnp.zeros((8, GLA_DK), F32)
    for hd in range(GLA_H):
        x = jnp.where(rowk == hd, xa[:, hd * GLA_DK:(hd + 1) * GLA_DK], x)
    gl = _log_sigmoid(x) * (1.0 / GLA_TAU)
    eg = jnp.exp(gl)
    q = q_ref[...] * (GLA_DK ** -0.5)
    k = k_ref[...]
    v = v_ref[...]
    qe = q * eg
    eye = lax.broadcasted_iota(jnp.int32, (GLA_DK, GLA_DK), 0) == lax.broadcasted_iota(jnp.int32, (GLA_DK, GLA_DK), 1)

    def col(row):
        return jnp.sum(jnp.where(eye, jnp.broadcast_to(row, (GLA_DK, GLA_DK)), 0.0), axis=-1, keepdims=True)

    o = jnp.sum(q * k, axis=-1, keepdims=True) * v
    for hd in range(GLA_H):
        s0 = s_ref[hd]
        o = o + jnp.dot(jnp.where(rowk == hd, qe, 0.0).astype(BF16), s0.astype(BF16), preferred_element_type=F32)
        so_ref[hd] = s0 * col(eg[hd:hd + 1, :]) + col(k[hd:hd + 1, :]) * v[hd:hd + 1, :]
    on = o * lax.rsqrt(jnp.mean(o * o, axis=-1, keepdims=True) + EPS) * gn_ref[...]
    o_ref[...] = on * g_ref[...]


def gla_sample(q, k, v, g, lr, wg, bg, gn, state, layer):
    bd = q.shape[0]
    kw = GLA_H * GLA_DK
    return pl.pallas_call(
        _gla_sample_kernel,
        grid=(bd,),
        in_specs=[
            pl.BlockSpec((None, 8, GLA_DK), lambda b: (b, 0, 0)),
            pl.BlockSpec((None, 8, GLA_DK), lambda b: (b, 0, 0)),
            pl.BlockSpec((None, 8, GLA_DV), lambda b: (b, 0, 0)),
            pl.BlockSpec((None, 8, GLA_DV), lambda b: (b, 0, 0)),
            pl.BlockSpec((None, 1, LANES), lambda b: (b, 0, 0)),
            pl.BlockSpec((LANES, kw), lambda b: (0, 0)),
            pl.BlockSpec((1, kw), lambda b: (0, 0)),
            pl.BlockSpec((8, GLA_DV), lambda b: (0, 0)),
            pl.BlockSpec((None, None, GLA_H, GLA_DK, GLA_DV), lambda b: (layer, b, 0, 0, 0)),
        ],
        out_specs=[
            pl.BlockSpec((None, 8, GLA_DV), lambda b: (b, 0, 0)),
            pl.BlockSpec((None, GLA_H, GLA_DK, GLA_DV), lambda b: (b, 0, 0, 0)),
        ],
        out_shape=[
            jax.ShapeDtypeStruct((bd, 8, GLA_DV), F32),
            jax.ShapeDtypeStruct(state.shape[1:], F32),
        ],
        compiler_params=_cparams(("parallel",)),
        name="gla_sample",
    )(q, k, v, g, lr, wg, bg, gn, state)


def _heads8(x, width):
    bd = x.shape[0]
    x = x.reshape(bd, -1, width)
    return jnp.pad(x, ((0, 0), (0, 8 - x.shape[1]), (0, 0)))


def kernel(x_prompt, mem_prompt, x_sample, cache_swa_k, cache_swa_v, state_gla, cache_mem_k, cache_mem_v,
           norm_g, w_in, attn_sinks, gla_w_gate, gla_b_gate, gla_norm_g, mem_norm_g, w_mem_kv, w_out,
           final_norm_g):
    batch, seq, d = x_prompt.shape
    bd = x_sample.shape[0]
    m = batch * seq
    kvw = SWA_KV * SWA_HD
    memw = MEM_H * MEM_HD

    def variants(cos, sin):
        return (jnp.stack([jnp.ones_like(cos), cos, cos * SWA_QSCALE]),
                jnp.stack([jnp.zeros_like(sin), sin, sin * SWA_QSCALE]))

    tabs_p = variants(*_rope_tables(jnp.arange(seq)))
    tabs_s = variants(*[jnp.broadcast_to(t, (bd, LANES)) for t in _rope_tables(PAST_LEN + jnp.arange(1))])
    perm = np.array([2 * (r % 16) + r // 16 for r in range(32)])
    w_in_t = jnp.transpose(w_in, (0, 2, 1))

    xp = x_prompt.reshape(m, d)
    xs = x_sample.reshape(bd, d)
    memx = mem_prompt.reshape(batch * MEM_LEN, d)

    outs = {k: [] for k in ("kp", "vp", "sp", "mkp", "mvp", "ks", "vs", "ss")}
    for l in range(DEPTH):
        wg = jnp.pad(gla_w_gate[l], ((0, LANES - GLA_RANK), (0, 0))).astype(BF16)
        bg = gla_b_gate[l].reshape(1, -1)
        gn = gla_norm_g[l].reshape(1, -1)

        kv = project(rmsnorm(memx, mem_norm_g[l], BF16, 256), w_mem_kv, l, name="mem_kv")
        xn = rmsnorm(xp, norm_g[l], BF16, 256)
        xns = rmsnorm(xs, norm_g[l], BF16, bd)
        h, hs = project(xn, w_in_t, l, n=H_END, tiles=_IN_TILES, tables=tabs_p, tables_small=tabs_s,
                        a_small=xns, name="in_proj")

        o_swa_p = swa_prompt(h, attn_sinks[l], batch, seq)
        o_gla_p, sp = gla_prompt(h, wg, bg, gn, batch, seq, 256)
        o_mem_p = mem_attn_prompt(h, kv, batch, seq, 512)
        last = h.reshape(batch, seq, H_END)[:, seq - WINDOW:]
        outs["kp"].append(last[:, :, H_SK:H_SK + kvw].reshape(batch, WINDOW, SWA_KV, SWA_HD))
        outs["vp"].append(last[:, :, H_SV:H_SV + kvw].reshape(batch, WINDOW, SWA_KV, SWA_HD))
        outs["sp"].append(sp)
        outs["mkp"].append(kv[:, :memw].reshape(batch, MEM_LEN, MEM_H, MEM_HD))
        outs["mvp"].append(kv[:, memw:].reshape(batch, MEM_LEN, MEM_H, MEM_HD))

        wb = cache_swa_k.shape[2]
        o_swa, kbs, vbs = swa_sample(
            hs[:, H_SQ:H_SQ + 2048].reshape(bd, 16, LANES), hs[:, H_SG:H_SG + 2048].reshape(bd, 16, LANES),
            hs[:, H_SK:H_SK + kvw].reshape(bd, 1, kvw), hs[:, H_SV:H_SV + kvw].reshape(bd, 1, kvw),
            cache_swa_k[l].reshape(bd, wb, kvw), cache_swa_v[l].reshape(bd, wb, kvw),
            jnp.broadcast_to(attn_sinks[l][perm][:, None], (32, LANES)))
        o_gla, ss = gla_sample(
            _heads8(hs[:, H_GQ:H_GQ + 512], GLA_DK), _heads8(hs[:, H_GK:H_GK + 512], GLA_DK),
            _heads8(hs[:, H_GV:H_GV + 1024], GLA_DV), _heads8(hs[:, H_GG:H_GG + 1024], GLA_DV),
            hs[:, H_GLR:H_GLR + LANES].reshape(bd, 1, LANES), wg, bg,
            jnp.pad(gla_norm_g[l].reshape(GLA_H, GLA_DV), ((0, 4), (0, 0))), state_gla, l)
        o_mem = mem_sample(hs[:, H_MQ:H_MQ + memw], hs[:, H_MG:H_MG + memw], cache_mem_k, cache_mem_v, l)
        mix_s = jnp.concatenate([o_swa.reshape(bd, 2048), o_gla[:, :GLA_H].reshape(bd, 1024), o_mem],
                                axis=1).astype(BF16)
        outs["ks"].append(kbs.reshape(bd, wb, SWA_KV, SWA_HD))
        outs["vs"].append(vbs.reshape(bd, wb, SWA_KV, SWA_HD))
        outs["ss"].append(ss)

        xp, xs = project([o_swa_p, o_gla_p, o_mem_p], w_out, l, a_small=mix_s, res=xp, res_small=xs,
                         name="out_proj")

    y_prompt = rmsnorm(xp, final_norm_g, F32, 256).reshape(batch, seq, d)
    y_sample = rmsnorm(xs, final_norm_g, F32, bd).reshape(bd, 1, d)
    st = lambda k: jnp.stack(outs[k])
    return (y_prompt, y_sample, st("kp"), st("vp"), st("sp"), st("mkp"), st("mvp"), st("ks"), st("vs"), st("ss"))
```

```python
import functools
import math

import jax
import jax.numpy as jnp
import numpy as np
from jax import lax
from jax.experimental import pallas as pl
from jax.experimental.pallas import tpu as pltpu

F32 = jnp.float32
BF16 = jnp.bfloat16

D_MODEL = 4096
DEPTH = 2
EPS = 1e-6
PAST_LEN = 16384
WINDOW = 128
ROPE_THETA = 10000.0
SWA_HD = 64
SWA_HQ = 32
SWA_KV = 4
GLA_H = 4
GLA_DK = 128
GLA_DV = 256
GLA_RANK = 16
GLA_TAU = 16.0
MEM_LEN = 256
MEM_H = 4
MEM_HD = 256
LANES = 128
GLA_CHUNK = 64
SUB = 16
LOG2E = math.log2(math.e)

TM = 2048
TN = 512
H_SQ, H_SG, H_GV, H_GG, H_MQ, H_MG, H_GQ, H_GK, H_SK, H_SV, H_GLR, H_END = (
    0, 2048, 4096, 5120, 6144, 7168, 8192, 8704, 9216, 9472, 9728, 9856)
_IN_PIECES = ((0, 2048), (2560, 2048), (5632, 1024), (6672, 1024), (7696, 1024), (8720, 1024), (4608, 512),
              (5120, 512), (2048, 512), (6656, 128))
IN_TILE_SRC = tuple(src + TN * t for src, width in _IN_PIECES for t in range(-(-width // TN)))
VMEM_LIMIT = 58 * 1024 * 1024


def _cparams(sem):
    return pltpu.CompilerParams(dimension_semantics=sem, vmem_limit_bytes=VMEM_LIMIT)


def _sigmoid(x):
    return 1.0 / (1.0 + jnp.exp(-x))


def _rmsnorm_kernel(x_ref, g_ref, o_ref):
    x = x_ref[...]
    ms = jnp.mean(x * x, axis=-1, keepdims=True)
    o_ref[...] = (x * lax.rsqrt(ms + EPS) * g_ref[...]).astype(o_ref.dtype)


def rmsnorm(x, g, out_dtype, tm):
    m, d = x.shape
    return pl.pallas_call(
        _rmsnorm_kernel,
        grid=(m // tm,),
        in_specs=[pl.BlockSpec((tm, d), lambda i: (i, 0)), pl.BlockSpec((1, d), lambda i: (0, 0))],
        out_specs=pl.BlockSpec((tm, d), lambda i: (i, 0)),
        out_shape=jax.ShapeDtypeStruct((m, d), out_dtype),
        compiler_params=_cparams(("parallel",)),
        name="rmsnorm",
    )(x, g.reshape(1, d))


def _proj_kernel(*refs, n_parts, has_small, has_res, tn, last_valid, w_rows):
    refs = list(refs)
    if w_rows:
        refs.pop(0)
    a_refs = [refs.pop(0) for _ in range(n_parts)]
    as_ref = refs.pop(0) if has_small else None
    w_ref = refs.pop(0)
    r_ref = refs.pop(0) if has_res else None
    rs_ref = refs.pop(0) if (has_res and has_small) else None
    o_ref = refs.pop(0)
    os_ref = refs.pop(0) if has_small else None
    i = pl.program_id(0)
    j = pl.program_id(1)
    nj = pl.num_programs(1)

    def compute(width):
        if w_rows:
            w = w_ref[:width, :].astype(BF16)
            mm = lambda a: lax.dot_general(a, w, (((1,), (1,)), ((), ())), preferred_element_type=F32)
            acc = mm(a_refs[0][...])
        else:
            w = w_ref[:, :width].astype(BF16)
            mm = lambda a: jnp.dot(a, w, preferred_element_type=F32)
            acc, k0 = None, 0
            for a_ref in a_refs:
                kp = a_ref.shape[1]
                part = jnp.dot(a_ref[...], w[k0:k0 + kp], preferred_element_type=F32)
                acc = part if acc is None else acc + part
                k0 += kp
        if has_res:
            acc = acc + r_ref[:, :width]
        o_ref[:, :width] = acc
        if has_small:
            @pl.when(i == 0)
            def _():
                accs = mm(as_ref[...])
                if has_res:
                    accs = accs + rs_ref[:, :width]
                os_ref[:, :width] = accs

    if last_valid == tn:
        compute(tn)
    else:
        pl.when(j < nj - 1)(lambda: compute(tn))
        pl.when(j == nj - 1)(lambda: compute(last_valid))


def project(a, w3, layer, n=None, tile_rows=None, a_small=None, res=None, res_small=None, name="proj"):
    a_parts = list(a) if isinstance(a, (tuple, list)) else [a]
    m = a_parts[0].shape[0]
    k = sum(p.shape[1] for p in a_parts)
    w_rows = tile_rows is not None
    if n is None:
        n = w3.shape[2]
    tm = min(TM, m)
    nj = pl.cdiv(n, TN)
    last_valid = n - (nj - 1) * TN
    has_small = a_small is not None
    has_res = res is not None
    small_idx = lambda i, j, *_: (0, jnp.where(i == 0, j, nj - 1))
    in_specs = [pl.BlockSpec((tm, p.shape[1]), lambda i, j, *_: (i, 0), pipeline_mode=pl.Buffered(1))
                for p in a_parts]
    args = list(a_parts)
    if has_small:
        ms = a_small.shape[0]
        in_specs.append(pl.BlockSpec((ms, k), lambda i, j, *_: (0, 0)))
        args.append(a_small)
    if w_rows:
        in_specs.append(pl.BlockSpec((None, pl.Element(TN), pl.Element(k)),
                                     lambda i, j, offs: (layer, pl.multiple_of(offs[j], 16), 0)))
    else:
        in_specs.append(pl.BlockSpec((None, k, TN), lambda i, j: (layer, 0, j)))
    args.append(w3)
    if has_res:
        in_specs.append(pl.BlockSpec((tm, TN), lambda i, j, *_: (i, j)))
        args.append(res)
        if has_small:
            in_specs.append(pl.BlockSpec((ms, TN), small_idx))
            args.append(res_small)
    out_specs = [pl.BlockSpec((tm, TN), lambda i, j, *_: (i, j))]
    out_shape = [jax.ShapeDtypeStruct((m, n), F32)]
    if has_small:
        out_specs.append(pl.BlockSpec((ms, TN), small_idx))
        out_shape.append(jax.ShapeDtypeStruct((ms, n), F32))
    kern = functools.partial(_proj_kernel, n_parts=len(a_parts), has_small=has_small, has_res=has_res, tn=TN,
                             last_valid=last_valid, w_rows=w_rows)
    if w_rows:
        grid_spec = pltpu.PrefetchScalarGridSpec(num_scalar_prefetch=1, grid=(m // tm, nj), in_specs=in_specs,
                                                 out_specs=out_specs)
        args = [jnp.asarray(tile_rows, jnp.int32)] + args
    else:
        grid_spec = pl.GridSpec(grid=(m // tm, nj), in_specs=in_specs, out_specs=out_specs)
    outs = pl.pallas_call(
        kern,
        grid_spec=grid_spec,
        out_shape=out_shape,
        compiler_params=_cparams(("arbitrary", "arbitrary")),
        name=name,
    )(*args)
    return outs if has_small else outs[0]


def _rope_tables(pos):
    half = SWA_HD // 2
    inv = ROPE_THETA ** (-jnp.arange(half, dtype=F32) / half)
    ang = pos.astype(F32)[:, None] * inv[None, :]
    cos, sin = jnp.cos(ang), jnp.sin(ang)
    return jnp.concatenate([cos, cos, cos, cos], axis=-1), jnp.concatenate([-sin, sin, -sin, sin], axis=-1)


def _rope128(x, cos, sin, first_half):
    partner = jnp.where(first_half, pltpu.roll(x, LANES - 32, 1), pltpu.roll(x, 32, 1))
    return x * cos + partner * sin


def _dup_halves(x, lo):
    r = pltpu.roll(x, 64, 1)
    return jnp.where(lo, x, r), jnp.where(lo, r, x)


def _swa_prompt_kernel(sink_ref, q_ref, g_ref, k_ref, v_ref, cos_ref, sin_ref,
                       o_ref, ko_ref, vo_ref, kprev, vprev):
    n = pl.program_id(1)

    @pl.when(n == 0)
    def _():
        kprev[...] = jnp.zeros_like(kprev)
        vprev[...] = jnp.zeros_like(vprev)

    w = WINDOW
    cos = cos_ref[...]
    sin = sin_ref[...]
    lane = lax.broadcasted_iota(jnp.int32, (w, LANES), 1)
    rowi = lax.broadcasted_iota(jnp.int32, (w, LANES), 0)
    first_half = (lane & 32) == 0
    lo = lane < 64
    own = lane <= rowi
    keep = jnp.where(own, 1, jnp.where(n > 0, 1, 0)) > 0
    ones = jnp.ones((2 * w, LANES), BF16)

    kall, vall = [], []
    kcur, vcur = [], []
    for p in range(2):
        kp = _rope128(k_ref[:, p * LANES:(p + 1) * LANES], cos, sin, first_half)
        vp = v_ref[:, p * LANES:(p + 1) * LANES]
        ko_ref[:, p * LANES:(p + 1) * LANES] = kp
        vo_ref[:, p * LANES:(p + 1) * LANES] = vp
        kcur += [x.astype(BF16) for x in _dup_halves(kp, lo)]
        vcur += [x.astype(BF16) for x in _dup_halves(vp, lo)]
    for kh in range(SWA_KV):
        kall.append(jnp.concatenate([kprev[kh], kcur[kh]], axis=0))
        vv = jnp.concatenate([vprev[kh], vcur[kh]], axis=0)
        vall.append(jnp.concatenate([vv, ones], axis=1))
    for kh in range(SWA_KV):
        kprev[kh] = kcur[kh]
        vprev[kh] = vcur[kh]

    scale = (SWA_HD ** -0.5) * LOG2E
    slabs = []
    for c in range(SWA_HQ // 2):
        qc = _rope128(q_ref[:, c * LANES:(c + 1) * LANES], cos, sin, first_half) * scale
        slabs.append(jnp.where(lo, qc, 0.0).astype(BF16))
        slabs.append(jnp.where(lo, 0.0, qc).astype(BF16))
    scores = [lax.dot_general(jnp.concatenate(slabs[8 * kh:8 * kh + 8], axis=0), kall[kh],
                              (((1,), (1,)), ((), ())), preferred_element_type=F32)
              for kh in range(SWA_KV)]
    folded = []
    for hq in range(SWA_HQ):
        sh = scores[hq // 8][(hq % 8) * w:(hq % 8 + 1) * w]
        folded.append(jnp.where(keep, jnp.where(own, sh[:, w:], sh[:, :w]), -1e30))
    maxes = [jnp.max(f, axis=-1, keepdims=True) for f in folded]
    probs = [jnp.exp2(f - m) for f, m in zip(folded, maxes)]
    sinks2 = [jnp.exp2(sink_ref[hq] * LOG2E - maxes[hq]) for hq in range(SWA_HQ)]
    ps = [jnp.concatenate([jnp.where(own, 0.0, p), jnp.where(own, p, 0.0)], axis=1).astype(BF16) for p in probs]
    outs = [jnp.dot(jnp.concatenate(ps[8 * kh:8 * kh + 8], axis=0), vall[kh], preferred_element_type=F32)
            for kh in range(SWA_KV)]
    normed = []
    for hq in range(SWA_HQ):
        oh = outs[hq // 8][(hq % 8) * w:(hq % 8 + 1) * w]
        normed.append(oh[:, :LANES] * (1.0 / (oh[:, LANES:] + sinks2[hq])))
    for c in range(SWA_HQ // 2):
        gt = g_ref[:, c * LANES:(c + 1) * LANES]
        o_ref[:, c * LANES:(c + 1) * LANES] = (
            jnp.where(lo, normed[2 * c], normed[2 * c + 1]) * (gt * _sigmoid(gt))).astype(o_ref.dtype)


def swa_prompt(h, sinks, cos, sin, batch, seq):
    m = h.shape[0]
    nb = seq // WINDOW
    kvw = SWA_KV * SWA_HD
    row = lambda b, n: b * nb + n
    return pl.pallas_call(
        _swa_prompt_kernel,
        grid=(batch, nb),
        in_specs=[
            pl.BlockSpec(memory_space=pltpu.SMEM),
            pl.BlockSpec((WINDOW, 2048), lambda b, n: (row(b, n), H_SQ // 2048)),
            pl.BlockSpec((WINDOW, 2048), lambda b, n: (row(b, n), H_SG // 2048)),
            pl.BlockSpec((WINDOW, kvw), lambda b, n: (row(b, n), H_SK // kvw)),
            pl.BlockSpec((WINDOW, kvw), lambda b, n: (row(b, n), H_SV // kvw)),
            pl.BlockSpec((WINDOW, LANES), lambda b, n: (n, 0)),
            pl.BlockSpec((WINDOW, LANES), lambda b, n: (n, 0)),
        ],
        out_specs=[
            pl.BlockSpec((WINDOW, 2048), lambda b, n: (row(b, n), 0)),
            pl.BlockSpec((None, WINDOW, kvw), lambda b, n: (b, 0, 0)),
            pl.BlockSpec((None, WINDOW, kvw), lambda b, n: (b, 0, 0)),
        ],
        out_shape=[
            jax.ShapeDtypeStruct((m, SWA_HQ * SWA_HD), BF16),
            jax.ShapeDtypeStruct((batch, WINDOW, kvw), F32),
            jax.ShapeDtypeStruct((batch, WINDOW, kvw), F32),
        ],
        scratch_shapes=[pltpu.VMEM((SWA_KV, WINDOW, LANES), BF16), pltpu.VMEM((SWA_KV, WINDOW, LANES), BF16)],
        compiler_params=_cparams(("arbitrary", "arbitrary")),
        name="swa_prompt",
    )(sinks, h, h, h, h, cos, sin)


def _mem_prompt_kernel(q_ref, g_ref, k_ref, v_ref, o_ref):
    scale = MEM_HD ** -0.5
    for h in range(MEM_H):
        sl = slice(h * MEM_HD, (h + 1) * MEM_HD)
        q = (q_ref[:, sl] * scale).astype(BF16)
        k = k_ref[:, sl].astype(BF16)
        v = v_ref[:, sl].astype(BF16)
        s = lax.dot_general(q, k, (((1,), (1,)), ((), ())), preferred_element_type=F32)
        m = jnp.max(s, axis=-1, keepdims=True)
        p = jnp.exp(s - m)
        den = jnp.sum(p, axis=-1, keepdims=True)
        o = jnp.dot(p.astype(BF16), v, preferred_element_type=F32) * (1.0 / den)
        g = g_ref[:, sl]
        o_ref[:, sl] = (o * (g * _sigmoid(g))).astype(o_ref.dtype)


def mem_attn_prompt(h, kv, batch, seq, tq):
    nt = seq // tq
    w = MEM_H * MEM_HD
    row = lambda b, t: b * nt + t
    return pl.pallas_call(
        _mem_prompt_kernel,
        grid=(batch, nt),
        in_specs=[
            pl.BlockSpec((tq, w), lambda b, t: (row(b, t), H_MQ // w)),
            pl.BlockSpec((tq, w), lambda b, t: (row(b, t), H_MG // w)),
            pl.BlockSpec((MEM_LEN, w), lambda b, t: (b, 0)),
            pl.BlockSpec((MEM_LEN, w), lambda b, t: (b, 1)),
        ],
        out_specs=pl.BlockSpec((tq, w), lambda b, t: (row(b, t), 0)),
        out_shape=jax.ShapeDtypeStruct((h.shape[0], w), BF16),
        compiler_params=_cparams(("parallel", "parallel")),
        name="mem_prompt",
    )(h, h, kv, kv)


def _log_sigmoid(x):
    return jnp.minimum(x, 0.0) - jnp.log1p(jnp.exp(-jnp.abs(x)))


def _gla_prompt_kernel(q_ref, k_ref, v_ref, g_ref, lr_ref, wg_ref, bg_ref, gn_ref, lc_ref,
                       o_ref, so_ref, st, qs, ks_, bs_, am, qt, kt, qh, eb, ob):
    t = pl.program_id(1)
    tb = q_ref.shape[0]
    ch = GLA_CHUNK
    nc = tb // ch
    ng = tb // SUB
    nsub = ch // SUB
    kw = GLA_H * GLA_DK
    hk = lambda hd: slice(hd * GLA_DK, (hd + 1) * GLA_DK)
    hv = lambda hd: slice(hd * GLA_DV, (hd + 1) * GLA_DV)

    @pl.when(t == 0)
    def _():
        st[...] = jnp.zeros_like(st)

    lane = lax.broadcasted_iota(jnp.int32, (ng, LANES), 1)
    grp = lax.broadcasted_iota(jnp.int32, (ng, LANES), 0)
    rel = lane - SUB * (grp % nsub)
    rr = lax.broadcasted_iota(jnp.int32, (ch, LANES), 0) // SUB
    cc = lax.broadcasted_iota(jnp.int32, (ch, LANES), 1)
    off_mask = jnp.where(cc >= 8 * rr * (rr - 1), jnp.where(cc < 8 * rr * (rr + 1), 1, 0), 0) > 0

    x = jnp.dot(lr_ref[...].astype(BF16), wg_ref[...], preferred_element_type=F32) + bg_ref[...]
    g2 = _log_sigmoid(x) * (LOG2E / GLA_TAU)
    g_hi = g2.astype(BF16)
    r1 = g2 - g_hi.astype(F32)
    g_mid = r1.astype(BF16)
    g_lo = (r1 - g_mid.astype(F32)).astype(BF16)
    cum = jnp.dot(lc_ref[...], jnp.concatenate([g_hi, g_mid, g_lo], axis=1), preferred_element_type=F32)
    cum = cum[:, :kw] + cum[:, kw:2 * kw] + cum[:, 2 * kw:]
    b, bl, bsb = cum[:tb], cum[tb:2 * tb], cum[2 * tb:]
    q = q_ref[...] * (GLA_DK ** -0.5)
    k = k_ref[...]
    for hd in range(GLA_H):
        qs[hd] = q[:, hk(hd)]
        ks_[hd] = k[:, hk(hd)]
        bs_[hd] = b[:, hk(hd)]
    qt[...] = (q * jnp.exp2(b)).astype(BF16)
    kt[...] = (k * jnp.exp2(bl - b)).astype(BF16)
    eb[...] = jnp.exp2(bl)
    qh[...] = (q * jnp.exp2(b - bsb)).astype(BF16)

    def rows_i(ref, i):
        return jnp.concatenate([ref[hd, pl.ds(i, ng, stride=SUB), :] for hd in range(GLA_H)], axis=1)

    qd = [rows_i(qs, i) for i in range(SUB)]
    kd = [rows_i(ks_, i) for i in range(SUB)]
    bd = [rows_i(bs_, i) for i in range(SUB)]
    for i in range(SUB):
        a_i = [jnp.zeros((ng, LANES), F32) for _ in range(GLA_H)]
        for j in range(i + 1):
            tt = qd[i] * kd[j] * jnp.exp2(bd[i] - bd[j])
            for hd in range(GLA_H):
                col = jnp.sum(tt[:, hk(hd)], axis=-1, keepdims=True)
                a_i[hd] = jnp.where(rel == j, col, a_i[hd])
        for hd in range(GLA_H):
            am[hd, pl.ds(i, ng, stride=SUB), :] = a_i[hd]

    kalls = []
    for cidx in range(nc):
        r0 = cidx * ch
        kc, bc = k[r0:r0 + ch], b[r0:r0 + ch]
        parts = []
        for sub in range(1, nsub):
            s_i = bsb[r0 + sub * SUB:r0 + sub * SUB + 1, :]
            parts.append((kc[:sub * SUB] * jnp.exp2(s_i - bc[:sub * SUB])).astype(BF16))
        parts.append(jnp.zeros((LANES - sum(p.shape[0] for p in parts), kw), BF16))
        kalls.append(jnp.concatenate(parts, axis=0))
    scs = {}
    for cidx in range(nc):
        for hd in range(GLA_H):
            scs[cidx, hd] = lax.dot_general(qh[cidx * ch:(cidx + 1) * ch, hk(hd)], kalls[cidx][:, hk(hd)],
                                            (((1,), (1,)), ((), ())), preferred_element_type=F32)
    for cidx in range(nc):
        rows = slice(cidx * ch, (cidx + 1) * ch)
        for hd in range(GLA_H):
            lhs = jnp.concatenate([jnp.where(off_mask, scs[cidx, hd], 0.0).astype(BF16),
                                   am[hd, rows, :].astype(BF16)], axis=1)
            vc = v_ref[rows, hv(hd)].astype(BF16)
            rhs = jnp.concatenate(
                [vc[:s * SUB] for s in range(1, nsub)]
                + [jnp.zeros((LANES - SUB * nsub * (nsub - 1) // 2, GLA_DV), BF16), vc,
                   jnp.zeros((LANES - ch, GLA_DV), BF16)], axis=0)
            ob[hd, rows, :] = jnp.dot(lhs, rhs, preferred_element_type=F32)

    for cidx in range(nc):
        rows = slice(cidx * ch, (cidx + 1) * ch)
        for hd in range(GLA_H):
            vc = v_ref[rows, hv(hd)].astype(BF16)
            sb = st[hd]
            ob[hd, rows, :] += lax.dot_general(qt[rows, hk(hd)], sb.astype(BF16), (((1,), (1,)), ((), ())),
                                               preferred_element_type=F32)
            upd = lax.dot_general(vc, kt[rows, hk(hd)], (((0,), (0,)), ((), ())), preferred_element_type=F32)
            st[hd] = sb * eb[cidx * ch:cidx * ch + 1, hk(hd)] + upd

    for hd in range(GLA_H):
        vs = slice(hd * GLA_DV, (hd + 1) * GLA_DV)
        o = ob[hd]
        on = o * lax.rsqrt(jnp.mean(o * o, axis=-1, keepdims=True) + EPS) * gn_ref[:, vs]
        gg = g_ref[:, vs]
        o_ref[:, vs] = (on * (gg * _sigmoid(gg))).astype(o_ref.dtype)

    @pl.when(t == pl.num_programs(1) - 1)
    def _():
        for hd in range(GLA_H):
            so_ref[hd] = st[hd].T


def gla_prompt(h, wg, bg, gn, batch, seq, tb):
    nt = seq // tb
    kw = GLA_H * GLA_DK
    vw = GLA_H * GLA_DV
    row = lambda b, t: b * nt + t
    r = np.arange(tb)[:, None]
    c = np.arange(tb)[None, :]
    same = (r // GLA_CHUNK) == (c // GLA_CHUNK)
    lcat = jnp.asarray(np.concatenate([same & (c <= r), same & (c >= 0), same & (c < (r // SUB) * SUB)], axis=0),
                       BF16)
    return pl.pallas_call(
        _gla_prompt_kernel,
        grid=(batch, nt),
        in_specs=[
            pl.BlockSpec((tb, kw), lambda b, t: (row(b, t), H_GQ // kw)),
            pl.BlockSpec((tb, kw), lambda b, t: (row(b, t), H_GK // kw)),
            pl.BlockSpec((tb, vw), lambda b, t: (row(b, t), H_GV // vw)),
            pl.BlockSpec((tb, vw), lambda b, t: (row(b, t), H_GG // vw)),
            pl.BlockSpec((tb, LANES), lambda b, t: (row(b, t), H_GLR // LANES)),
            pl.BlockSpec((LANES, kw), lambda b, t: (0, 0)),
            pl.BlockSpec((1, kw), lambda b, t: (0, 0)),
            pl.BlockSpec((1, vw), lambda b, t: (0, 0)),
            pl.BlockSpec((3 * tb, tb), lambda b, t: (0, 0)),
        ],
        out_specs=[
            pl.BlockSpec((tb, vw), lambda b, t: (row(b, t), 0)),
            pl.BlockSpec((None, GLA_H, GLA_DK, GLA_DV), lambda b, t: (b, 0, 0, 0)),
        ],
        out_shape=[
            jax.ShapeDtypeStruct((h.shape[0], vw), BF16),
            jax.ShapeDtypeStruct((batch, GLA_H, GLA_DK, GLA_DV), F32),
        ],
        scratch_shapes=[
            pltpu.VMEM((GLA_H, GLA_DV, GLA_DK), F32),
            pltpu.VMEM((GLA_H, tb, GLA_DK), F32),
            pltpu.VMEM((GLA_H, tb, GLA_DK), F32),
            pltpu.VMEM((GLA_H, tb, GLA_DK), F32),
            pltpu.VMEM((GLA_H, tb, LANES), F32),
            pltpu.VMEM((tb, kw), BF16),
            pltpu.VMEM((tb, kw), BF16),
            pltpu.VMEM((tb, kw), BF16),
            pltpu.VMEM((tb, kw), F32),
            pltpu.VMEM((GLA_H, tb, GLA_DV), F32),
        ],
        compiler_params=_cparams(("arbitrary", "arbitrary")),
        name="gla_prompt",
    )(h, h, h, h, h, wg, bg, gn, lcat)


def _swa_sample_kernel(q_ref, g_ref, k_ref, v_ref, kb_ref, vb_ref, cos_ref, sin_ref, sink_ref,
                       o_ref, ko_ref, vo_ref):
    w = kb_ref.shape[0]
    cos = cos_ref[...]
    sin = sin_ref[...]
    lane1 = lax.broadcasted_iota(jnp.int32, (1, LANES), 1)
    lane = lax.broadcasted_iota(jnp.int32, (16, LANES), 1)
    rowi = lax.broadcasted_iota(jnp.int32, (w, LANES), 0)
    lo_w = lax.broadcasted_iota(jnp.int32, (w, LANES), 1) < 64
    lo = lane < 64

    kk, vv = [], []
    for p in range(2):
        sl = slice(p * LANES, (p + 1) * LANES)
        knew = _rope128(k_ref[:, sl], cos, sin, (lane1 & 32) == 0)
        vnew = v_ref[:, sl]
        kwin = jnp.where(rowi == w - 1, knew, pltpu.roll(kb_ref[:, sl], w - 1, 0))
        vwin = jnp.where(rowi == w - 1, vnew, pltpu.roll(vb_ref[:, sl], w - 1, 0))
        ko_ref[:, sl] = kwin
        vo_ref[:, sl] = vwin
        kk += [x.astype(BF16) for x in _dup_halves(kwin, lo_w)]
        vv += [x.astype(BF16) for x in _dup_halves(vwin, lo_w)]

    q = _rope128(q_ref[...], cos, sin, (lane & 32) == 0) * (SWA_HD ** -0.5)
    qq = jnp.concatenate([jnp.where(lo, q, 0.0), jnp.where(lo, 0.0, q)], axis=0)
    grp = (lax.broadcasted_iota(jnp.int32, (32, LANES), 0) % 16) // 4
    s = jnp.zeros((32, w), F32)
    for kh in range(SWA_KV):
        s = s + lax.dot_general(jnp.where(grp == kh, qq, 0.0).astype(BF16), kk[kh],
                                (((1,), (1,)), ((), ())), preferred_element_type=F32)
    sk = sink_ref[...][:, 0:1]
    m = jnp.maximum(jnp.max(s, axis=-1, keepdims=True), sk)
    p = jnp.exp(s - m)
    den = jnp.sum(p, axis=-1, keepdims=True) + jnp.exp(sk - m)
    o = jnp.zeros((32, LANES), F32)
    for kh in range(SWA_KV):
        o = o + jnp.dot(jnp.where(grp == kh, p, 0.0).astype(BF16), vv[kh], preferred_element_type=F32)
    o = o * (1.0 / den)
    g = g_ref[...]
    o_ref[...] = jnp.where(lo, o[0:16], o[16:32]) * (g * _sigmoid(g))


def swa_sample(q, g, k, v, kbuf, vbuf, cos, sin, sink_rows):
    bd, wb, kvw = kbuf.shape
    return pl.pallas_call(
        _swa_sample_kernel,
        grid=(bd,),
        in_specs=[
            pl.BlockSpec((None, 16, LANES), lambda b: (b, 0, 0)),
            pl.BlockSpec((None, 16, LANES), lambda b: (b, 0, 0)),
            pl.BlockSpec((None, 1, kvw), lambda b: (b, 0, 0)),
            pl.BlockSpec((None, 1, kvw), lambda b: (b, 0, 0)),
            pl.BlockSpec((None, wb, kvw), lambda b: (b, 0, 0)),
            pl.BlockSpec((None, wb, kvw), lambda b: (b, 0, 0)),
            pl.BlockSpec((1, LANES), lambda b: (0, 0)),
            pl.BlockSpec((1, LANES), lambda b: (0, 0)),
            pl.BlockSpec((32, LANES), lambda b: (0, 0)),
        ],
        out_specs=[
            pl.BlockSpec((None, 16, LANES), lambda b: (b, 0, 0)),
            pl.BlockSpec((None, wb, kvw), lambda b: (b, 0, 0)),
            pl.BlockSpec((None, wb, kvw), lambda b: (b, 0, 0)),
        ],
        out_shape=[
            jax.ShapeDtypeStruct((bd, 16, LANES), F32),
            jax.ShapeDtypeStruct((bd, wb, kvw), F32),
            jax.ShapeDtypeStruct((bd, wb, kvw), F32),
        ],
        compiler_params=_cparams(("parallel",)),
        name="swa_sample",
    )(q, g, k, v, kbuf, vbuf, cos, sin, sink_rows)


def _mem_sample_kernel(q_ref, g_ref, k_ref, v_ref, o_ref):
    n = k_ref.shape[0]
    grp = 2 * MEM_H
    q8 = (q_ref[...] * (MEM_HD ** -0.5)).astype(BF16)
    r = lax.dot_general(q8, k_ref[...].astype(BF16), (((1,), (1,)), ((), ())), preferred_element_type=F32)
    row = lax.broadcasted_iota(jnp.int32, (grp, n), 0)
    col = lax.broadcasted_iota(jnp.int32, (grp, n), 1)
    own = (col % grp) == row
    a = jnp.where(own, r, 0.0)
    s = a + pltpu.roll(pltpu.roll(a, MEM_H, 0), n - MEM_H, 1)
    valid = jnp.where(own, jnp.where(row < MEM_H, 1, 0), 0) > 0
    sm = jnp.where(valid, s, -1e30)
    m = jnp.max(sm, axis=-1, keepdims=True)
    p = jnp.where(valid, jnp.exp(sm - m), 0.0)
    den = jnp.sum(p, axis=-1, keepdims=True)
    pn = p * (1.0 / jnp.where(den > 0.0, den, 1.0))
    p8 = pn + pltpu.roll(pltpu.roll(pn, MEM_H, 0), MEM_H, 1)
    o = jnp.dot(p8.astype(BF16), v_ref[...].astype(BF16), preferred_element_type=F32)
    g = g_ref[...]
    o_ref[...] = o * (g * _sigmoid(g))


def _cache_rows(c):
    depth, bd, mlen, nh, hd = c.shape
    return c.reshape(depth, bd, mlen, nh, hd // LANES, LANES).transpose(0, 1, 2, 4, 3, 5).reshape(
        depth, bd, mlen * nh * (hd // LANES), LANES)


def _half_head_rows(x):
    bd = x.shape[0]
    return x.reshape(bd, MEM_H, MEM_HD // LANES, LANES).transpose(0, 2, 1, 3).reshape(bd, -1, LANES)


def mem_sample(q, g, k, v, layer):
    bd = q.shape[0]
    rows = MEM_LEN * MEM_H * (MEM_HD // LANES)
    cache_spec = pl.BlockSpec((None, None, rows, LANES), lambda b: (layer, b, 0, 0))
    vec_spec = pl.BlockSpec((None, 2 * MEM_H, LANES), lambda b: (b, 0, 0))
    o = pl.pallas_call(
        _mem_sample_kernel,
        grid=(bd,),
        in_specs=[vec_spec, vec_spec, cache_spec, cache_spec],
        out_specs=vec_spec,
        out_shape=jax.ShapeDtypeStruct((bd, 2 * MEM_H, LANES), F32),
        compiler_params=_cparams(("parallel",)),
        name="mem_sample",
    )(_half_head_rows(q), _half_head_rows(g), _cache_rows(k), _cache_rows(v))
    return o.reshape(bd, MEM_HD // LANES, MEM_H, LANES).transpose(0, 2, 1, 3).reshape(bd, MEM_H * MEM_HD)


def _gla_sample_kernel(q_ref, k_ref, v_ref, g_ref, lr_ref, wg_ref, bg_ref, gn_ref, s_ref, o_ref, so_ref):
    rowk = lax.broadcasted_iota(jnp.int32, (8, GLA_DK), 0)
    lr = jnp.broadcast_to(lr_ref[...], (8, LANES)).astype(BF16)
    xa = jnp.dot(lr, wg_ref[...], preferred_element_type=F32) + bg_ref[...]
    x = jnp.zeros((8, GLA_DK), F32)
    for hd in range(GLA_H):
        x = jnp.where(rowk == hd, xa[:, hd * GLA_DK:(hd + 1) * GLA_DK], x)
    gl = _log_sigmoid(x) * (1.0 / GLA_TAU)
    eg = jnp.exp(gl)
    q = q_ref[...] * (GLA_DK ** -0.5)
    k = k_ref[...]
    v = v_ref[...]
    qe = q * eg
    eye = lax.broadcasted_iota(jnp.int32, (GLA_DK, GLA_DK), 0) == lax.broadcasted_iota(jnp.int32, (GLA_DK, GLA_DK), 1)

    def col(row):
        return jnp.sum(jnp.where(eye, jnp.broadcast_to(row, (GLA_DK, GLA_DK)), 0.0), axis=-1, keepdims=True)

    o = jnp.sum(q * k, axis=-1, keepdims=True) * v
    for hd in range(GLA_H):
        s0 = s_ref[hd]
        o = o + jnp.dot(jnp.where(rowk == hd, qe, 0.0).astype(BF16), s0.astype(BF16), preferred_element_type=F32)
        so_ref[hd] = s0 * col(eg[hd:hd + 1, :]) + col(k[hd:hd + 1, :]) * v[hd:hd + 1, :]
    on = o * lax.rsqrt(jnp.mean(o * o, axis=-1, keepdims=True) + EPS) * gn_ref[...]
    gg = g_ref[...]
    o_ref[...] = on * (gg * _sigmoid(gg))


def gla_sample(q, k, v, g, lr, wg, bg, gn, state, layer):
    bd = q.shape[0]
    kw = GLA_H * GLA_DK
    return pl.pallas_call(
        _gla_sample_kernel,
        grid=(bd,),
        in_specs=[
            pl.BlockSpec((None, 8, GLA_DK), lambda b: (b, 0, 0)),
            pl.BlockSpec((None, 8, GLA_DK), lambda b: (b, 0, 0)),
            pl.BlockSpec((None, 8, GLA_DV), lambda b: (b, 0, 0)),
            pl.BlockSpec((None, 8, GLA_DV), lambda b: (b, 0, 0)),
            pl.BlockSpec((None, 1, LANES), lambda b: (b, 0, 0)),
            pl.BlockSpec((LANES, kw), lambda b: (0, 0)),
            pl.BlockSpec((1, kw), lambda b: (0, 0)),
            pl.BlockSpec((8, GLA_DV), lambda b: (0, 0)),
            pl.BlockSpec((None, None, GLA_H, GLA_DK, GLA_DV), lambda b: (layer, b, 0, 0, 0)),
        ],
        out_specs=[
            pl.BlockSpec((None, 8, GLA_DV), lambda b: (b, 0, 0)),
            pl.BlockSpec((None, GLA_H, GLA_DK, GLA_DV), lambda b: (b, 0, 0, 0)),
        ],
        out_shape=[
            jax.ShapeDtypeStruct((bd, 8, GLA_DV), F32),
            jax.ShapeDtypeStruct(state.shape[1:], F32),
        ],
        compiler_params=_cparams(("parallel",)),
        name="gla_sample",
    )(q, k, v, g, lr, wg, bg, gn, state)


def _heads8(x, width):
    bd = x.shape[0]
    x = x.reshape(bd, -1, width)
    return jnp.pad(x, ((0, 0), (0, 8 - x.shape[1]), (0, 0)))


def kernel(x_prompt, mem_prompt, x_sample, cache_swa_k, cache_swa_v, state_gla, cache_mem_k, cache_mem_v,
           norm_g, w_in, attn_sinks, gla_w_gate, gla_b_gate, gla_norm_g, mem_norm_g, w_mem_kv, w_out,
           final_norm_g):
    batch, seq, d = x_prompt.shape
    bd = x_sample.shape[0]
    m = batch * seq
    kvw = SWA_KV * SWA_HD
    memw = MEM_H * MEM_HD

    cos_p, sin_p = _rope_tables(jnp.arange(seq))
    cos_s, sin_s = _rope_tables(PAST_LEN + jnp.arange(1))
    perm = np.array([2 * (r % 16) + r // 16 for r in range(32)])
    w_in_t = jnp.transpose(w_in, (0, 2, 1))

    xp = x_prompt.reshape(m, d)
    xs = x_sample.reshape(bd, d)
    memx = mem_prompt.reshape(batch * MEM_LEN, d)

    outs = {k: [] for k in ("kp", "vp", "sp", "mkp", "mvp", "ks", "vs", "ss")}
    for l in range(DEPTH):
        wg = jnp.pad(gla_w_gate[l], ((0, LANES - GLA_RANK), (0, 0))).astype(BF16)
        bg = gla_b_gate[l].reshape(1, -1)
        gn = gla_norm_g[l].reshape(1, -1)

        kv = project(rmsnorm(memx, mem_norm_g[l], BF16, 256), w_mem_kv, l, name="mem_kv")
        xn = rmsnorm(xp, norm_g[l], BF16, 256)
        xns = rmsnorm(xs, norm_g[l], BF16, bd)
        h, hs = project(xn, w_in_t, l, n=H_END, tile_rows=IN_TILE_SRC, a_small=xns, name="in_proj")

        o_swa_p, kb, vb = swa_prompt(h, attn_sinks[l], cos_p, sin_p, batch, seq)
        o_gla_p, sp = gla_prompt(h, wg, bg, gn, batch, seq, 256)
        o_mem_p = mem_attn_prompt(h, kv, batch, seq, 512)
        outs["kp"].append(kb.reshape(batch, WINDOW, SWA_KV, SWA_HD))
        outs["vp"].append(vb.reshape(batch, WINDOW, SWA_KV, SWA_HD))
        outs["sp"].append(sp)
        outs["mkp"].append(kv[:, :memw].reshape(batch, MEM_LEN, MEM_H, MEM_HD))
        outs["mvp"].append(kv[:, memw:].reshape(batch, MEM_LEN, MEM_H, MEM_HD))

        wb = cache_swa_k.shape[2]
        o_swa, kbs, vbs = swa_sample(
            hs[:, H_SQ:H_SQ + 2048].reshape(bd, 16, LANES), hs[:, H_SG:H_SG + 2048].reshape(bd, 16, LANES),
            hs[:, H_SK:H_SK + kvw].reshape(bd, 1, kvw), hs[:, H_SV:H_SV + kvw].reshape(bd, 1, kvw),
            cache_swa_k[l].reshape(bd, wb, kvw), cache_swa_v[l].reshape(bd, wb, kvw),
            cos_s, sin_s, jnp.broadcast_to(attn_sinks[l][perm][:, None], (32, LANES)))
        o_gla, ss = gla_sample(
            _heads8(hs[:, H_GQ:H_GQ + 512], GLA_DK), _heads8(hs[:, H_GK:H_GK + 512], GLA_DK),
            _heads8(hs[:, H_GV:H_GV + 1024], GLA_DV), _heads8(hs[:, H_GG:H_GG + 1024], GLA_DV),
            hs[:, H_GLR:H_GLR + LANES].reshape(bd, 1, LANES), wg, bg,
            jnp.pad(gla_norm_g[l].reshape(GLA_H, GLA_DV), ((0, 4), (0, 0))), state_gla, l)
        o_mem = mem_sample(hs[:, H_MQ:H_MQ + memw], hs[:, H_MG:H_MG + memw], cache_mem_k, cache_mem_v, l)
        mix_s = jnp.concatenate([o_swa.reshape(bd, 2048), o_gla[:, :GLA_H].reshape(bd, 1024), o_mem],
                                axis=1).astype(BF16)
        outs["ks"].append(kbs.reshape(bd, wb, SWA_KV, SWA_HD))
        outs["vs"].append(vbs.reshape(bd, wb, SWA_KV, SWA_HD))
        outs["ss"].append(ss)

        xp, xs = project([o_swa_p, o_gla_p, o_mem_p], w_out, l, a_small=mix_s, res=xp, res_small=xs,
                         name="out_proj")

    y_prompt = rmsnorm(xp, final_norm_g, F32, 256).reshape(batch, seq, d)
    y_sample = rmsnorm(xs, final_norm_g, F32, bd).reshape(bd, 1, d)
    st = lambda k: jnp.stack(outs[k])
    return (y_prompt, y_sample, st("kp"), st("vp"), st("sp"), st("mkp"), st("mvp"), st("ks"), st("vs"), st("ss"))
```

```python
import functools
import math

import jax
import jax.numpy as jnp
import numpy as np
from jax import lax
from jax.experimental import pallas as pl
from jax.experimental.pallas import tpu as pltpu

F32 = jnp.float32
BF16 = jnp.bfloat16

D_MODEL = 4096
DEPTH = 2
EPS = 1e-6
PAST_LEN = 16384
WINDOW = 128
ROPE_THETA = 10000.0
SWA_HD = 64
SWA_HQ = 32
SWA_KV = 4
GLA_H = 4
GLA_DK = 128
GLA_DV = 256
GLA_RANK = 16
GLA_TAU = 16.0
MEM_LEN = 256
MEM_H = 4
MEM_HD = 256
LANES = 128
GLA_CHUNK = 64
SUB = 16
SAMPLE_BLOCK = 4
SWA_STEP_BLOCKS = 2
LOG2E = math.log2(math.e)

TM = 2048
TN = 512
H_SQ, H_SG, H_GV, H_GG, H_MQ, H_MG, H_GQ, H_GK, H_SK, H_SV, H_GLR, H_END = (
    0, 2048, 4096, 5120, 6144, 7168, 8192, 8704, 9216, 9472, 9728, 9856)
_IN_PIECES = ((0, 2048), (2560, 2048), (5632, 1024), (6672, 1024), (7696, 1024), (8720, 1024), (4608, 512),
              (5120, 512), (2048, 512), (6656, 128))
IN_TILE_SRC = tuple(src + TN * t for src, width in _IN_PIECES for t in range(-(-width // TN)))
VMEM_LIMIT = 58 * 1024 * 1024


def _cparams(sem):
    return pltpu.CompilerParams(dimension_semantics=sem, vmem_limit_bytes=VMEM_LIMIT)


def _sigmoid(x):
    return 1.0 / (1.0 + jnp.exp(-x))


def _rmsnorm_kernel(x_ref, g_ref, o_ref):
    x = x_ref[...]
    ms = jnp.mean(x * x, axis=-1, keepdims=True)
    o_ref[...] = (x * lax.rsqrt(ms + EPS) * g_ref[...]).astype(o_ref.dtype)


def rmsnorm(x, g, out_dtype, tm):
    m, d = x.shape
    return pl.pallas_call(
        _rmsnorm_kernel,
        grid=(m // tm,),
        in_specs=[pl.BlockSpec((tm, d), lambda i: (i, 0)), pl.BlockSpec((1, d), lambda i: (0, 0))],
        out_specs=pl.BlockSpec((tm, d), lambda i: (i, 0)),
        out_shape=jax.ShapeDtypeStruct((m, d), out_dtype),
        compiler_params=_cparams(("parallel",)),
        name="rmsnorm",
    )(x, g.reshape(1, d))


def _proj_kernel(*refs, n_parts, has_small, has_res, tn, last_valid, w_rows):
    refs = list(refs)
    if w_rows:
        refs.pop(0)
    a_refs = [refs.pop(0) for _ in range(n_parts)]
    as_ref = refs.pop(0) if has_small else None
    w_ref = refs.pop(0)
    r_ref = refs.pop(0) if has_res else None
    rs_ref = refs.pop(0) if (has_res and has_small) else None
    o_ref = refs.pop(0)
    os_ref = refs.pop(0) if has_small else None
    i = pl.program_id(0)
    j = pl.program_id(1)
    nj = pl.num_programs(1)

    def compute(width):
        if w_rows:
            w = w_ref[:width, :].astype(BF16)
            mm = lambda a: lax.dot_general(a, w, (((1,), (1,)), ((), ())), preferred_element_type=F32)
            acc = mm(a_refs[0][...])
        else:
            w = w_ref[:, :width].astype(BF16)
            mm = lambda a: jnp.dot(a, w, preferred_element_type=F32)
            acc, k0 = None, 0
            for a_ref in a_refs:
                kp = a_ref.shape[1]
                part = jnp.dot(a_ref[...], w[k0:k0 + kp], preferred_element_type=F32)
                acc = part if acc is None else acc + part
                k0 += kp
        if has_res:
            acc = acc + r_ref[:, :width]
        o_ref[:, :width] = acc
        if has_small:
            @pl.when(i == 0)
            def _():
                accs = mm(as_ref[...])
                if has_res:
                    accs = accs + rs_ref[:, :width]
                os_ref[:, :width] = accs

    if last_valid == tn:
        compute(tn)
    else:
        pl.when(j < nj - 1)(lambda: compute(tn))
        pl.when(j == nj - 1)(lambda: compute(last_valid))


def project(a, w3, layer, n=None, tile_rows=None, a_small=None, res=None, res_small=None, name="proj"):
    a_parts = list(a) if isinstance(a, (tuple, list)) else [a]
    m = a_parts[0].shape[0]
    k = sum(p.shape[1] for p in a_parts)
    w_rows = tile_rows is not None
    if n is None:
        n = w3.shape[2]
    tm = min(TM, m)
    nj = pl.cdiv(n, TN)
    last_valid = n - (nj - 1) * TN
    has_small = a_small is not None
    has_res = res is not None
    small_idx = lambda i, j, *_: (0, jnp.where(i == 0, j, nj - 1))
    in_specs = [pl.BlockSpec((tm, p.shape[1]), lambda i, j, *_: (i, 0), pipeline_mode=pl.Buffered(1))
                for p in a_parts]
    args = list(a_parts)
    if has_small:
        ms = a_small.shape[0]
        in_specs.append(pl.BlockSpec((ms, k), lambda i, j, *_: (0, 0)))
        args.append(a_small)
    if w_rows:
        in_specs.append(pl.BlockSpec((None, pl.Element(TN), pl.Element(k)),
                                     lambda i, j, offs: (layer, pl.multiple_of(offs[j], 16), 0)))
    else:
        in_specs.append(pl.BlockSpec((None, k, TN), lambda i, j: (layer, 0, j)))
    args.append(w3)
    if has_res:
        in_specs.append(pl.BlockSpec((tm, TN), lambda i, j, *_: (i, j)))
        args.append(res)
        if has_small:
            in_specs.append(pl.BlockSpec((ms, TN), small_idx))
            args.append(res_small)
    out_specs = [pl.BlockSpec((tm, TN), lambda i, j, *_: (i, j))]
    out_shape = [jax.ShapeDtypeStruct((m, n), F32)]
    if has_small:
        out_specs.append(pl.BlockSpec((ms, TN), small_idx))
        out_shape.append(jax.ShapeDtypeStruct((ms, n), F32))
    kern = functools.partial(_proj_kernel, n_parts=len(a_parts), has_small=has_small, has_res=has_res, tn=TN,
                             last_valid=last_valid, w_rows=w_rows)
    if w_rows:
        grid_spec = pltpu.PrefetchScalarGridSpec(num_scalar_prefetch=1, grid=(m // tm, nj), in_specs=in_specs,
                                                 out_specs=out_specs)
        args = [jnp.asarray(tile_rows, jnp.int32)] + args
    else:
        grid_spec = pl.GridSpec(grid=(m // tm, nj), in_specs=in_specs, out_specs=out_specs)
    outs = pl.pallas_call(
        kern,
        grid_spec=grid_spec,
        out_shape=out_shape,
        compiler_params=_cparams(("arbitrary", "arbitrary")),
        name=name,
    )(*args)
    return outs if has_small else outs[0]


def _rope_tables(pos):
    half = SWA_HD // 2
    inv = ROPE_THETA ** (-jnp.arange(half, dtype=F32) / half)
    ang = pos.astype(F32)[:, None] * inv[None, :]
    cos, sin = jnp.cos(ang), jnp.sin(ang)
    return jnp.concatenate([cos, cos, cos, cos], axis=-1), jnp.concatenate([-sin, sin, -sin, sin], axis=-1)


def _rope128(x, cos, sin, first_half):
    partner = jnp.where(first_half, pltpu.roll(x, LANES - 32, 1), pltpu.roll(x, 32, 1))
    return x * cos + partner * sin


def _dup_halves(x, lo):
    r = pltpu.roll(x, 64, 1)
    return jnp.where(lo, x, r), jnp.where(lo, r, x)


def _swa_prompt_kernel(sink_ref, q_ref, g_ref, k_ref, v_ref, cos_ref, sin_ref,
                       o_ref, ko_ref, vo_ref, kprev, vprev):
    step = pl.program_id(1)

    @pl.when(step == 0)
    def _():
        kprev[...] = jnp.zeros_like(kprev)
        vprev[...] = jnp.zeros_like(vprev)

    w = WINDOW
    for sub in range(q_ref.shape[0] // w):
        rows = slice(sub * w, (sub + 1) * w)
        has_prev = None if sub > 0 else step > 0
        _swa_block(sink_ref, q_ref.at[rows], g_ref.at[rows], k_ref.at[rows], v_ref.at[rows], cos_ref.at[rows],
                   sin_ref.at[rows], o_ref.at[rows], ko_ref, vo_ref, kprev, vprev, has_prev)


def _swa_block(sink_ref, q_ref, g_ref, k_ref, v_ref, cos_ref, sin_ref, o_ref, ko_ref, vo_ref, kprev, vprev,
               has_prev):
    w = WINDOW
    cos = cos_ref[...]
    sin = sin_ref[...]
    lane = lax.broadcasted_iota(jnp.int32, (w, LANES), 1)
    rowi = lax.broadcasted_iota(jnp.int32, (w, LANES), 0)
    first_half = (lane & 32) == 0
    lo = lane < 64
    own = lane <= rowi
    ones = jnp.ones((2 * w, LANES), BF16)

    kall, vall = [], []
    kcur, vcur = [], []
    for p in range(2):
        kp = _rope128(k_ref[:, p * LANES:(p + 1) * LANES], cos, sin, first_half)
        vp = v_ref[:, p * LANES:(p + 1) * LANES]
        ko_ref[:, p * LANES:(p + 1) * LANES] = kp
        vo_ref[:, p * LANES:(p + 1) * LANES] = vp
        kcur += [x.astype(BF16) for x in _dup_halves(kp, lo)]
        vcur += [x.astype(BF16) for x in _dup_halves(vp, lo)]
    for kh in range(SWA_KV):
        kall.append(jnp.concatenate([kprev[kh], kcur[kh]], axis=0))
        vv = jnp.concatenate([vprev[kh], vcur[kh]], axis=0)
        vall.append(jnp.concatenate([vv, ones], axis=1))
    for kh in range(SWA_KV):
        kprev[kh] = kcur[kh]
        vprev[kh] = vcur[kh]

    scale = (SWA_HD ** -0.5) * LOG2E
    slabs = []
    for c in range(SWA_HQ // 2):
        qc = _rope128(q_ref[:, c * LANES:(c + 1) * LANES], cos, sin, first_half) * scale
        slabs.append(jnp.where(lo, qc, 0.0).astype(BF16))
        slabs.append(jnp.where(lo, 0.0, qc).astype(BF16))
    scores = [lax.dot_general(jnp.concatenate(slabs[8 * kh:8 * kh + 8], axis=0), kall[kh],
                              (((1,), (1,)), ((), ())), preferred_element_type=F32)
              for kh in range(SWA_KV)]
    keep = None if has_prev is None else jnp.where(own, 1, jnp.where(has_prev, 1, 0)) > 0
    folded = []
    for hq in range(SWA_HQ):
        sh = scores[hq // 8][(hq % 8) * w:(hq % 8 + 1) * w]
        f = jnp.where(own, sh[:, w:], sh[:, :w])
        folded.append(f if keep is None else jnp.where(keep, f, -1e30))
    maxes = [jnp.max(f, axis=-1, keepdims=True) for f in folded]
    probs = [jnp.exp2(f - m) for f, m in zip(folded, maxes)]
    sinks2 = [jnp.exp2(sink_ref[hq] * LOG2E - maxes[hq]) for hq in range(SWA_HQ)]
    ps = [jnp.concatenate([jnp.where(own, 0.0, p), jnp.where(own, p, 0.0)], axis=1).astype(BF16) for p in probs]
    outs = [jnp.dot(jnp.concatenate(ps[8 * kh:8 * kh + 8], axis=0), vall[kh], preferred_element_type=F32)
            for kh in range(SWA_KV)]
    normed = []
    for hq in range(SWA_HQ):
        oh = outs[hq // 8][(hq % 8) * w:(hq % 8 + 1) * w]
        normed.append(oh[:, :LANES] * (1.0 / (oh[:, LANES:] + sinks2[hq])))
    for c in range(SWA_HQ // 2):
        gt = g_ref[:, c * LANES:(c + 1) * LANES]
        o_ref[:, c * LANES:(c + 1) * LANES] = (
            jnp.where(lo, normed[2 * c], normed[2 * c + 1]) * (gt * _sigmoid(gt))).astype(o_ref.dtype)


def swa_prompt(h, sinks, cos, sin, batch, seq):
    m = h.shape[0]
    tq = SWA_STEP_BLOCKS * WINDOW
    nb = seq // tq
    kvw = SWA_KV * SWA_HD
    row = lambda b, n: b * nb + n
    return pl.pallas_call(
        _swa_prompt_kernel,
        grid=(batch, nb),
        in_specs=[
            pl.BlockSpec(memory_space=pltpu.SMEM),
            pl.BlockSpec((tq, 2048), lambda b, n: (row(b, n), H_SQ // 2048)),
            pl.BlockSpec((tq, 2048), lambda b, n: (row(b, n), H_SG // 2048)),
            pl.BlockSpec((tq, kvw), lambda b, n: (row(b, n), H_SK // kvw)),
            pl.BlockSpec((tq, kvw), lambda b, n: (row(b, n), H_SV // kvw)),
            pl.BlockSpec((tq, LANES), lambda b, n: (n, 0)),
            pl.BlockSpec((tq, LANES), lambda b, n: (n, 0)),
        ],
        out_specs=[
            pl.BlockSpec((tq, 2048), lambda b, n: (row(b, n), 0)),
            pl.BlockSpec((None, WINDOW, kvw), lambda b, n: (b, 0, 0)),
            pl.BlockSpec((None, WINDOW, kvw), lambda b, n: (b, 0, 0)),
        ],
        out_shape=[
            jax.ShapeDtypeStruct((m, SWA_HQ * SWA_HD), BF16),
            jax.ShapeDtypeStruct((batch, WINDOW, kvw), F32),
            jax.ShapeDtypeStruct((batch, WINDOW, kvw), F32),
        ],
        scratch_shapes=[pltpu.VMEM((SWA_KV, WINDOW, LANES), BF16), pltpu.VMEM((SWA_KV, WINDOW, LANES), BF16)],
        compiler_params=_cparams(("arbitrary", "arbitrary")),
        name="swa_prompt",
    )(sinks, h, h, h, h, cos, sin)


def _mem_prompt_kernel(q_ref, g_ref, k_ref, v_ref, o_ref):
    scale = MEM_HD ** -0.5
    for h in range(MEM_H):
        sl = slice(h * MEM_HD, (h + 1) * MEM_HD)
        q = (q_ref[:, sl] * scale).astype(BF16)
        k = k_ref[:, sl].astype(BF16)
        v = v_ref[:, sl].astype(BF16)
        s = lax.dot_general(q, k, (((1,), (1,)), ((), ())), preferred_element_type=F32)
        m = jnp.max(s, axis=-1, keepdims=True)
        p = jnp.exp(s - m)
        den = jnp.sum(p, axis=-1, keepdims=True)
        o = jnp.dot(p.astype(BF16), v, preferred_element_type=F32) * (1.0 / den)
        g = g_ref[:, sl]
        o_ref[:, sl] = (o * (g * _sigmoid(g))).astype(o_ref.dtype)


def mem_attn_prompt(h, kv, batch, seq, tq):
    nt = seq // tq
    w = MEM_H * MEM_HD
    row = lambda b, t: b * nt + t
    return pl.pallas_call(
        _mem_prompt_kernel,
        grid=(batch, nt),
        in_specs=[
            pl.BlockSpec((tq, w), lambda b, t: (row(b, t), H_MQ // w)),
            pl.BlockSpec((tq, w), lambda b, t: (row(b, t), H_MG // w)),
            pl.BlockSpec((MEM_LEN, w), lambda b, t: (b, 0)),
            pl.BlockSpec((MEM_LEN, w), lambda b, t: (b, 1)),
        ],
        out_specs=pl.BlockSpec((tq, w), lambda b, t: (row(b, t), 0)),
        out_shape=jax.ShapeDtypeStruct((h.shape[0], w), BF16),
        compiler_params=_cparams(("parallel", "parallel")),
        name="mem_prompt",
    )(h, h, kv, kv)


def _log_sigmoid(x):
    return jnp.minimum(x, 0.0) - jnp.log1p(jnp.exp(-jnp.abs(x)))


def _gla_prompt_kernel(q_ref, k_ref, v_ref, g_ref, lr_ref, wg_ref, bg_ref, gn_ref, lc_ref,
                       o_ref, so_ref, st, qs, ks_, bs_, am, qt, kt, qh, eb, ob):
    t = pl.program_id(1)
    tb = q_ref.shape[0]
    ch = GLA_CHUNK
    nc = tb // ch
    ng = tb // SUB
    nsub = ch // SUB
    kw = GLA_H * GLA_DK
    hk = lambda hd: slice(hd * GLA_DK, (hd + 1) * GLA_DK)
    hv = lambda hd: slice(hd * GLA_DV, (hd + 1) * GLA_DV)

    @pl.when(t == 0)
    def _():
        st[...] = jnp.zeros_like(st)

    lane = lax.broadcasted_iota(jnp.int32, (ng, LANES), 1)
    grp = lax.broadcasted_iota(jnp.int32, (ng, LANES), 0)
    rel = lane - SUB * (grp % nsub)
    rr = lax.broadcasted_iota(jnp.int32, (ch, LANES), 0) // SUB
    cc = lax.broadcasted_iota(jnp.int32, (ch, LANES), 1)
    off_mask = jnp.where(cc >= 8 * rr * (rr - 1), jnp.where(cc < 8 * rr * (rr + 1), 1, 0), 0) > 0

    x = jnp.dot(lr_ref[...].astype(BF16), wg_ref[...], preferred_element_type=F32) + bg_ref[...]
    g2 = _log_sigmoid(x) * (LOG2E / GLA_TAU)
    g_hi = g2.astype(BF16)
    r1 = g2 - g_hi.astype(F32)
    g_mid = r1.astype(BF16)
    g_lo = (r1 - g_mid.astype(F32)).astype(BF16)
    cum = jnp.dot(lc_ref[...], jnp.concatenate([g_hi, g_mid, g_lo], axis=1), preferred_element_type=F32)
    cum = cum[:, :kw] + cum[:, kw:2 * kw] + cum[:, 2 * kw:]
    b, bl, bsb = cum[:tb], cum[tb:2 * tb], cum[2 * tb:]
    q = q_ref[...] * (GLA_DK ** -0.5)
    k = k_ref[...]
    for hd in range(GLA_H):
        qs[hd] = q[:, hk(hd)]
        ks_[hd] = k[:, hk(hd)]
        bs_[hd] = b[:, hk(hd)]
    qt[...] = (q * jnp.exp2(b)).astype(BF16)
    kt[...] = (k * jnp.exp2(bl - b)).astype(BF16)
    eb[...] = jnp.exp2(bl)
    qh[...] = (q * jnp.exp2(b - bsb)).astype(BF16)

    def rows_i(ref, i):
        return jnp.concatenate([ref[hd, pl.ds(i, ng, stride=SUB), :] for hd in range(GLA_H)], axis=1)

    qd = [rows_i(qs, i) for i in range(SUB)]
    kd = [rows_i(ks_, i) for i in range(SUB)]
    bd = [rows_i(bs_, i) for i in range(SUB)]
    for i in range(SUB):
        a_i = [jnp.zeros((ng, LANES), F32) for _ in range(GLA_H)]
        for j in range(i + 1):
            tt = qd[i] * kd[j] * jnp.exp2(bd[i] - bd[j])
            for hd in range(GLA_H):
                col = jnp.sum(tt[:, hk(hd)], axis=-1, keepdims=True)
                a_i[hd] = jnp.where(rel == j, col, a_i[hd])
        for hd in range(GLA_H):
            am[hd, pl.ds(i, ng, stride=SUB), :] = a_i[hd]

    kalls = []
    for cidx in range(nc):
        r0 = cidx * ch
        kc, bc = k[r0:r0 + ch], b[r0:r0 + ch]
        parts = []
        for sub in range(1, nsub):
            s_i = bsb[r0 + sub * SUB:r0 + sub * SUB + 1, :]
            parts.append((kc[:sub * SUB] * jnp.exp2(s_i - bc[:sub * SUB])).astype(BF16))
        parts.append(jnp.zeros((LANES - sum(p.shape[0] for p in parts), kw), BF16))
        kalls.append(jnp.concatenate(parts, axis=0))
    scs = {}
    for cidx in range(nc):
        for hd in range(GLA_H):
            scs[cidx, hd] = lax.dot_general(qh[cidx * ch:(cidx + 1) * ch, hk(hd)], kalls[cidx][:, hk(hd)],
                                            (((1,), (1,)), ((), ())), preferred_element_type=F32)
    for cidx in range(nc):
        rows = slice(cidx * ch, (cidx + 1) * ch)
        for hd in range(GLA_H):
            lhs = jnp.concatenate([jnp.where(off_mask, scs[cidx, hd], 0.0).astype(BF16),
                                   am[hd, rows, :].astype(BF16)], axis=1)
            vc = v_ref[rows, hv(hd)].astype(BF16)
            rhs = jnp.concatenate(
                [vc[:s * SUB] for s in range(1, nsub)]
                + [jnp.zeros((LANES - SUB * nsub * (nsub - 1) // 2, GLA_DV), BF16), vc,
                   jnp.zeros((LANES - ch, GLA_DV), BF16)], axis=0)
            ob[hd, rows, :] = jnp.dot(lhs, rhs, preferred_element_type=F32)

    for cidx in range(nc):
        rows = slice(cidx * ch, (cidx + 1) * ch)
        for hd in range(GLA_H):
            vc = v_ref[rows, hv(hd)].astype(BF16)
            sb = st[hd]
            ob[hd, rows, :] += lax.dot_general(qt[rows, hk(hd)], sb.astype(BF16), (((1,), (1,)), ((), ())),
                                               preferred_element_type=F32)
            upd = lax.dot_general(vc, kt[rows, hk(hd)], (((0,), (0,)), ((), ())), preferred_element_type=F32)
            st[hd] = sb * eb[cidx * ch:cidx * ch + 1, hk(hd)] + upd

    for hd in range(GLA_H):
        vs = slice(hd * GLA_DV, (hd + 1) * GLA_DV)
        o = ob[hd]
        on = o * lax.rsqrt(jnp.mean(o * o, axis=-1, keepdims=True) + EPS) * gn_ref[:, vs]
        gg = g_ref[:, vs]
        o_ref[:, vs] = (on * (gg * _sigmoid(gg))).astype(o_ref.dtype)

    @pl.when(t == pl.num_programs(1) - 1)
    def _():
        for hd in range(GLA_H):
            so_ref[hd] = st[hd].T


def gla_prompt(h, wg, bg, gn, batch, seq, tb):
    nt = seq // tb
    kw = GLA_H * GLA_DK
    vw = GLA_H * GLA_DV
    row = lambda b, t: b * nt + t
    r = np.arange(tb)[:, None]
    c = np.arange(tb)[None, :]
    same = (r // GLA_CHUNK) == (c // GLA_CHUNK)
    lcat = jnp.asarray(np.concatenate([same & (c <= r), same & (c >= 0), same & (c < (r // SUB) * SUB)], axis=0),
                       BF16)
    return pl.pallas_call(
        _gla_prompt_kernel,
        grid=(batch, nt),
        in_specs=[
            pl.BlockSpec((tb, kw), lambda b, t: (row(b, t), H_GQ // kw)),
            pl.BlockSpec((tb, kw), lambda b, t: (row(b, t), H_GK // kw)),
            pl.BlockSpec((tb, vw), lambda b, t: (row(b, t), H_GV // vw)),
            pl.BlockSpec((tb, vw), lambda b, t: (row(b, t), H_GG // vw)),
            pl.BlockSpec((tb, LANES), lambda b, t: (row(b, t), H_GLR // LANES)),
            pl.BlockSpec((LANES, kw), lambda b, t: (0, 0)),
            pl.BlockSpec((1, kw), lambda b, t: (0, 0)),
            pl.BlockSpec((1, vw), lambda b, t: (0, 0)),
            pl.BlockSpec((3 * tb, tb), lambda b, t: (0, 0)),
        ],
        out_specs=[
            pl.BlockSpec((tb, vw), lambda b, t: (row(b, t), 0)),
            pl.BlockSpec((None, GLA_H, GLA_DK, GLA_DV), lambda b, t: (b, 0, 0, 0)),
        ],
        out_shape=[
            jax.ShapeDtypeStruct((h.shape[0], vw), BF16),
            jax.ShapeDtypeStruct((batch, GLA_H, GLA_DK, GLA_DV), F32),
        ],
        scratch_shapes=[
            pltpu.VMEM((GLA_H, GLA_DV, GLA_DK), F32),
            pltpu.VMEM((GLA_H, tb, GLA_DK), F32),
            pltpu.VMEM((GLA_H, tb, GLA_DK), F32),
            pltpu.VMEM((GLA_H, tb, GLA_DK), F32),
            pltpu.VMEM((GLA_H, tb, LANES), F32),
            pltpu.VMEM((tb, kw), BF16),
            pltpu.VMEM((tb, kw), BF16),
            pltpu.VMEM((tb, kw), BF16),
            pltpu.VMEM((tb, kw), F32),
            pltpu.VMEM((GLA_H, tb, GLA_DV), F32),
        ],
        compiler_params=_cparams(("arbitrary", "arbitrary")),
        name="gla_prompt",
    )(h, h, h, h, h, wg, bg, gn, lcat)


def _per_sequence(body):
    def kern(*refs, n_shared):
        shared = refs[:n_shared]
        for s in range(refs[n_shared].shape[0]):
            body(*shared, *[r.at[s] for r in refs[n_shared:]])
    return kern


def _swa_sample_kernel(cos_ref, sin_ref, sink_ref, q_ref, g_ref, k_ref, v_ref, kb_ref, vb_ref,
                       o_ref, ko_ref, vo_ref):
    w = kb_ref.shape[0]
    cos = cos_ref[...]
    sin = sin_ref[...]
    lane1 = lax.broadcasted_iota(jnp.int32, (1, LANES), 1)
    lane = lax.broadcasted_iota(jnp.int32, (16, LANES), 1)
    rowi = lax.broadcasted_iota(jnp.int32, (w, LANES), 0)
    lo_w = lax.broadcasted_iota(jnp.int32, (w, LANES), 1) < 64
    lo = lane < 64

    kk, vv = [], []
    for p in range(2):
        sl = slice(p * LANES, (p + 1) * LANES)
        knew = _rope128(k_ref[:, sl], cos, sin, (lane1 & 32) == 0)
        vnew = v_ref[:, sl]
        kwin = jnp.where(rowi == w - 1, knew, pltpu.roll(kb_ref[:, sl], w - 1, 0))
        vwin = jnp.where(rowi == w - 1, vnew, pltpu.roll(vb_ref[:, sl], w - 1, 0))
        ko_ref[:, sl] = kwin
        vo_ref[:, sl] = vwin
        kk += [x.astype(BF16) for x in _dup_halves(kwin, lo_w)]
        vv += [x.astype(BF16) for x in _dup_halves(vwin, lo_w)]

    q = _rope128(q_ref[...], cos, sin, (lane & 32) == 0) * (SWA_HD ** -0.5)
    qq = jnp.concatenate([jnp.where(lo, q, 0.0), jnp.where(lo, 0.0, q)], axis=0)
    grp = (lax.broadcasted_iota(jnp.int32, (32, LANES), 0) % 16) // 4
    s = jnp.zeros((32, w), F32)
    for kh in range(SWA_KV):
        s = s + lax.dot_general(jnp.where(grp == kh, qq, 0.0).astype(BF16), kk[kh],
                                (((1,), (1,)), ((), ())), preferred_element_type=F32)
    sk = sink_ref[...][:, 0:1]
    m = jnp.maximum(jnp.max(s, axis=-1, keepdims=True), sk)
    p = jnp.exp(s - m)
    den = jnp.sum(p, axis=-1, keepdims=True) + jnp.exp(sk - m)
    o = jnp.zeros((32, LANES), F32)
    for kh in range(SWA_KV):
        o = o + jnp.dot(jnp.where(grp == kh, p, 0.0).astype(BF16), vv[kh], preferred_element_type=F32)
    o = o * (1.0 / den)
    g = g_ref[...]
    o_ref[...] = jnp.where(lo, o[0:16], o[16:32]) * (g * _sigmoid(g))


def swa_sample(q, g, k, v, kbuf, vbuf, cos, sin, sink_rows):
    bd, wb, kvw = kbuf.shape
    sb = SAMPLE_BLOCK
    seq3 = lambda r, c: pl.BlockSpec((sb, r, c), lambda b: (b, 0, 0))
    return pl.pallas_call(
        functools.partial(_per_sequence(_swa_sample_kernel), n_shared=3),
        grid=(bd // sb,),
        in_specs=[
            pl.BlockSpec((1, LANES), lambda b: (0, 0)),
            pl.BlockSpec((1, LANES), lambda b: (0, 0)),
            pl.BlockSpec((32, LANES), lambda b: (0, 0)),
            seq3(16, LANES), seq3(16, LANES), seq3(1, kvw), seq3(1, kvw), seq3(wb, kvw), seq3(wb, kvw),
        ],
        out_specs=[seq3(16, LANES), seq3(wb, kvw), seq3(wb, kvw)],
        out_shape=[
            jax.ShapeDtypeStruct((bd, 16, LANES), F32),
            jax.ShapeDtypeStruct((bd, wb, kvw), F32),
            jax.ShapeDtypeStruct((bd, wb, kvw), F32),
        ],
        compiler_params=_cparams(("parallel",)),
        name="swa_sample",
    )(cos, sin, sink_rows, q, g, k, v, kbuf, vbuf)


def _mem_sample_kernel(q_ref, g_ref, k_ref, v_ref, o_ref):
    n = k_ref.shape[0]
    grp = 2 * MEM_H
    q8 = (q_ref[...] * (MEM_HD ** -0.5)).astype(BF16)
    r = lax.dot_general(q8, k_ref[...].astype(BF16), (((1,), (1,)), ((), ())), preferred_element_type=F32)
    row = lax.broadcasted_iota(jnp.int32, (grp, n), 0)
    col = lax.broadcasted_iota(jnp.int32, (grp, n), 1)
    own = (col % grp) == row
    a = jnp.where(own, r, 0.0)
    s = a + pltpu.roll(pltpu.roll(a, MEM_H, 0), n - MEM_H, 1)
    valid = jnp.where(own, jnp.where(row < MEM_H, 1, 0), 0) > 0
    sm = jnp.where(valid, s, -1e30)
    m = jnp.max(sm, axis=-1, keepdims=True)
    p = jnp.where(valid, jnp.exp(sm - m), 0.0)
    den = jnp.sum(p, axis=-1, keepdims=True)
    pn = p * (1.0 / jnp.where(den > 0.0, den, 1.0))
    p8 = pn + pltpu.roll(pltpu.roll(pn, MEM_H, 0), MEM_H, 1)
    o = jnp.dot(p8.astype(BF16), v_ref[...].astype(BF16), preferred_element_type=F32)
    g = g_ref[...]
    o_ref[...] = o * (g * _sigmoid(g))


def _cache_rows(c):
    depth, bd, mlen, nh, hd = c.shape
    return c.reshape(depth, bd, mlen, nh, hd // LANES, LANES).transpose(0, 1, 2, 4, 3, 5).reshape(
        depth, bd, mlen * nh * (hd // LANES), LANES)


def _half_head_rows(x):
    bd = x.shape[0]
    return x.reshape(bd, MEM_H, MEM_HD // LANES, LANES).transpose(0, 2, 1, 3).reshape(bd, -1, LANES)


def mem_sample(q, g, k, v, layer):
    bd = q.shape[0]
    sb = SAMPLE_BLOCK
    rows = MEM_LEN * MEM_H * (MEM_HD // LANES)
    cache_spec = pl.BlockSpec((None, sb, rows, LANES), lambda b: (layer, b, 0, 0))
    vec_spec = pl.BlockSpec((sb, 2 * MEM_H, LANES), lambda b: (b, 0, 0))
    o = pl.pallas_call(
        functools.partial(_per_sequence(_mem_sample_kernel), n_shared=0),
        grid=(bd // sb,),
        in_specs=[vec_spec, vec_spec, cache_spec, cache_spec],
        out_specs=vec_spec,
        out_shape=jax.ShapeDtypeStruct((bd, 2 * MEM_H, LANES), F32),
        compiler_params=_cparams(("parallel",)),
        name="mem_sample",
    )(_half_head_rows(q), _half_head_rows(g), _cache_rows(k), _cache_rows(v))
    return o.reshape(bd, MEM_HD // LANES, MEM_H, LANES).transpose(0, 2, 1, 3).reshape(bd, MEM_H * MEM_HD)


def _gla_sample_kernel(wg_ref, bg_ref, gn_ref, q_ref, k_ref, v_ref, g_ref, lr_ref, s_ref, o_ref, so_ref):
    rowk = lax.broadcasted_iota(jnp.int32, (8, GLA_DK), 0)
    lr = jnp.broadcast_to(lr_ref[...], (8, LANES)).astype(BF16)
    xa = jnp.dot(lr, wg_ref[...], preferred_element_type=F32) + bg_ref[...]
    x = jnp.zeros((8, GLA_DK), F32)
    for hd in range(GLA_H):
        x = jnp.where(rowk == hd, xa[:, hd * GLA_DK:(hd + 1) * GLA_DK], x)
    gl = _log_sigmoid(x) * (1.0 / GLA_TAU)
    eg = jnp.exp(gl)
    q = q_ref[...] * (GLA_DK ** -0.5)
    k = k_ref[...]
    v = v_ref[...]
    qe = q * eg
    eye = lax.broadcasted_iota(jnp.int32, (GLA_DK, GLA_DK), 0) == lax.broadcasted_iota(jnp.int32, (GLA_DK, GLA_DK), 1)

    def col(row):
        return jnp.sum(jnp.where(eye, jnp.broadcast_to(row, (GLA_DK, GLA_DK)), 0.0), axis=-1, keepdims=True)

    o = jnp.sum(q * k, axis=-1, keepdims=True) * v
    for hd in range(GLA_H):
        s0 = s_ref[hd]
        o = o + jnp.dot(jnp.where(rowk == hd, qe, 0.0).astype(BF16), s0.astype(BF16), preferred_element_type=F32)
        so_ref[hd] = s0 * col(eg[hd:hd + 1, :]) + col(k[hd:hd + 1, :]) * v[hd:hd + 1, :]
    on = o * lax.rsqrt(jnp.mean(o * o, axis=-1, keepdims=True) + EPS) * gn_ref[...]
    gg = g_ref[...]
    o_ref[...] = on * (gg * _sigmoid(gg))


def gla_sample(q, k, v, g, lr, wg, bg, gn, state, layer):
    bd = q.shape[0]
    sb = SAMPLE_BLOCK
    kw = GLA_H * GLA_DK
    seq3 = lambda r, c: pl.BlockSpec((sb, r, c), lambda b: (b, 0, 0))
    return pl.pallas_call(
        functools.partial(_per_sequence(_gla_sample_kernel), n_shared=3),
        grid=(bd // sb,),
        in_specs=[
            pl.BlockSpec((LANES, kw), lambda b: (0, 0)),
            pl.BlockSpec((1, kw), lambda b: (0, 0)),
            pl.BlockSpec((8, GLA_DV), lambda b: (0, 0)),
            seq3(8, GLA_DK), seq3(8, GLA_DK), seq3(8, GLA_DV), seq3(8, GLA_DV), seq3(1, LANES),
            pl.BlockSpec((None, sb, GLA_H, GLA_DK, GLA_DV), lambda b: (layer, b, 0, 0, 0)),
        ],
        out_specs=[
            seq3(8, GLA_DV),
            pl.BlockSpec((sb, GLA_H, GLA_DK, GLA_DV), lambda b: (b, 0, 0, 0)),
        ],
        out_shape=[
            jax.ShapeDtypeStruct((bd, 8, GLA_DV), F32),
            jax.ShapeDtypeStruct(state.shape[1:], F32),
        ],
        compiler_params=_cparams(("parallel",)),
        name="gla_sample",
    )(wg, bg, gn, q, k, v, g, lr, state)


def _heads8(x, width):
    bd = x.shape[0]
    x = x.reshape(bd, -1, width)
    return jnp.pad(x, ((0, 0), (0, 8 - x.shape[1]), (0, 0)))


def kernel(x_prompt, mem_prompt, x_sample, cache_swa_k, cache_swa_v, state_gla, cache_mem_k, cache_mem_v,
           norm_g, w_in, attn_sinks, gla_w_gate, gla_b_gate, gla_norm_g, mem_norm_g, w_mem_kv, w_out,
           final_norm_g):
    batch, seq, d = x_prompt.shape
    bd = x_sample.shape[0]
    m = batch * seq
    kvw = SWA_KV * SWA_HD
    memw = MEM_H * MEM_HD

    cos_p, sin_p = _rope_tables(jnp.arange(seq))
    cos_s, sin_s = _rope_tables(PAST_LEN + jnp.arange(1))
    perm = np.array([2 * (r % 16) + r // 16 for r in range(32)])
    w_in_t = jnp.transpose(w_in, (0, 2, 1))

    xp = x_prompt.reshape(m, d)
    xs = x_sample.reshape(bd, d)
    memx = mem_prompt.reshape(batch * MEM_LEN, d)

    outs = {k: [] for k in ("kp", "vp", "sp", "mkp", "mvp", "ks", "vs", "ss")}
    for l in range(DEPTH):
        wg = jnp.pad(gla_w_gate[l], ((0, LANES - GLA_RANK), (0, 0))).astype(BF16)
        bg = gla_b_gate[l].reshape(1, -1)
        gn = gla_norm_g[l].reshape(1, -1)

        kv = project(rmsnorm(memx, mem_norm_g[l], BF16, 256), w_mem_kv, l, name="mem_kv")
        xn = rmsnorm(xp, norm_g[l], BF16, 256)
        xns = rmsnorm(xs, norm_g[l], BF16, bd)
        h, hs = project(xn, w_in_t, l, n=H_END, tile_rows=IN_TILE_SRC, a_small=xns, name="in_proj")

        o_swa_p, kb, vb = swa_prompt(h, attn_sinks[l], cos_p, sin_p, batch, seq)
        o_gla_p, sp = gla_prompt(h, wg, bg, gn, batch, seq, 256)
        o_mem_p = mem_attn_prompt(h, kv, batch, seq, 512)
        outs["kp"].append(kb.reshape(batch, WINDOW, SWA_KV, SWA_HD))
        outs["vp"].append(vb.reshape(batch, WINDOW, SWA_KV, SWA_HD))
        outs["sp"].append(sp)
        outs["mkp"].append(kv[:, :memw].reshape(batch, MEM_LEN, MEM_H, MEM_HD))
        outs["mvp"].append(kv[:, memw:].reshape(batch, MEM_LEN, MEM_H, MEM_HD))

        wb = cache_swa_k.shape[2]
        o_swa, kbs, vbs = swa_sample(
            hs[:, H_SQ:H_SQ + 2048].reshape(bd, 16, LANES), hs[:, H_SG:H_SG + 2048].reshape(bd, 16, LANES),
            hs[:, H_SK:H_SK + kvw].reshape(bd, 1, kvw), hs[:, H_SV:H_SV + kvw].reshape(bd, 1, kvw),
            cache_swa_k[l].reshape(bd, wb, kvw), cache_swa_v[l].reshape(bd, wb, kvw),
            cos_s, sin_s, jnp.broadcast_to(attn_sinks[l][perm][:, None], (32, LANES)))
        o_gla, ss = gla_sample(
            _heads8(hs[:, H_GQ:H_GQ + 512], GLA_DK), _heads8(hs[:, H_GK:H_GK + 512], GLA_DK),
            _heads8(hs[:, H_GV:H_GV + 1024], GLA_DV), _heads8(hs[:, H_GG:H_GG + 1024], GLA_DV),
            hs[:, H_GLR:H_GLR + LANES].reshape(bd, 1, LANES), wg, bg,
            jnp.pad(gla_norm_g[l].reshape(GLA_H, GLA_DV), ((0, 4), (0, 0))), state_gla, l)
        o_mem = mem_sample(hs[:, H_MQ:H_MQ + memw], hs[:, H_MG:H_MG + memw], cache_mem_k, cache_mem_v, l)
        mix_s = jnp.concatenate([o_swa.reshape(bd, 2048), o_gla[:, :GLA_H].reshape(bd, 1024), o_mem],
                                axis=1).astype(BF16)
        outs["ks"].append(kbs.reshape(bd, wb, SWA_KV, SWA_HD))
        outs["vs"].append(vbs.reshape(bd, wb, SWA_KV, SWA_HD))
        outs["ss"].append(ss)

        xp, xs = project([o_swa_p, o_gla_p, o_mem_p], w_out, l, a_small=mix_s, res=xp, res_small=xs,
                         name="out_proj")

    y_prompt = rmsnorm(xp, final_norm_g, F32, 256).reshape(batch, seq, d)
    y_sample = rmsnorm(xs, final_norm_g, F32, bd).reshape(bd, 1, d)
    st = lambda k: jnp.stack(outs[k])
    return (y_prompt, y_sample, st("kp"), st("vp"), st("sp"), st("mkp"), st("mvp"), st("ks"), st("vs"), st("ss"))
```

```python
import functools
import math

import jax
import jax.numpy as jnp
import numpy as np
from jax import lax
from jax.experimental import pallas as pl
from jax.experimental.pallas import tpu as pltpu

F32 = jnp.float32
BF16 = jnp.bfloat16

D_MODEL = 4096
DEPTH = 2
EPS = 1e-6
PAST_LEN = 16384
WINDOW = 128
ROPE_THETA = 10000.0
SWA_HD = 64
SWA_HQ = 32
SWA_KV = 4
GLA_H = 4
GLA_DK = 128
GLA_DV = 256
GLA_RANK = 16
GLA_TAU = 16.0
MEM_LEN = 256
MEM_H = 4
MEM_HD = 256
LANES = 128
GLA_CHUNK = 64
SUB = 16
SAMPLE_BLOCK = 4
SWA_STEP_BLOCKS = 4
LOG2E = math.log2(math.e)

TM = 2048
TN = 512
H_SQ, H_SG, H_GV, H_GG, H_MQ, H_MG, H_GQ, H_GK, H_SK, H_SV, H_GLR, H_END = (
    0, 2048, 4096, 5120, 6144, 7168, 8192, 8704, 9216, 9472, 9728, 9856)
_IN_PIECES = ((0, 2048), (2560, 2048), (5632, 1024), (6672, 1024), (7696, 1024), (8720, 1024), (4608, 512),
              (5120, 512), (2048, 512), (6656, 128))
IN_TILE_SRC = tuple(src + TN * t for src, width in _IN_PIECES for t in range(-(-width // TN)))
VMEM_LIMIT = 58 * 1024 * 1024


def _cparams(sem):
    return pltpu.CompilerParams(dimension_semantics=sem, vmem_limit_bytes=VMEM_LIMIT)


def _sigmoid(x):
    return 1.0 / (1.0 + jnp.exp(-x))


def _rmsnorm_kernel(x_ref, g_ref, o_ref):
    x = x_ref[...]
    ms = jnp.mean(x * x, axis=-1, keepdims=True)
    o_ref[...] = (x * lax.rsqrt(ms + EPS) * g_ref[...]).astype(o_ref.dtype)


def rmsnorm(x, g, out_dtype, tm):
    m, d = x.shape
    return pl.pallas_call(
        _rmsnorm_kernel,
        grid=(m // tm,),
        in_specs=[pl.BlockSpec((tm, d), lambda i: (i, 0)), pl.BlockSpec((1, d), lambda i: (0, 0))],
        out_specs=pl.BlockSpec((tm, d), lambda i: (i, 0)),
        out_shape=jax.ShapeDtypeStruct((m, d), out_dtype),
        compiler_params=_cparams(("parallel",)),
        name="rmsnorm",
    )(x, g.reshape(1, d))


def _proj_kernel(*refs, n_parts, has_small, has_res, tn, last_valid, w_rows):
    refs = list(refs)
    if w_rows:
        refs.pop(0)
    a_refs = [refs.pop(0) for _ in range(n_parts)]
    as_ref = refs.pop(0) if has_small else None
    w_ref = refs.pop(0)
    r_ref = refs.pop(0) if has_res else None
    rs_ref = refs.pop(0) if (has_res and has_small) else None
    o_ref = refs.pop(0)
    os_ref = refs.pop(0) if has_small else None
    i = pl.program_id(0)
    j = pl.program_id(1)
    nj = pl.num_programs(1)

    def compute(width):
        if w_rows:
            w = w_ref[:width, :].astype(BF16)
            mm = lambda a: lax.dot_general(a, w, (((1,), (1,)), ((), ())), preferred_element_type=F32)
            acc = mm(a_refs[0][...])
        else:
            w = w_ref[:, :width].astype(BF16)
            mm = lambda a: jnp.dot(a, w, preferred_element_type=F32)
            acc, k0 = None, 0
            for a_ref in a_refs:
                kp = a_ref.shape[1]
                part = jnp.dot(a_ref[...], w[k0:k0 + kp], preferred_element_type=F32)
                acc = part if acc is None else acc + part
                k0 += kp
        if has_res:
            acc = acc + r_ref[:, :width]
        o_ref[:, :width] = acc
        if has_small:
            @pl.when(i == 0)
            def _():
                accs = mm(as_ref[...])
                if has_res:
                    accs = accs + rs_ref[:, :width]
                os_ref[:, :width] = accs

    if last_valid == tn:
        compute(tn)
    else:
        pl.when(j < nj - 1)(lambda: compute(tn))
        pl.when(j == nj - 1)(lambda: compute(last_valid))


def project(a, w3, layer, n=None, tile_rows=None, a_small=None, res=None, res_small=None, name="proj"):
    a_parts = list(a) if isinstance(a, (tuple, list)) else [a]
    m = a_parts[0].shape[0]
    k = sum(p.shape[1] for p in a_parts)
    w_rows = tile_rows is not None
    if n is None:
        n = w3.shape[2]
    tm = min(TM, m)
    nj = pl.cdiv(n, TN)
    last_valid = n - (nj - 1) * TN
    has_small = a_small is not None
    has_res = res is not None
    small_idx = lambda i, j, *_: (0, jnp.where(i == 0, j, nj - 1))
    in_specs = [pl.BlockSpec((tm, p.shape[1]), lambda i, j, *_: (i, 0), pipeline_mode=pl.Buffered(1))
                for p in a_parts]
    args = list(a_parts)
    if has_small:
        ms = a_small.shape[0]
        in_specs.append(pl.BlockSpec((ms, k), lambda i, j, *_: (0, 0)))
        args.append(a_small)
    if w_rows:
        in_specs.append(pl.BlockSpec((None, pl.Element(TN), pl.Element(k)),
                                     lambda i, j, offs: (layer, pl.multiple_of(offs[j], 16), 0)))
    else:
        in_specs.append(pl.BlockSpec((None, k, TN), lambda i, j: (layer, 0, j)))
    args.append(w3)
    if has_res:
        in_specs.append(pl.BlockSpec((tm, TN), lambda i, j, *_: (i, j)))
        args.append(res)
        if has_small:
            in_specs.append(pl.BlockSpec((ms, TN), small_idx))
            args.append(res_small)
    out_specs = [pl.BlockSpec((tm, TN), lambda i, j, *_: (i, j))]
    out_shape = [jax.ShapeDtypeStruct((m, n), F32)]
    if has_small:
        out_specs.append(pl.BlockSpec((ms, TN), small_idx))
        out_shape.append(jax.ShapeDtypeStruct((ms, n), F32))
    kern = functools.partial(_proj_kernel, n_parts=len(a_parts), has_small=has_small, has_res=has_res, tn=TN,
                             last_valid=last_valid, w_rows=w_rows)
    if w_rows:
        grid_spec = pltpu.PrefetchScalarGridSpec(num_scalar_prefetch=1, grid=(m // tm, nj), in_specs=in_specs,
                                                 out_specs=out_specs)
        args = [jnp.asarray(tile_rows, jnp.int32)] + args
    else:
        grid_spec = pl.GridSpec(grid=(m // tm, nj), in_specs=in_specs, out_specs=out_specs)
    outs = pl.pallas_call(
        kern,
        grid_spec=grid_spec,
        out_shape=out_shape,
        compiler_params=_cparams(("arbitrary", "arbitrary")),
        name=name,
    )(*args)
    return outs if has_small else outs[0]


def _rope_tables(pos):
    half = SWA_HD // 2
    inv = ROPE_THETA ** (-jnp.arange(half, dtype=F32) / half)
    ang = pos.astype(F32)[:, None] * inv[None, :]
    cos, sin = jnp.cos(ang), jnp.sin(ang)
    return jnp.concatenate([cos, cos, cos, cos], axis=-1), jnp.concatenate([-sin, sin, -sin, sin], axis=-1)


def _rope128(x, cos, sin, first_half):
    partner = jnp.where(first_half, pltpu.roll(x, LANES - 32, 1), pltpu.roll(x, 32, 1))
    return x * cos + partner * sin


def _dup_halves(x, lo):
    r = pltpu.roll(x, 64, 1)
    return jnp.where(lo, x, r), jnp.where(lo, r, x)


def _swa_prompt_kernel(sink_ref, q_ref, g_ref, k_ref, v_ref, cos_ref, sin_ref,
                       o_ref, ko_ref, vo_ref, kprev, vprev):
    step = pl.program_id(1)

    @pl.when(step == 0)
    def _():
        kprev[...] = jnp.zeros_like(kprev)
        vprev[...] = jnp.zeros_like(vprev)

    w = WINDOW
    for sub in range(q_ref.shape[0] // w):
        rows = slice(sub * w, (sub + 1) * w)
        has_prev = None if sub > 0 else step > 0
        _swa_block(sink_ref, q_ref.at[rows], g_ref.at[rows], k_ref.at[rows], v_ref.at[rows], cos_ref.at[rows],
                   sin_ref.at[rows], o_ref.at[rows], ko_ref, vo_ref, kprev, vprev, has_prev)


def _swa_block(sink_ref, q_ref, g_ref, k_ref, v_ref, cos_ref, sin_ref, o_ref, ko_ref, vo_ref, kprev, vprev,
               has_prev):
    w = WINDOW
    cos = cos_ref[...]
    sin = sin_ref[...]
    lane = lax.broadcasted_iota(jnp.int32, (w, LANES), 1)
    rowi = lax.broadcasted_iota(jnp.int32, (w, LANES), 0)
    first_half = (lane & 32) == 0
    lo = lane < 64
    own = lane <= rowi
    ones = jnp.ones((2 * w, LANES), BF16)

    kall, vall = [], []
    kcur, vcur = [], []
    for p in range(2):
        kp = _rope128(k_ref[:, p * LANES:(p + 1) * LANES], cos, sin, first_half)
        vp = v_ref[:, p * LANES:(p + 1) * LANES]
        ko_ref[:, p * LANES:(p + 1) * LANES] = kp
        vo_ref[:, p * LANES:(p + 1) * LANES] = vp
        kcur += [x.astype(BF16) for x in _dup_halves(kp, lo)]
        vcur += [x.astype(BF16) for x in _dup_halves(vp, lo)]
    for kh in range(SWA_KV):
        kall.append(jnp.concatenate([kprev[kh], kcur[kh]], axis=0))
        vv = jnp.concatenate([vprev[kh], vcur[kh]], axis=0)
        vall.append(jnp.concatenate([vv, ones], axis=1))
    for kh in range(SWA_KV):
        kprev[kh] = kcur[kh]
        vprev[kh] = vcur[kh]

    scale = (SWA_HD ** -0.5) * LOG2E
    slabs = []
    for c in range(SWA_HQ // 2):
        qc = _rope128(q_ref[:, c * LANES:(c + 1) * LANES], cos, sin, first_half) * scale
        slabs.append(jnp.where(lo, qc, 0.0).astype(BF16))
        slabs.append(jnp.where(lo, 0.0, qc).astype(BF16))
    scores = [lax.dot_general(jnp.concatenate(slabs[8 * kh:8 * kh + 8], axis=0), kall[kh],
                              (((1,), (1,)), ((), ())), preferred_element_type=F32)
              for kh in range(SWA_KV)]
    keep = None if has_prev is None else jnp.where(own, 1, jnp.where(has_prev, 1, 0)) > 0
    folded = []
    for hq in range(SWA_HQ):
        sh = scores[hq // 8][(hq % 8) * w:(hq % 8 + 1) * w]
        f = jnp.where(own, sh[:, w:], sh[:, :w])
        folded.append(f if keep is None else jnp.where(keep, f, -1e30))
    maxes = [jnp.max(f, axis=-1, keepdims=True) for f in folded]
    probs = [jnp.exp2(f - m) for f, m in zip(folded, maxes)]
    sinks2 = [jnp.exp2(sink_ref[hq] * LOG2E - maxes[hq]) for hq in range(SWA_HQ)]
    ps = [jnp.concatenate([jnp.where(own, 0.0, p), jnp.where(own, p, 0.0)], axis=1).astype(BF16) for p in probs]
    outs = [jnp.dot(jnp.concatenate(ps[8 * kh:8 * kh + 8], axis=0), vall[kh], preferred_element_type=F32)
            for kh in range(SWA_KV)]
    normed = []
    for hq in range(SWA_HQ):
        oh = outs[hq // 8][(hq % 8) * w:(hq % 8 + 1) * w]
        normed.append(oh[:, :LANES] * (1.0 / (oh[:, LANES:] + sinks2[hq])))
    for c in range(SWA_HQ // 2):
        gt = g_ref[:, c * LANES:(c + 1) * LANES]
        o_ref[:, c * LANES:(c + 1) * LANES] = (
            jnp.where(lo, normed[2 * c], normed[2 * c + 1]) * (gt * _sigmoid(gt))).astype(o_ref.dtype)


def swa_prompt(h, sinks, cos, sin, batch, seq):
    m = h.shape[0]
    tq = SWA_STEP_BLOCKS * WINDOW
    nb = seq // tq
    kvw = SWA_KV * SWA_HD
    row = lambda b, n: b * nb + n
    return pl.pallas_call(
        _swa_prompt_kernel,
        grid=(batch, nb),
        in_specs=[
            pl.BlockSpec(memory_space=pltpu.SMEM),
            pl.BlockSpec((tq, 2048), lambda b, n: (row(b, n), H_SQ // 2048)),
            pl.BlockSpec((tq, 2048), lambda b, n: (row(b, n), H_SG // 2048)),
            pl.BlockSpec((tq, kvw), lambda b, n: (row(b, n), H_SK // kvw)),
            pl.BlockSpec((tq, kvw), lambda b, n: (row(b, n), H_SV // kvw)),
            pl.BlockSpec((tq, LANES), lambda b, n: (n, 0)),
            pl.BlockSpec((tq, LANES), lambda b, n: (n, 0)),
        ],
        out_specs=[
            pl.BlockSpec((tq, 2048), lambda b, n: (row(b, n), 0)),
            pl.BlockSpec((None, WINDOW, kvw), lambda b, n: (b, 0, 0)),
            pl.BlockSpec((None, WINDOW, kvw), lambda b, n: (b, 0, 0)),
        ],
        out_shape=[
            jax.ShapeDtypeStruct((m, SWA_HQ * SWA_HD), BF16),
            jax.ShapeDtypeStruct((batch, WINDOW, kvw), F32),
            jax.ShapeDtypeStruct((batch, WINDOW, kvw), F32),
        ],
        scratch_shapes=[pltpu.VMEM((SWA_KV, WINDOW, LANES), BF16), pltpu.VMEM((SWA_KV, WINDOW, LANES), BF16)],
        compiler_params=_cparams(("arbitrary", "arbitrary")),
        name="swa_prompt",
    )(sinks, h, h, h, h, cos, sin)


def _mem_prompt_kernel(q_ref, g_ref, k_ref, v_ref, o_ref):
    scale = MEM_HD ** -0.5
    for h in range(MEM_H):
        sl = slice(h * MEM_HD, (h + 1) * MEM_HD)
        q = (q_ref[:, sl] * scale).astype(BF16)
        k = k_ref[:, sl].astype(BF16)
        v = v_ref[:, sl].astype(BF16)
        s = lax.dot_general(q, k, (((1,), (1,)), ((), ())), preferred_element_type=F32)
        m = jnp.max(s, axis=-1, keepdims=True)
        p = jnp.exp(s - m)
        den = jnp.sum(p, axis=-1, keepdims=True)
        o = jnp.dot(p.astype(BF16), v, preferred_element_type=F32) * (1.0 / den)
        g = g_ref[:, sl]
        o_ref[:, sl] = (o * (g * _sigmoid(g))).astype(o_ref.dtype)


def mem_attn_prompt(h, kv, batch, seq, tq):
    nt = seq // tq
    w = MEM_H * MEM_HD
    row = lambda b, t: b * nt + t
    return pl.pallas_call(
        _mem_prompt_kernel,
        grid=(batch, nt),
        in_specs=[
            pl.BlockSpec((tq, w), lambda b, t: (row(b, t), H_MQ // w)),
            pl.BlockSpec((tq, w), lambda b, t: (row(b, t), H_MG // w)),
            pl.BlockSpec((MEM_LEN, w), lambda b, t: (b, 0)),
            pl.BlockSpec((MEM_LEN, w), lambda b, t: (b, 1)),
        ],
        out_specs=pl.BlockSpec((tq, w), lambda b, t: (row(b, t), 0)),
        out_shape=jax.ShapeDtypeStruct((h.shape[0], w), BF16),
        compiler_params=_cparams(("parallel", "parallel")),
        name="mem_prompt",
    )(h, h, kv, kv)


def _log_sigmoid(x):
    return jnp.minimum(x, 0.0) - jnp.log1p(jnp.exp(-jnp.abs(x)))


def _gla_prompt_kernel(q_ref, k_ref, v_ref, g_ref, lr_ref, wg_ref, bg_ref, gn_ref, lc_ref,
                       o_ref, so_ref, st, qs, ks_, bs_, am, qt, kt, qh, eb, ob):
    t = pl.program_id(1)
    tb = q_ref.shape[0]
    ch = GLA_CHUNK
    nc = tb // ch
    ng = tb // SUB
    nsub = ch // SUB
    kw = GLA_H * GLA_DK
    hk = lambda hd: slice(hd * GLA_DK, (hd + 1) * GLA_DK)
    hv = lambda hd: slice(hd * GLA_DV, (hd + 1) * GLA_DV)

    @pl.when(t == 0)
    def _():
        st[...] = jnp.zeros_like(st)

    lane = lax.broadcasted_iota(jnp.int32, (ng, LANES), 1)
    grp = lax.broadcasted_iota(jnp.int32, (ng, LANES), 0)
    rel = lane - SUB * (grp % nsub)
    rr = lax.broadcasted_iota(jnp.int32, (ch, LANES), 0) // SUB
    cc = lax.broadcasted_iota(jnp.int32, (ch, LANES), 1)
    off_mask = jnp.where(cc >= 8 * rr * (rr - 1), jnp.where(cc < 8 * rr * (rr + 1), 1, 0), 0) > 0

    x = jnp.dot(lr_ref[...].astype(BF16), wg_ref[...], preferred_element_type=F32) + bg_ref[...]
    g2 = _log_sigmoid(x) * (LOG2E / GLA_TAU)
    g_hi = g2.astype(BF16)
    r1 = g2 - g_hi.astype(F32)
    g_mid = r1.astype(BF16)
    g_lo = (r1 - g_mid.astype(F32)).astype(BF16)
    cum = jnp.dot(lc_ref[...], jnp.concatenate([g_hi, g_mid, g_lo], axis=1), preferred_element_type=F32)
    cum = cum[:, :kw] + cum[:, kw:2 * kw] + cum[:, 2 * kw:]
    b, bl, bsb = cum[:tb], cum[tb:2 * tb], cum[2 * tb:]
    q = q_ref[...] * (GLA_DK ** -0.5)
    k = k_ref[...]
    for hd in range(GLA_H):
        qs[hd] = q[:, hk(hd)]
        ks_[hd] = k[:, hk(hd)]
        bs_[hd] = b[:, hk(hd)]
    qt[...] = (q * jnp.exp2(b)).astype(BF16)
    kt[...] = (k * jnp.exp2(bl - b)).astype(BF16)
    eb[...] = jnp.exp2(bl)
    qh[...] = (q * jnp.exp2(b - bsb)).astype(BF16)

    def rows_i(ref, i):
        return jnp.concatenate([ref[hd, pl.ds(i, ng, stride=SUB), :] for hd in range(GLA_H)], axis=1)

    qd = [rows_i(qs, i) for i in range(SUB)]
    kd = [rows_i(ks_, i) for i in range(SUB)]
    bd = [rows_i(bs_, i) for i in range(SUB)]
    for i in range(SUB):
        a_i = [jnp.zeros((ng, LANES), F32) for _ in range(GLA_H)]
        for j in range(i + 1):
            tt = qd[i] * kd[j] * jnp.exp2(bd[i] - bd[j])
            for hd in range(GLA_H):
                col = jnp.sum(tt[:, hk(hd)], axis=-1, keepdims=True)
                a_i[hd] = jnp.where(rel == j, col, a_i[hd])
        for hd in range(GLA_H):
            am[hd, pl.ds(i, ng, stride=SUB), :] = a_i[hd]

    kalls = []
    for cidx in range(nc):
        r0 = cidx * ch
        kc, bc = k[r0:r0 + ch], b[r0:r0 + ch]
        parts = []
        for sub in range(1, nsub):
            s_i = bsb[r0 + sub * SUB:r0 + sub * SUB + 1, :]
            parts.append((kc[:sub * SUB] * jnp.exp2(s_i - bc[:sub * SUB])).astype(BF16))
        parts.append(jnp.zeros((LANES - sum(p.shape[0] for p in parts), kw), BF16))
        kalls.append(jnp.concatenate(parts, axis=0))
    scs = {}
    for cidx in range(nc):
        for hd in range(GLA_H):
            scs[cidx, hd] = lax.dot_general(qh[cidx * ch:(cidx + 1) * ch, hk(hd)], kalls[cidx][:, hk(hd)],
                                            (((1,), (1,)), ((), ())), preferred_element_type=F32)
    for cidx in range(nc):
        rows = slice(cidx * ch, (cidx + 1) * ch)
        for hd in range(GLA_H):
            lhs = jnp.concatenate([jnp.where(off_mask, scs[cidx, hd], 0.0).astype(BF16),
                                   am[hd, rows, :].astype(BF16)], axis=1)
            vc = v_ref[rows, hv(hd)].astype(BF16)
            rhs = jnp.concatenate(
                [vc[:s * SUB] for s in range(1, nsub)]
                + [jnp.zeros((LANES - SUB * nsub * (nsub - 1) // 2, GLA_DV), BF16), vc,
                   jnp.zeros((LANES - ch, GLA_DV), BF16)], axis=0)
            ob[hd, rows, :] = jnp.dot(lhs, rhs, preferred_element_type=F32)

    for cidx in range(nc):
        rows = slice(cidx * ch, (cidx + 1) * ch)
        for hd in range(GLA_H):
            vc = v_ref[rows, hv(hd)].astype(BF16)
            sb = st[hd]
            ob[hd, rows, :] += lax.dot_general(qt[rows, hk(hd)], sb.astype(BF16), (((1,), (1,)), ((), ())),
                                               preferred_element_type=F32)
            upd = lax.dot_general(vc, kt[rows, hk(hd)], (((0,), (0,)), ((), ())), preferred_element_type=F32)
            st[hd] = sb * eb[cidx * ch:cidx * ch + 1, hk(hd)] + upd

    for hd in range(GLA_H):
        vs = slice(hd * GLA_DV, (hd + 1) * GLA_DV)
        o = ob[hd]
        on = o * lax.rsqrt(jnp.mean(o * o, axis=-1, keepdims=True) + EPS) * gn_ref[:, vs]
        gg = g_ref[:, vs]
        o_ref[:, vs] = (on * (gg * _sigmoid(gg))).astype(o_ref.dtype)

    @pl.when(t == pl.num_programs(1) - 1)
    def _():
        for hd in range(GLA_H):
            so_ref[hd] = st[hd].T


def gla_prompt(h, wg, bg, gn, batch, seq, tb):
    nt = seq // tb
    kw = GLA_H * GLA_DK
    vw = GLA_H * GLA_DV
    row = lambda b, t: b * nt + t
    r = np.arange(tb)[:, None]
    c = np.arange(tb)[None, :]
    same = (r // GLA_CHUNK) == (c // GLA_CHUNK)
    lcat = jnp.asarray(np.concatenate([same & (c <= r), same & (c >= 0), same & (c < (r // SUB) * SUB)], axis=0),
                       BF16)
    return pl.pallas_call(
        _gla_prompt_kernel,
        grid=(batch, nt),
        in_specs=[
            pl.BlockSpec((tb, kw), lambda b, t: (row(b, t), H_GQ // kw)),
            pl.BlockSpec((tb, kw), lambda b, t: (row(b, t), H_GK // kw)),
            pl.BlockSpec((tb, vw), lambda b, t: (row(b, t), H_GV // vw)),
            pl.BlockSpec((tb, vw), lambda b, t: (row(b, t), H_GG // vw)),
            pl.BlockSpec((tb, LANES), lambda b, t: (row(b, t), H_GLR // LANES)),
            pl.BlockSpec((LANES, kw), lambda b, t: (0, 0)),
            pl.BlockSpec((1, kw), lambda b, t: (0, 0)),
            pl.BlockSpec((1, vw), lambda b, t: (0, 0)),
            pl.BlockSpec((3 * tb, tb), lambda b, t: (0, 0)),
        ],
        out_specs=[
            pl.BlockSpec((tb, vw), lambda b, t: (row(b, t), 0)),
            pl.BlockSpec((None, GLA_H, GLA_DK, GLA_DV), lambda b, t: (b, 0, 0, 0)),
        ],
        out_shape=[
            jax.ShapeDtypeStruct((h.shape[0], vw), BF16),
            jax.ShapeDtypeStruct((batch, GLA_H, GLA_DK, GLA_DV), F32),
        ],
        scratch_shapes=[
            pltpu.VMEM((GLA_H, GLA_DV, GLA_DK), F32),
            pltpu.VMEM((GLA_H, tb, GLA_DK), F32),
            pltpu.VMEM((GLA_H, tb, GLA_DK), F32),
            pltpu.VMEM((GLA_H, tb, GLA_DK), F32),
            pltpu.VMEM((GLA_H, tb, LANES), F32),
            pltpu.VMEM((tb, kw), BF16),
            pltpu.VMEM((tb, kw), BF16),
            pltpu.VMEM((tb, kw), BF16),
            pltpu.VMEM((tb, kw), F32),
            pltpu.VMEM((GLA_H, tb, GLA_DV), F32),
        ],
        compiler_params=_cparams(("arbitrary", "arbitrary")),
        name="gla_prompt",
    )(h, h, h, h, h, wg, bg, gn, lcat)


def _per_sequence(body):
    def kern(*refs, n_shared):
        shared = refs[:n_shared]
        for s in range(refs[n_shared].shape[0]):
            body(*shared, *[r.at[s] for r in refs[n_shared:]])
    return kern


def _swa_sample_kernel(cos_ref, sin_ref, sink_ref, q_ref, g_ref, k_ref, v_ref, kb_ref, vb_ref,
                       o_ref, ko_ref, vo_ref):
    w = kb_ref.shape[2]
    cos = cos_ref[...]
    sin = sin_ref[...]
    lane1 = lax.broadcasted_iota(jnp.int32, (1, LANES), 1)
    lane = lax.broadcasted_iota(jnp.int32, (16, LANES), 1)
    pos = lax.broadcasted_iota(jnp.int32, (SWA_HD, w), 1)
    eye = lax.broadcasted_iota(jnp.int32, (SWA_HD, SWA_HD), 0) == lax.broadcasted_iota(jnp.int32, (SWA_HD, SWA_HD), 1)
    lo = lane < 64

    def col(row):
        return jnp.sum(jnp.where(eye, jnp.broadcast_to(row, (SWA_HD, SWA_HD)), 0.0), axis=-1, keepdims=True)

    kk, vv = [], []
    for p in range(2):
        sl = slice(p * LANES, (p + 1) * LANES)
        knew = _rope128(k_ref[:, sl], cos, sin, (lane1 & 32) == 0)
        vnew = v_ref[:, sl]
        for half in range(2):
            kh = 2 * p + half
            hs = slice(half * SWA_HD, (half + 1) * SWA_HD)
            kwin = jnp.where(pos == w - 1, col(knew[:, hs]), pltpu.roll(kb_ref[kh], w - 1, 1))
            vwin = jnp.where(pos == w - 1, col(vnew[:, hs]), pltpu.roll(vb_ref[kh], w - 1, 1))
            ko_ref[kh] = kwin
            vo_ref[kh] = vwin
            kk.append(jnp.concatenate([kwin, kwin], axis=0).astype(BF16))
            vv.append(jnp.concatenate([vwin, vwin], axis=0).astype(BF16))

    q = _rope128(q_ref[...], cos, sin, (lane & 32) == 0) * (SWA_HD ** -0.5)
    qq = jnp.concatenate([jnp.where(lo, q, 0.0), jnp.where(lo, 0.0, q)], axis=0)
    grp = (lax.broadcasted_iota(jnp.int32, (32, LANES), 0) % 16) // 4
    s = jnp.zeros((32, w), F32)
    for kh in range(SWA_KV):
        s = s + jnp.dot(jnp.where(grp == kh, qq, 0.0).astype(BF16), kk[kh], preferred_element_type=F32)
    sk = sink_ref[...][:, 0:1]
    m = jnp.maximum(jnp.max(s, axis=-1, keepdims=True), sk)
    p = jnp.exp(s - m)
    den = jnp.sum(p, axis=-1, keepdims=True) + jnp.exp(sk - m)
    o = jnp.zeros((32, LANES), F32)
    for kh in range(SWA_KV):
        o = o + lax.dot_general(jnp.where(grp == kh, p, 0.0).astype(BF16), vv[kh], (((1,), (1,)), ((), ())),
                                preferred_element_type=F32)
    o = o * (1.0 / den)
    g = g_ref[...]
    o_ref[...] = jnp.where(lo, o[0:16], o[16:32]) * (g * _sigmoid(g))


def swa_sample(q, g, k, v, kcache_t, vcache_t, layer, cos, sin, sink_rows):
    _, bd, nkv, hd, wb = kcache_t.shape
    kvw = nkv * hd
    sb = SAMPLE_BLOCK
    seq3 = lambda r, c: pl.BlockSpec((sb, r, c), lambda b: (b, 0, 0))
    cache_in = pl.BlockSpec((None, sb, nkv, hd, wb), lambda b: (layer, b, 0, 0, 0))
    cache_out = pl.BlockSpec((sb, nkv, hd, wb), lambda b: (b, 0, 0, 0))
    return pl.pallas_call(
        functools.partial(_per_sequence(_swa_sample_kernel), n_shared=3),
        grid=(bd // sb,),
        in_specs=[
            pl.BlockSpec((1, LANES), lambda b: (0, 0)),
            pl.BlockSpec((1, LANES), lambda b: (0, 0)),
            pl.BlockSpec((32, LANES), lambda b: (0, 0)),
            seq3(16, LANES), seq3(16, LANES), seq3(1, kvw), seq3(1, kvw), cache_in, cache_in,
        ],
        out_specs=[seq3(16, LANES), cache_out, cache_out],
        out_shape=[
            jax.ShapeDtypeStruct((bd, 16, LANES), F32),
            jax.ShapeDtypeStruct((bd, nkv, hd, wb), F32),
            jax.ShapeDtypeStruct((bd, nkv, hd, wb), F32),
        ],
        compiler_params=_cparams(("parallel",)),
        name="swa_sample",
    )(cos, sin, sink_rows, q, g, k, v, kcache_t, vcache_t)


def _mem_sample_kernel(q_ref, g_ref, k_ref, v_ref, o_ref):
    n = k_ref.shape[0]
    grp = 2 * MEM_H
    q8 = (q_ref[...] * (MEM_HD ** -0.5)).astype(BF16)
    r = lax.dot_general(q8, k_ref[...].astype(BF16), (((1,), (1,)), ((), ())), preferred_element_type=F32)
    row = lax.broadcasted_iota(jnp.int32, (grp, n), 0)
    col = lax.broadcasted_iota(jnp.int32, (grp, n), 1)
    own = (col % grp) == row
    a = jnp.where(own, r, 0.0)
    s = a + pltpu.roll(pltpu.roll(a, MEM_H, 0), n - MEM_H, 1)
    valid = jnp.where(own, jnp.where(row < MEM_H, 1, 0), 0) > 0
    sm = jnp.where(valid, s, -1e30)
    m = jnp.max(sm, axis=-1, keepdims=True)
    p = jnp.where(valid, jnp.exp(sm - m), 0.0)
    den = jnp.sum(p, axis=-1, keepdims=True)
    pn = p * (1.0 / jnp.where(den > 0.0, den, 1.0))
    p8 = pn + pltpu.roll(pltpu.roll(pn, MEM_H, 0), MEM_H, 1)
    o = jnp.dot(p8.astype(BF16), v_ref[...].astype(BF16), preferred_element_type=F32)
    g = g_ref[...]
    o_ref[...] = o * (g * _sigmoid(g))


def _cache_rows(c):
    depth, bd, mlen, nh, hd = c.shape
    return c.reshape(depth, bd, mlen, nh, hd // LANES, LANES).transpose(0, 1, 2, 4, 3, 5).reshape(
        depth, bd, mlen * nh * (hd // LANES), LANES)


def _half_head_rows(x):
    bd = x.shape[0]
    return x.reshape(bd, MEM_H, MEM_HD // LANES, LANES).transpose(0, 2, 1, 3).reshape(bd, -1, LANES)


def mem_sample(q, g, k, v, layer):
    bd = q.shape[0]
    sb = SAMPLE_BLOCK
    rows = MEM_LEN * MEM_H * (MEM_HD // LANES)
    cache_spec = pl.BlockSpec((None, sb, rows, LANES), lambda b: (layer, b, 0, 0))
    vec_spec = pl.BlockSpec((sb, 2 * MEM_H, LANES), lambda b: (b, 0, 0))
    o = pl.pallas_call(
        functools.partial(_per_sequence(_mem_sample_kernel), n_shared=0),
        grid=(bd // sb,),
        in_specs=[vec_spec, vec_spec, cache_spec, cache_spec],
        out_specs=vec_spec,
        out_shape=jax.ShapeDtypeStruct((bd, 2 * MEM_H, LANES), F32),
        compiler_params=_cparams(("parallel",)),
        name="mem_sample",
    )(_half_head_rows(q), _half_head_rows(g), _cache_rows(k), _cache_rows(v))
    return o.reshape(bd, MEM_HD // LANES, MEM_H, LANES).transpose(0, 2, 1, 3).reshape(bd, MEM_H * MEM_HD)


def _gla_sample_kernel(wg_ref, bg_ref, gn_ref, q_ref, k_ref, v_ref, g_ref, lr_ref, s_ref, o_ref, so_ref):
    rowk = lax.broadcasted_iota(jnp.int32, (8, GLA_DK), 0)
    lr = jnp.broadcast_to(lr_ref[...], (8, LANES)).astype(BF16)
    xa = jnp.dot(lr, wg_ref[...], preferred_element_type=F32) + bg_ref[...]
    x = jnp.zeros((8, GLA_DK), F32)
    for hd in range(GLA_H):
        x = jnp.where(rowk == hd, xa[:, hd * GLA_DK:(hd + 1) * GLA_DK], x)
    gl = _log_sigmoid(x) * (1.0 / GLA_TAU)
    eg = jnp.exp(gl)
    q = q_ref[...] * (GLA_DK ** -0.5)
    k = k_ref[...]
    v = v_ref[...]
    qe = q * eg
    eye = lax.broadcasted_iota(jnp.int32, (GLA_DK, GLA_DK), 0) == lax.broadcasted_iota(jnp.int32, (GLA_DK, GLA_DK), 1)

    def col(row):
        return jnp.sum(jnp.where(eye, jnp.broadcast_to(row, (GLA_DK, GLA_DK)), 0.0), axis=-1, keepdims=True)

    o = jnp.sum(q * k, axis=-1, keepdims=True) * v
    for hd in range(GLA_H):
        s0 = s_ref[hd]
        o = o + jnp.dot(jnp.where(rowk == hd, qe, 0.0).astype(BF16), s0.astype(BF16), preferred_element_type=F32)
        so_ref[hd] = s0 * col(eg[hd:hd + 1, :]) + col(k[hd:hd + 1, :]) * v[hd:hd + 1, :]
    on = o * lax.rsqrt(jnp.mean(o * o, axis=-1, keepdims=True) + EPS) * gn_ref[...]
    gg = g_ref[...]
    o_ref[...] = on * (gg * _sigmoid(gg))


def gla_sample(q, k, v, g, lr, wg, bg, gn, state, layer):
    bd = q.shape[0]
    sb = SAMPLE_BLOCK
    kw = GLA_H * GLA_DK
    seq3 = lambda r, c: pl.BlockSpec((sb, r, c), lambda b: (b, 0, 0))
    return pl.pallas_call(
        functools.partial(_per_sequence(_gla_sample_kernel), n_shared=3),
        grid=(bd // sb,),
        in_specs=[
            pl.BlockSpec((LANES, kw), lambda b: (0, 0)),
            pl.BlockSpec((1, kw), lambda b: (0, 0)),
            pl.BlockSpec((8, GLA_DV), lambda b: (0, 0)),
            seq3(8, GLA_DK), seq3(8, GLA_DK), seq3(8, GLA_DV), seq3(8, GLA_DV), seq3(1, LANES),
            pl.BlockSpec((None, sb, GLA_H, GLA_DK, GLA_DV), lambda b: (layer, b, 0, 0, 0)),
        ],
        out_specs=[
            seq3(8, GLA_DV),
            pl.BlockSpec((sb, GLA_H, GLA_DK, GLA_DV), lambda b: (b, 0, 0, 0)),
        ],
        out_shape=[
            jax.ShapeDtypeStruct((bd, 8, GLA_DV), F32),
            jax.ShapeDtypeStruct(state.shape[1:], F32),
        ],
        compiler_params=_cparams(("parallel",)),
        name="gla_sample",
    )(wg, bg, gn, q, k, v, g, lr, state)


def _heads8(x, width):
    bd = x.shape[0]
    x = x.reshape(bd, -1, width)
    return jnp.pad(x, ((0, 0), (0, 8 - x.shape[1]), (0, 0)))


def kernel(x_prompt, mem_prompt, x_sample, cache_swa_k, cache_swa_v, state_gla, cache_mem_k, cache_mem_v,
           norm_g, w_in, attn_sinks, gla_w_gate, gla_b_gate, gla_norm_g, mem_norm_g, w_mem_kv, w_out,
           final_norm_g):
    batch, seq, d = x_prompt.shape
    bd = x_sample.shape[0]
    m = batch * seq
    kvw = SWA_KV * SWA_HD
    memw = MEM_H * MEM_HD

    cos_p, sin_p = _rope_tables(jnp.arange(seq))
    cos_s, sin_s = _rope_tables(PAST_LEN + jnp.arange(1))
    perm = np.array([2 * (r % 16) + r // 16 for r in range(32)])
    w_in_t = jnp.transpose(w_in, (0, 2, 1))
    swa_k_t = jnp.transpose(cache_swa_k, (0, 1, 3, 4, 2))
    swa_v_t = jnp.transpose(cache_swa_v, (0, 1, 3, 4, 2))

    xp = x_prompt.reshape(m, d)
    xs = x_sample.reshape(bd, d)
    memx = mem_prompt.reshape(batch * MEM_LEN, d)

    outs = {k: [] for k in ("kp", "vp", "sp", "mkp", "mvp", "ks", "vs", "ss")}
    for l in range(DEPTH):
        wg = jnp.pad(gla_w_gate[l], ((0, LANES - GLA_RANK), (0, 0))).astype(BF16)
        bg = gla_b_gate[l].reshape(1, -1)
        gn = gla_norm_g[l].reshape(1, -1)

        kv = project(rmsnorm(memx, mem_norm_g[l], BF16, 256), w_mem_kv, l, name="mem_kv")
        xn = rmsnorm(xp, norm_g[l], BF16, 256)
        xns = rmsnorm(xs, norm_g[l], BF16, bd)
        h, hs = project(xn, w_in_t, l, n=H_END, tile_rows=IN_TILE_SRC, a_small=xns, name="in_proj")

        o_swa_p, kb, vb = swa_prompt(h, attn_sinks[l], cos_p, sin_p, batch, seq)
        o_gla_p, sp = gla_prompt(h, wg, bg, gn, batch, seq, 256)
        o_mem_p = mem_attn_prompt(h, kv, batch, seq, 512)
        outs["kp"].append(kb.reshape(batch, WINDOW, SWA_KV, SWA_HD))
        outs["vp"].append(vb.reshape(batch, WINDOW, SWA_KV, SWA_HD))
        outs["sp"].append(sp)
        outs["mkp"].append(kv[:, :memw].reshape(batch, MEM_LEN, MEM_H, MEM_HD))
        outs["mvp"].append(kv[:, memw:].reshape(batch, MEM_LEN, MEM_H, MEM_HD))

        o_swa, kbs, vbs = swa_sample(
            hs[:, H_SQ:H_SQ + 2048].reshape(bd, 16, LANES), hs[:, H_SG:H_SG + 2048].reshape(bd, 16, LANES),
            hs[:, H_SK:H_SK + kvw].reshape(bd, 1, kvw), hs[:, H_SV:H_SV + kvw].reshape(bd, 1, kvw),
            swa_k_t, swa_v_t, l, cos_s, sin_s, jnp.broadcast_to(attn_sinks[l][perm][:, None], (32, LANES)))
        o_gla, ss = gla_sample(
            _heads8(hs[:, H_GQ:H_GQ + 512], GLA_DK), _heads8(hs[:, H_GK:H_GK + 512], GLA_DK),
            _heads8(hs[:, H_GV:H_GV + 1024], GLA_DV), _heads8(hs[:, H_GG:H_GG + 1024], GLA_DV),
            hs[:, H_GLR:H_GLR + LANES].reshape(bd, 1, LANES), wg, bg,
            jnp.pad(gla_norm_g[l].reshape(GLA_H, GLA_DV), ((0, 4), (0, 0))), state_gla, l)
        o_mem = mem_sample(hs[:, H_MQ:H_MQ + memw], hs[:, H_MG:H_MG + memw], cache_mem_k, cache_mem_v, l)
        mix_s = jnp.concatenate([o_swa.reshape(bd, 2048), o_gla[:, :GLA_H].reshape(bd, 1024), o_mem],
                                axis=1).astype(BF16)
        outs["ks"].append(kbs.transpose(0, 3, 1, 2))
        outs["vs"].append(vbs.transpose(0, 3, 1, 2))
        outs["ss"].append(ss)

        xp, xs = project([o_swa_p, o_gla_p, o_mem_p], w_out, l, a_small=mix_s, res=xp, res_small=xs,
                         name="out_proj")

    y_prompt = rmsnorm(xp, final_norm_g, F32, 256).reshape(batch, seq, d)
    y_sample = rmsnorm(xs, final_norm_g, F32, bd).reshape(bd, 1, d)
    st = lambda k: jnp.stack(outs[k])
    return (y_prompt, y_sample, st("kp"), st("vp"), st("sp"), st("mkp"), st("mvp"), st("ks"), st("vs"), st("ss"))
```

```python
import functools
import math

import jax
import jax.numpy as jnp
import numpy as np
from jax import lax
from jax.experimental import pallas as pl
from jax.experimental.pallas import tpu as pltpu

F32 = jnp.float32
BF16 = jnp.bfloat16

D_MODEL = 4096
DEPTH = 2
EPS = 1e-6
PAST_LEN = 16384
WINDOW = 128
ROPE_THETA = 10000.0
SWA_HD = 64
SWA_HQ = 32
SWA_KV = 4
GLA_H = 4
GLA_DK = 128
GLA_DV = 256
GLA_RANK = 16
GLA_TAU = 16.0
MEM_LEN = 256
MEM_H = 4
MEM_HD = 256
LANES = 128
GLA_CHUNK = 64
SUB = 16
SAMPLE_BLOCK = 8
SWA_SAMPLE_BLOCK = 4
SWA_STEP_BLOCKS = 4
LOG2E = math.log2(math.e)

TM = 2048
TN = 512
H_SQ, H_SG, H_GV, H_GG, H_MQ, H_MG, H_GQ, H_GK, H_SK, H_SV, H_GLR, H_END = (
    0, 2048, 4096, 5120, 6144, 7168, 8192, 8704, 9216, 9472, 9728, 9856)
_IN_PIECES = ((0, 2048), (2560, 2048), (5632, 1024), (6672, 1024), (7696, 1024), (8720, 1024), (4608, 512),
              (5120, 512), (2048, 512), (6656, 128))
IN_TILE_SRC = tuple(src + TN * t for src, width in _IN_PIECES for t in range(-(-width // TN)))
VMEM_LIMIT = 58 * 1024 * 1024


def _cparams(sem):
    return pltpu.CompilerParams(dimension_semantics=sem, vmem_limit_bytes=VMEM_LIMIT)


def _sigmoid(x):
    return 1.0 / (1.0 + jnp.exp(-x))


def _rmsnorm_kernel(x_ref, g_ref, o_ref):
    x = x_ref[...]
    ms = jnp.mean(x * x, axis=-1, keepdims=True)
    o_ref[...] = (x * lax.rsqrt(ms + EPS) * g_ref[...]).astype(o_ref.dtype)


def rmsnorm(x, g, out_dtype, tm):
    m, d = x.shape
    return pl.pallas_call(
        _rmsnorm_kernel,
        grid=(m // tm,),
        in_specs=[pl.BlockSpec((tm, d), lambda i: (i, 0)), pl.BlockSpec((1, d), lambda i: (0, 0))],
        out_specs=pl.BlockSpec((tm, d), lambda i: (i, 0)),
        out_shape=jax.ShapeDtypeStruct((m, d), out_dtype),
        compiler_params=_cparams(("parallel",)),
        name="rmsnorm",
    )(x, g.reshape(1, d))


def _rmsnorm_multi_kernel(x_ref, g_ref, *o_refs):
    x = x_ref[...]
    y = x * lax.rsqrt(jnp.mean(x * x, axis=-1, keepdims=True) + EPS)
    for l, o_ref in enumerate(o_refs):
        o_ref[...] = (y * g_ref[l:l + 1, :]).astype(o_ref.dtype)


def rmsnorm_multi(x, gains, out_dtype, tm):
    m, d = x.shape
    n = gains.shape[0]
    return pl.pallas_call(
        _rmsnorm_multi_kernel,
        grid=(m // tm,),
        in_specs=[pl.BlockSpec((tm, d), lambda i: (i, 0)), pl.BlockSpec((n, d), lambda i: (0, 0))],
        out_specs=[pl.BlockSpec((tm, d), lambda i: (i, 0))] * n,
        out_shape=[jax.ShapeDtypeStruct((m, d), out_dtype)] * n,
        compiler_params=_cparams(("parallel",)),
        name="rmsnorm_multi",
    )(x, gains)


def _proj_kernel(*refs, n_parts, has_small, has_res, tn, last_valid, w_rows):
    refs = list(refs)
    if w_rows:
        refs.pop(0)
    a_refs = [refs.pop(0) for _ in range(n_parts)]
    as_ref = refs.pop(0) if has_small else None
    w_ref = refs.pop(0)
    r_ref = refs.pop(0) if has_res else None
    rs_ref = refs.pop(0) if (has_res and has_small) else None
    o_ref = refs.pop(0)
    os_ref = refs.pop(0) if has_small else None
    i = pl.program_id(0)
    j = pl.program_id(1)
    nj = pl.num_programs(1)

    def compute(width):
        if w_rows:
            w = w_ref[:width, :].astype(BF16)
            mm = lambda a: lax.dot_general(a, w, (((1,), (1,)), ((), ())), preferred_element_type=F32)
            acc = mm(a_refs[0][...])
        else:
            w = w_ref[:, :width].astype(BF16)
            mm = lambda a: jnp.dot(a, w, preferred_element_type=F32)
            acc, k0 = None, 0
            for a_ref in a_refs:
                kp = a_ref.shape[1]
                part = jnp.dot(a_ref[...], w[k0:k0 + kp], preferred_element_type=F32)
                acc = part if acc is None else acc + part
                k0 += kp
        if has_res:
            acc = acc + r_ref[:, :width]
        o_ref[:, :width] = acc
        if has_small:
            @pl.when(i == 0)
            def _():
                accs = mm(as_ref[...])
                if has_res:
                    accs = accs + rs_ref[:, :width]
                os_ref[:, :width] = accs

    if last_valid == tn:
        compute(tn)
    else:
        pl.when(j < nj - 1)(lambda: compute(tn))
        pl.when(j == nj - 1)(lambda: compute(last_valid))


def project(a, w3, layer, n=None, tile_rows=None, a_small=None, res=None, res_small=None, name="proj"):
    a_parts = list(a) if isinstance(a, (tuple, list)) else [a]
    m = a_parts[0].shape[0]
    k = sum(p.shape[1] for p in a_parts)
    w_rows = tile_rows is not None
    if n is None:
        n = w3.shape[2]
    tm = min(TM, m)
    nj = pl.cdiv(n, TN)
    last_valid = n - (nj - 1) * TN
    has_small = a_small is not None
    has_res = res is not None
    small_idx = lambda i, j, *_: (0, jnp.where(i == 0, j, nj - 1))
    in_specs = [pl.BlockSpec((tm, p.shape[1]), lambda i, j, *_: (i, 0), pipeline_mode=pl.Buffered(1))
                for p in a_parts]
    args = list(a_parts)
    if has_small:
        ms = a_small.shape[0]
        in_specs.append(pl.BlockSpec((ms, k), lambda i, j, *_: (0, 0)))
        args.append(a_small)
    if w_rows:
        in_specs.append(pl.BlockSpec((None, pl.Element(TN), pl.Element(k)),
                                     lambda i, j, offs: (layer, pl.multiple_of(offs[j], 16), 0)))
    else:
        in_specs.append(pl.BlockSpec((None, k, TN), lambda i, j: (layer, 0, j)))
    args.append(w3)
    if has_res:
        in_specs.append(pl.BlockSpec((tm, TN), lambda i, j, *_: (i, j)))
        args.append(res)
        if has_small:
            in_specs.append(pl.BlockSpec((ms, TN), small_idx))
            args.append(res_small)
    out_specs = [pl.BlockSpec((tm, TN), lambda i, j, *_: (i, j))]
    out_shape = [jax.ShapeDtypeStruct((m, n), F32)]
    if has_small:
        out_specs.append(pl.BlockSpec((ms, TN), small_idx))
        out_shape.append(jax.ShapeDtypeStruct((ms, n), F32))
    kern = functools.partial(_proj_kernel, n_parts=len(a_parts), has_small=has_small, has_res=has_res, tn=TN,
                             last_valid=last_valid, w_rows=w_rows)
    if w_rows:
        grid_spec = pltpu.PrefetchScalarGridSpec(num_scalar_prefetch=1, grid=(m // tm, nj), in_specs=in_specs,
                                                 out_specs=out_specs)
        args = [jnp.asarray(tile_rows, jnp.int32)] + args
    else:
        grid_spec = pl.GridSpec(grid=(m // tm, nj), in_specs=in_specs, out_specs=out_specs)
    outs = pl.pallas_call(
        kern,
        grid_spec=grid_spec,
        out_shape=out_shape,
        compiler_params=_cparams(("arbitrary", "arbitrary")),
        name=name,
    )(*args)
    return outs if has_small else outs[0]


def _rope_tables(pos):
    half = SWA_HD // 2
    inv = ROPE_THETA ** (-jnp.arange(half, dtype=F32) / half)
    ang = pos.astype(F32)[:, None] * inv[None, :]
    cos, sin = jnp.cos(ang), jnp.sin(ang)
    return jnp.concatenate([cos, cos, cos, cos], axis=-1), jnp.concatenate([-sin, sin, -sin, sin], axis=-1)


def _rope128(x, cos, sin, first_half):
    partner = jnp.where(first_half, pltpu.roll(x, LANES - 32, 1), pltpu.roll(x, 32, 1))
    return x * cos + partner * sin


def _dup_halves(x, lo):
    r = pltpu.roll(x, 64, 1)
    return jnp.where(lo, x, r), jnp.where(lo, r, x)


def _swa_prompt_kernel(sink_ref, q_ref, g_ref, k_ref, v_ref, cos_ref, sin_ref,
                       o_ref, ko_ref, vo_ref, kprev, vprev):
    step = pl.program_id(1)

    @pl.when(step == 0)
    def _():
        kprev[...] = jnp.zeros_like(kprev)
        vprev[...] = jnp.zeros_like(vprev)

    w = WINDOW
    for sub in range(q_ref.shape[0] // w):
        rows = slice(sub * w, (sub + 1) * w)
        has_prev = None if sub > 0 else step > 0
        _swa_block(sink_ref, q_ref.at[rows], g_ref.at[rows], k_ref.at[rows], v_ref.at[rows], cos_ref.at[rows],
                   sin_ref.at[rows], o_ref.at[rows], ko_ref, vo_ref, kprev, vprev, has_prev)


def _swa_block(sink_ref, q_ref, g_ref, k_ref, v_ref, cos_ref, sin_ref, o_ref, ko_ref, vo_ref, kprev, vprev,
               has_prev):
    w = WINDOW
    cos = cos_ref[...]
    sin = sin_ref[...]
    lane = lax.broadcasted_iota(jnp.int32, (w, LANES), 1)
    rowi = lax.broadcasted_iota(jnp.int32, (w, LANES), 0)
    first_half = (lane & 32) == 0
    lo = lane < 64
    own = lane <= rowi
    ones = jnp.ones((2 * w, LANES), BF16)

    kall, vall = [], []
    kcur, vcur = [], []
    for p in range(2):
        kp = _rope128(k_ref[:, p * LANES:(p + 1) * LANES], cos, sin, first_half)
        vp = v_ref[:, p * LANES:(p + 1) * LANES]
        ko_ref[:, p * LANES:(p + 1) * LANES] = kp
        vo_ref[:, p * LANES:(p + 1) * LANES] = vp
        kcur += [x.astype(BF16) for x in _dup_halves(kp, lo)]
        vcur += [x.astype(BF16) for x in _dup_halves(vp, lo)]
    for kh in range(SWA_KV):
        kall.append(jnp.concatenate([kprev[kh], kcur[kh]], axis=0))
        vv = jnp.concatenate([vprev[kh], vcur[kh]], axis=0)
        vall.append(jnp.concatenate([vv, ones], axis=1))
    for kh in range(SWA_KV):
        kprev[kh] = kcur[kh]
        vprev[kh] = vcur[kh]

    scale = (SWA_HD ** -0.5) * LOG2E
    slabs = []
    for c in range(SWA_HQ // 2):
        qc = _rope128(q_ref[:, c * LANES:(c + 1) * LANES], cos, sin, first_half) * scale
        slabs.append(jnp.where(lo, qc, 0.0).astype(BF16))
        slabs.append(jnp.where(lo, 0.0, qc).astype(BF16))
    scores = [lax.dot_general(jnp.concatenate(slabs[8 * kh:8 * kh + 8], axis=0), kall[kh],
                              (((1,), (1,)), ((), ())), preferred_element_type=F32)
              for kh in range(SWA_KV)]
    keep = None if has_prev is None else jnp.where(own, 1, jnp.where(has_prev, 1, 0)) > 0
    folded = []
    for hq in range(SWA_HQ):
        sh = scores[hq // 8][(hq % 8) * w:(hq % 8 + 1) * w]
        f = jnp.where(own, sh[:, w:], sh[:, :w])
        folded.append(f if keep is None else jnp.where(keep, f, -1e30))
    maxes = [jnp.max(f, axis=-1, keepdims=True) for f in folded]
    probs = [jnp.exp2(f - m) for f, m in zip(folded, maxes)]
    sinks2 = [jnp.exp2(sink_ref[hq] * LOG2E - maxes[hq]) for hq in range(SWA_HQ)]
    ps = [jnp.concatenate([jnp.where(own, 0.0, p), jnp.where(own, p, 0.0)], axis=1).astype(BF16) for p in probs]
    outs = [jnp.dot(jnp.concatenate(ps[8 * kh:8 * kh + 8], axis=0), vall[kh], preferred_element_type=F32)
            for kh in range(SWA_KV)]
    normed = []
    for hq in range(SWA_HQ):
        oh = outs[hq // 8][(hq % 8) * w:(hq % 8 + 1) * w]
        normed.append(oh[:, :LANES] * (1.0 / (oh[:, LANES:] + sinks2[hq])))
    for c in range(SWA_HQ // 2):
        gt = g_ref[:, c * LANES:(c + 1) * LANES]
        o_ref[:, c * LANES:(c + 1) * LANES] = (
            jnp.where(lo, normed[2 * c], normed[2 * c + 1]) * (gt * _sigmoid(gt))).astype(o_ref.dtype)


def swa_prompt(h, sinks, cos, sin, batch, seq):
    m = h.shape[0]
    tq = SWA_STEP_BLOCKS * WINDOW
    nb = seq // tq
    kvw = SWA_KV * SWA_HD
    row = lambda b, n: b * nb + n
    return pl.pallas_call(
        _swa_prompt_kernel,
        grid=(batch, nb),
        in_specs=[
            pl.BlockSpec(memory_space=pltpu.SMEM),
            pl.BlockSpec((tq, 2048), lambda b, n: (row(b, n), H_SQ // 2048)),
            pl.BlockSpec((tq, 2048), lambda b, n: (row(b, n), H_SG // 2048)),
            pl.BlockSpec((tq, kvw), lambda b, n: (row(b, n), H_SK // kvw)),
            pl.BlockSpec((tq, kvw), lambda b, n: (row(b, n), H_SV // kvw)),
            pl.BlockSpec((tq, LANES), lambda b, n: (n, 0)),
            pl.BlockSpec((tq, LANES), lambda b, n: (n, 0)),
        ],
        out_specs=[
            pl.BlockSpec((tq, 2048), lambda b, n: (row(b, n), 0)),
            pl.BlockSpec((None, WINDOW, kvw), lambda b, n: (b, 0, 0)),
            pl.BlockSpec((None, WINDOW, kvw), lambda b, n: (b, 0, 0)),
        ],
        out_shape=[
            jax.ShapeDtypeStruct((m, SWA_HQ * SWA_HD), BF16),
            jax.ShapeDtypeStruct((batch, WINDOW, kvw), F32),
            jax.ShapeDtypeStruct((batch, WINDOW, kvw), F32),
        ],
        scratch_shapes=[pltpu.VMEM((SWA_KV, WINDOW, LANES), BF16), pltpu.VMEM((SWA_KV, WINDOW, LANES), BF16)],
        compiler_params=_cparams(("arbitrary", "arbitrary")),
        name="swa_prompt",
    )(sinks, h, h, h, h, cos, sin)


def _mem_prompt_kernel(q_ref, g_ref, k_ref, v_ref, o_ref):
    scale = (MEM_HD ** -0.5) * LOG2E
    heads = [slice(h * MEM_HD, (h + 1) * MEM_HD) for h in range(MEM_H)]
    ones = jnp.ones((MEM_LEN, LANES), BF16)
    scores = [lax.dot_general((q_ref[:, sl] * scale).astype(BF16), k_ref[:, sl].astype(BF16),
                              (((1,), (1,)), ((), ())), preferred_element_type=F32) for sl in heads]
    probs = [jnp.exp2(s - jnp.max(s, axis=-1, keepdims=True)).astype(BF16) for s in scores]
    outs = [jnp.dot(p, jnp.concatenate([v_ref[:, sl].astype(BF16), ones], axis=1), preferred_element_type=F32)
            for p, sl in zip(probs, heads)]
    for o, sl in zip(outs, heads):
        inv = 1.0 / o[:, MEM_HD:]
        g = g_ref[:, sl]
        o_ref[:, sl] = (o[:, :MEM_HD] * jnp.concatenate([inv, inv], axis=1) * (g * _sigmoid(g))).astype(o_ref.dtype)


def mem_attn_prompt(h, kv, batch, seq, tq):
    nt = seq // tq
    w = MEM_H * MEM_HD
    row = lambda b, t: b * nt + t
    return pl.pallas_call(
        _mem_prompt_kernel,
        grid=(batch, nt),
        in_specs=[
            pl.BlockSpec((tq, w), lambda b, t: (row(b, t), H_MQ // w)),
            pl.BlockSpec((tq, w), lambda b, t: (row(b, t), H_MG // w)),
            pl.BlockSpec((MEM_LEN, w), lambda b, t: (b, 0)),
            pl.BlockSpec((MEM_LEN, w), lambda b, t: (b, 1)),
        ],
        out_specs=pl.BlockSpec((tq, w), lambda b, t: (row(b, t), 0)),
        out_shape=jax.ShapeDtypeStruct((h.shape[0], w), BF16),
        compiler_params=_cparams(("parallel", "parallel")),
        name="mem_prompt",
    )(h, h, kv, kv)


def _log_sigmoid(x):
    return jnp.minimum(x, 0.0) - jnp.log1p(jnp.exp(-jnp.abs(x)))


def _gla_prompt_kernel(q_ref, k_ref, v_ref, g_ref, lr_ref, wg_ref, bg_ref, gn_ref, lc_ref,
                       o_ref, so_ref, st, qs, ks_, bs_, am, qt, kt, qh, eb, ob):
    t = pl.program_id(1)
    tb = q_ref.shape[0]
    ch = GLA_CHUNK
    nc = tb // ch
    ng = tb // SUB
    nsub = ch // SUB
    kw = GLA_H * GLA_DK
    hk = lambda hd: slice(hd * GLA_DK, (hd + 1) * GLA_DK)
    hv = lambda hd: slice(hd * GLA_DV, (hd + 1) * GLA_DV)

    @pl.when(t == 0)
    def _():
        st[...] = jnp.zeros_like(st)

    lane = lax.broadcasted_iota(jnp.int32, (ng, LANES), 1)
    grp = lax.broadcasted_iota(jnp.int32, (ng, LANES), 0)
    rel = lane - SUB * (grp % nsub)
    rr = lax.broadcasted_iota(jnp.int32, (ch, LANES), 0) // SUB
    cc = lax.broadcasted_iota(jnp.int32, (ch, LANES), 1)
    off_mask = jnp.where(cc >= 8 * rr * (rr - 1), jnp.where(cc < 8 * rr * (rr + 1), 1, 0), 0) > 0

    x = jnp.dot(lr_ref[...].astype(BF16), wg_ref[...], preferred_element_type=F32) + bg_ref[...]
    g2 = _log_sigmoid(x) * (LOG2E / GLA_TAU)
    g_hi = g2.astype(BF16)
    r1 = g2 - g_hi.astype(F32)
    g_mid = r1.astype(BF16)
    g_lo = (r1 - g_mid.astype(F32)).astype(BF16)
    cum = jnp.dot(lc_ref[...], jnp.concatenate([g_hi, g_mid, g_lo], axis=1), preferred_element_type=F32)
    cum = cum[:, :kw] + cum[:, kw:2 * kw] + cum[:, 2 * kw:]
    b, bl, bsb = cum[:tb], cum[tb:2 * tb], cum[2 * tb:]
    q = q_ref[...] * (GLA_DK ** -0.5)
    k = k_ref[...]
    for hd in range(GLA_H):
        qs[hd] = q[:, hk(hd)]
        ks_[hd] = k[:, hk(hd)]
        bs_[hd] = b[:, hk(hd)]
    qt[...] = (q * jnp.exp2(b)).astype(BF16)
    kt[...] = (k * jnp.exp2(bl - b)).astype(BF16)
    eb[...] = jnp.exp2(bl)
    qh[...] = (q * jnp.exp2(b - bsb)).astype(BF16)

    def rows_i(ref, i):
        return jnp.concatenate([ref[hd, pl.ds(i, ng, stride=SUB), :] for hd in range(GLA_H)], axis=1)

    qd = [rows_i(qs, i) for i in range(SUB)]
    kd = [rows_i(ks_, i) for i in range(SUB)]
    bd = [rows_i(bs_, i) for i in range(SUB)]
    at_col = [rel == j for j in range(SUB)]
    for i in range(SUB):
        a_i = [jnp.zeros((ng, LANES), F32) for _ in range(GLA_H)]
        for j in range(i + 1):
            tt = qd[i] * kd[j] * jnp.exp2(bd[i] - bd[j])
            for hd in range(GLA_H):
                col = jnp.sum(tt[:, hk(hd)], axis=-1, keepdims=True)
                a_i[hd] = jnp.where(at_col[j], col, a_i[hd])
        for hd in range(GLA_H):
            am[hd, pl.ds(i, ng, stride=SUB), :] = a_i[hd]

    kalls = []
    for cidx in range(nc):
        r0 = cidx * ch
        kc, bc = k[r0:r0 + ch], b[r0:r0 + ch]
        parts = []
        for sub in range(1, nsub):
            s_i = bsb[r0 + sub * SUB:r0 + sub * SUB + 1, :]
            parts.append((kc[:sub * SUB] * jnp.exp2(s_i - bc[:sub * SUB])).astype(BF16))
        parts.append(jnp.zeros((LANES - sum(p.shape[0] for p in parts), kw), BF16))
        kalls.append(jnp.concatenate(parts, axis=0))
    scs = {}
    for cidx in range(nc):
        for hd in range(GLA_H):
            scs[cidx, hd] = lax.dot_general(qh[cidx * ch:(cidx + 1) * ch, hk(hd)], kalls[cidx][:, hk(hd)],
                                            (((1,), (1,)), ((), ())), preferred_element_type=F32)
    for cidx in range(nc):
        rows = slice(cidx * ch, (cidx + 1) * ch)
        for hd in range(GLA_H):
            lhs = jnp.concatenate([jnp.where(off_mask, scs[cidx, hd], 0.0).astype(BF16),
                                   am[hd, rows, :].astype(BF16)], axis=1)
            vc = v_ref[rows, hv(hd)].astype(BF16)
            rhs = jnp.concatenate(
                [vc[:s * SUB] for s in range(1, nsub)]
                + [jnp.zeros((LANES - SUB * nsub * (nsub - 1) // 2, GLA_DV), BF16), vc,
                   jnp.zeros((LANES - ch, GLA_DV), BF16)], axis=0)
            ob[hd, rows, :] = jnp.dot(lhs, rhs, preferred_element_type=F32)

    for cidx in range(nc):
        rows = slice(cidx * ch, (cidx + 1) * ch)
        for hd in range(GLA_H):
            vc = v_ref[rows, hv(hd)].astype(BF16)
            sb = st[hd]
            ob[hd, rows, :] += lax.dot_general(qt[rows, hk(hd)], sb.astype(BF16), (((1,), (1,)), ((), ())),
                                               preferred_element_type=F32)
            upd = lax.dot_general(vc, kt[rows, hk(hd)], (((0,), (0,)), ((), ())), preferred_element_type=F32)
            st[hd] = sb * eb[cidx * ch:cidx * ch + 1, hk(hd)] + upd

    for hd in range(GLA_H):
        vs = slice(hd * GLA_DV, (hd + 1) * GLA_DV)
        o = ob[hd]
        on = o * lax.rsqrt(jnp.mean(o * o, axis=-1, keepdims=True) + EPS) * gn_ref[:, vs]
        gg = g_ref[:, vs]
        o_ref[:, vs] = (on * (gg * _sigmoid(gg))).astype(o_ref.dtype)

    @pl.when(t == pl.num_programs(1) - 1)
    def _():
        for hd in range(GLA_H):
            so_ref[hd] = st[hd].T


def gla_prompt(h, wg, bg, gn, batch, seq, tb):
    nt = seq // tb
    kw = GLA_H * GLA_DK
    vw = GLA_H * GLA_DV
    row = lambda b, t: b * nt + t
    r = np.arange(tb)[:, None]
    c = np.arange(tb)[None, :]
    same = (r // GLA_CHUNK) == (c // GLA_CHUNK)
    lcat = jnp.asarray(np.concatenate([same & (c <= r), same & (c >= 0), same & (c < (r // SUB) * SUB)], axis=0),
                       BF16)
    return pl.pallas_call(
        _gla_prompt_kernel,
        grid=(batch, nt),
        in_specs=[
            pl.BlockSpec((tb, kw), lambda b, t: (row(b, t), H_GQ // kw)),
            pl.BlockSpec((tb, kw), lambda b, t: (row(b, t), H_GK // kw)),
            pl.BlockSpec((tb, vw), lambda b, t: (row(b, t), H_GV // vw)),
            pl.BlockSpec((tb, vw), lambda b, t: (row(b, t), H_GG // vw)),
            pl.BlockSpec((tb, LANES), lambda b, t: (row(b, t), H_GLR // LANES)),
            pl.BlockSpec((LANES, kw), lambda b, t: (0, 0)),
            pl.BlockSpec((1, kw), lambda b, t: (0, 0)),
            pl.BlockSpec((1, vw), lambda b, t: (0, 0)),
            pl.BlockSpec((3 * tb, tb), lambda b, t: (0, 0)),
        ],
        out_specs=[
            pl.BlockSpec((tb, vw), lambda b, t: (row(b, t), 0)),
            pl.BlockSpec((None, GLA_H, GLA_DK, GLA_DV), lambda b, t: (b, 0, 0, 0)),
        ],
        out_shape=[
            jax.ShapeDtypeStruct((h.shape[0], vw), BF16),
            jax.ShapeDtypeStruct((batch, GLA_H, GLA_DK, GLA_DV), F32),
        ],
        scratch_shapes=[
            pltpu.VMEM((GLA_H, GLA_DV, GLA_DK), F32),
            pltpu.VMEM((GLA_H, tb, GLA_DK), F32),
            pltpu.VMEM((GLA_H, tb, GLA_DK), F32),
            pltpu.VMEM((GLA_H, tb, GLA_DK), F32),
            pltpu.VMEM((GLA_H, tb, LANES), F32),
            pltpu.VMEM((tb, kw), BF16),
            pltpu.VMEM((tb, kw), BF16),
            pltpu.VMEM((tb, kw), BF16),
            pltpu.VMEM((tb, kw), F32),
            pltpu.VMEM((GLA_H, tb, GLA_DV), F32),
        ],
        compiler_params=_cparams(("arbitrary", "arbitrary")),
        name="gla_prompt",
    )(h, h, h, h, h, wg, bg, gn, lcat)


def _per_sequence(body):
    def kern(*refs, n_shared):
        shared = refs[:n_shared]
        for s in range(refs[n_shared].shape[0]):
            body(*shared, *[r.at[s] for r in refs[n_shared:]])
    return kern


def _swa_sample_kernel(cos_ref, sin_ref, sink_ref, q_ref, g_ref, k_ref, v_ref, kb_ref, vb_ref,
                       o_ref, ko_ref, vo_ref):
    w = kb_ref.shape[2]
    cos = cos_ref[...]
    sin = sin_ref[...]
    lane1 = lax.broadcasted_iota(jnp.int32, (1, LANES), 1)
    lane = lax.broadcasted_iota(jnp.int32, (16, LANES), 1)
    pos = lax.broadcasted_iota(jnp.int32, (SWA_HD, w), 1)
    eye = lax.broadcasted_iota(jnp.int32, (SWA_HD, SWA_HD), 0) == lax.broadcasted_iota(jnp.int32, (SWA_HD, SWA_HD), 1)
    lo = lane < 64

    def col(row):
        return jnp.sum(jnp.where(eye, jnp.broadcast_to(row, (SWA_HD, SWA_HD)), 0.0), axis=-1, keepdims=True)

    kk, vv = [], []
    for p in range(2):
        sl = slice(p * LANES, (p + 1) * LANES)
        knew = _rope128(k_ref[:, sl], cos, sin, (lane1 & 32) == 0)
        vnew = v_ref[:, sl]
        for half in range(2):
            kh = 2 * p + half
            hs = slice(half * SWA_HD, (half + 1) * SWA_HD)
            kwin = jnp.where(pos == w - 1, col(knew[:, hs]), pltpu.roll(kb_ref[kh], w - 1, 1))
            vwin = jnp.where(pos == w - 1, col(vnew[:, hs]), pltpu.roll(vb_ref[kh], w - 1, 1))
            ko_ref[kh] = kwin
            vo_ref[kh] = vwin
            kk.append(jnp.concatenate([kwin, kwin], axis=0).astype(BF16))
            vv.append(jnp.concatenate([vwin, vwin], axis=0).astype(BF16))

    q = _rope128(q_ref[...], cos, sin, (lane & 32) == 0) * (SWA_HD ** -0.5)
    qq = jnp.concatenate([jnp.where(lo, q, 0.0), jnp.where(lo, 0.0, q)], axis=0)
    grp = (lax.broadcasted_iota(jnp.int32, (32, LANES), 0) % 16) // 4
    s = jnp.zeros((32, w), F32)
    for kh in range(SWA_KV):
        s = s + jnp.dot(jnp.where(grp == kh, qq, 0.0).astype(BF16), kk[kh], preferred_element_type=F32)
    sk = sink_ref[...][:, 0:1]
    m = jnp.maximum(jnp.max(s, axis=-1, keepdims=True), sk)
    p = jnp.exp(s - m)
    den = jnp.sum(p, axis=-1, keepdims=True) + jnp.exp(sk - m)
    o = jnp.zeros((32, LANES), F32)
    for kh in range(SWA_KV):
        o = o + lax.dot_general(jnp.where(grp == kh, p, 0.0).astype(BF16), vv[kh], (((1,), (1,)), ((), ())),
                                preferred_element_type=F32)
    o = o * (1.0 / den)
    g = g_ref[...]
    o_ref[...] = jnp.where(lo, o[0:16], o[16:32]) * (g * _sigmoid(g))


def swa_sample(q, g, k, v, kcache_t, vcache_t, layer, cos, sin, sink_rows):
    _, bd, nkv, hd, wb = kcache_t.shape
    kvw = nkv * hd
    sb = SWA_SAMPLE_BLOCK
    seq3 = lambda r, c: pl.BlockSpec((sb, r, c), lambda b: (b, 0, 0))
    cache_in = pl.BlockSpec((None, sb, nkv, hd, wb), lambda b: (layer, b, 0, 0, 0))
    cache_out = pl.BlockSpec((sb, nkv, hd, wb), lambda b: (b, 0, 0, 0))
    return pl.pallas_call(
        functools.partial(_per_sequence(_swa_sample_kernel), n_shared=3),
        grid=(bd // sb,),
        in_specs=[
            pl.BlockSpec((1, LANES), lambda b: (0, 0)),
            pl.BlockSpec((1, LANES), lambda b: (0, 0)),
            pl.BlockSpec((32, LANES), lambda b: (0, 0)),
            seq3(16, LANES), seq3(16, LANES), seq3(1, kvw), seq3(1, kvw), cache_in, cache_in,
        ],
        out_specs=[seq3(16, LANES), cache_out, cache_out],
        out_shape=[
            jax.ShapeDtypeStruct((bd, 16, LANES), F32),
            jax.ShapeDtypeStruct((bd, nkv, hd, wb), F32),
            jax.ShapeDtypeStruct((bd, nkv, hd, wb), F32),
        ],
        compiler_params=_cparams(("parallel",)),
        name="swa_sample",
    )(cos, sin, sink_rows, q, g, k, v, kcache_t, vcache_t)


def _mem_sample_kernel(q_ref, g_ref, k_ref, v_ref, o_ref):
    n = k_ref.shape[0]
    grp = 2 * MEM_H
    q8 = (q_ref[...] * (MEM_HD ** -0.5)).astype(BF16)
    r = lax.dot_general(q8, k_ref[...].astype(BF16), (((1,), (1,)), ((), ())), preferred_element_type=F32)
    row = lax.broadcasted_iota(jnp.int32, (grp, n), 0)
    col = lax.broadcasted_iota(jnp.int32, (grp, n), 1)
    own = (col % grp) == row
    a = jnp.where(own, r, 0.0)
    s = a + pltpu.roll(pltpu.roll(a, MEM_H, 0), n - MEM_H, 1)
    valid = jnp.where(own, jnp.where(row < MEM_H, 1, 0), 0) > 0
    sm = jnp.where(valid, s, -1e30)
    m = jnp.max(sm, axis=-1, keepdims=True)
    p = jnp.where(valid, jnp.exp(sm - m), 0.0)
    den = jnp.sum(p, axis=-1, keepdims=True)
    pn = p * (1.0 / jnp.where(den > 0.0, den, 1.0))
    p8 = pn + pltpu.roll(pltpu.roll(pn, MEM_H, 0), MEM_H, 1)
    o = jnp.dot(p8.astype(BF16), v_ref[...].astype(BF16), preferred_element_type=F32)
    g = g_ref[...]
    o_ref[...] = o * (g * _sigmoid(g))


def _cache_rows(c):
    depth, bd, mlen, nh, hd = c.shape
    return c.reshape(depth, bd, mlen, nh, hd // LANES, LANES).transpose(0, 1, 2, 4, 3, 5).reshape(
        depth, bd, mlen * nh * (hd // LANES), LANES)


def _half_head_rows(x):
    bd = x.shape[0]
    return x.reshape(bd, MEM_H, MEM_HD // LANES, LANES).transpose(0, 2, 1, 3).reshape(bd, -1, LANES)


def mem_sample(q, g, k, v, layer):
    bd = q.shape[0]
    sb = SAMPLE_BLOCK
    rows = MEM_LEN * MEM_H * (MEM_HD // LANES)
    cache_spec = pl.BlockSpec((None, sb, rows, LANES), lambda b: (layer, b, 0, 0))
    vec_spec = pl.BlockSpec((sb, 2 * MEM_H, LANES), lambda b: (b, 0, 0))
    o = pl.pallas_call(
        functools.partial(_per_sequence(_mem_sample_kernel), n_shared=0),
        grid=(bd // sb,),
        in_specs=[vec_spec, vec_spec, cache_spec, cache_spec],
        out_specs=vec_spec,
        out_shape=jax.ShapeDtypeStruct((bd, 2 * MEM_H, LANES), F32),
        compiler_params=_cparams(("parallel",)),
        name="mem_sample",
    )(_half_head_rows(q), _half_head_rows(g), _cache_rows(k), _cache_rows(v))
    return o.reshape(bd, MEM_HD // LANES, MEM_H, LANES).transpose(0, 2, 1, 3).reshape(bd, MEM_H * MEM_HD)


def _gla_sample_kernel(wg_ref, bg_ref, gn_ref, q_ref, k_ref, v_ref, g_ref, lr_ref, s_ref, o_ref, so_ref):
    rowk = lax.broadcasted_iota(jnp.int32, (8, GLA_DK), 0)
    lr = jnp.broadcast_to(lr_ref[...], (8, LANES)).astype(BF16)
    xa = jnp.dot(lr, wg_ref[...], preferred_element_type=F32) + bg_ref[...]
    x = jnp.zeros((8, GLA_DK), F32)
    for hd in range(GLA_H):
        x = jnp.where(rowk == hd, xa[:, hd * GLA_DK:(hd + 1) * GLA_DK], x)
    gl = _log_sigmoid(x) * (1.0 / GLA_TAU)
    eg = jnp.exp(gl)
    q = q_ref[...] * (GLA_DK ** -0.5)
    k = k_ref[...]
    v = v_ref[...]
    qe = q * eg
    eye = lax.broadcasted_iota(jnp.int32, (GLA_DK, GLA_DK), 0) == lax.broadcasted_iota(jnp.int32, (GLA_DK, GLA_DK), 1)

    def col(row):
        return jnp.sum(jnp.where(eye, jnp.broadcast_to(row, (GLA_DK, GLA_DK)), 0.0), axis=-1, keepdims=True)

    o = jnp.sum(q * k, axis=-1, keepdims=True) * v
    for hd in range(GLA_H):
        s0 = s_ref[hd]
        o = o + jnp.dot(jnp.where(rowk == hd, qe, 0.0).astype(BF16), s0.astype(BF16), preferred_element_type=F32)
        so_ref[hd] = s0 * col(eg[hd:hd + 1, :]) + col(k[hd:hd + 1, :]) * v[hd:hd + 1, :]
    on = o * lax.rsqrt(jnp.mean(o * o, axis=-1, keepdims=True) + EPS) * gn_ref[...]
    gg = g_ref[...]
    o_ref[...] = on * (gg * _sigmoid(gg))


def gla_sample(q, k, v, g, lr, wg, bg, gn, state, layer):
    bd = q.shape[0]
    sb = SAMPLE_BLOCK
    kw = GLA_H * GLA_DK
    seq3 = lambda r, c: pl.BlockSpec((sb, r, c), lambda b: (b, 0, 0))
    return pl.pallas_call(
        functools.partial(_per_sequence(_gla_sample_kernel), n_shared=3),
        grid=(bd // sb,),
        in_specs=[
            pl.BlockSpec((LANES, kw), lambda b: (0, 0)),
            pl.BlockSpec((1, kw), lambda b: (0, 0)),
            pl.BlockSpec((8, GLA_DV), lambda b: (0, 0)),
            seq3(8, GLA_DK), seq3(8, GLA_DK), seq3(8, GLA_DV), seq3(8, GLA_DV), seq3(1, LANES),
            pl.BlockSpec((None, sb, GLA_H, GLA_DK, GLA_DV), lambda b: (layer, b, 0, 0, 0)),
        ],
        out_specs=[
            seq3(8, GLA_DV),
            pl.BlockSpec((sb, GLA_H, GLA_DK, GLA_DV), lambda b: (b, 0, 0, 0)),
        ],
        out_shape=[
            jax.ShapeDtypeStruct((bd, 8, GLA_DV), F32),
            jax.ShapeDtypeStruct(state.shape[1:], F32),
        ],
        compiler_params=_cparams(("parallel",)),
        name="gla_sample",
    )(wg, bg, gn, q, k, v, g, lr, state)


def _heads8(x, width):
    bd = x.shape[0]
    x = x.reshape(bd, -1, width)
    return jnp.pad(x, ((0, 0), (0, 8 - x.shape[1]), (0, 0)))


def kernel(x_prompt, mem_prompt, x_sample, cache_swa_k, cache_swa_v, state_gla, cache_mem_k, cache_mem_v,
           norm_g, w_in, attn_sinks, gla_w_gate, gla_b_gate, gla_norm_g, mem_norm_g, w_mem_kv, w_out,
           final_norm_g):
    batch, seq, d = x_prompt.shape
    bd = x_sample.shape[0]
    m = batch * seq
    kvw = SWA_KV * SWA_HD
    memw = MEM_H * MEM_HD

    cos_p, sin_p = _rope_tables(jnp.arange(seq))
    cos_s, sin_s = _rope_tables(PAST_LEN + jnp.arange(1))
    perm = np.array([2 * (r % 16) + r // 16 for r in range(32)])
    w_in_t = jnp.transpose(w_in, (0, 2, 1))
    swa_k_t = jnp.transpose(cache_swa_k, (0, 1, 3, 4, 2))
    swa_v_t = jnp.transpose(cache_swa_v, (0, 1, 3, 4, 2))

    xp = x_prompt.reshape(m, d)
    xs = x_sample.reshape(bd, d)
    memx = mem_prompt.reshape(batch * MEM_LEN, d)

    mem_n = rmsnorm_multi(memx, mem_norm_g, BF16, 256)

    outs = {k: [] for k in ("kp", "vp", "sp", "mkp", "mvp", "ks", "vs", "ss")}
    for l in range(DEPTH):
        wg = jnp.pad(gla_w_gate[l], ((0, LANES - GLA_RANK), (0, 0))).astype(BF16)
        bg = gla_b_gate[l].reshape(1, -1)
        gn = gla_norm_g[l].reshape(1, -1)

        kv = project(mem_n[l], w_mem_kv, l, name="mem_kv")
        xn = rmsnorm(xp, norm_g[l], BF16, 512)
        xns = rmsnorm(xs, norm_g[l], BF16, bd)
        h, hs = project(xn, w_in_t, l, n=H_END, tile_rows=IN_TILE_SRC, a_small=xns, name="in_proj")

        o_swa_p, kb, vb = swa_prompt(h, attn_sinks[l], cos_p, sin_p, batch, seq)
        o_gla_p, sp = gla_prompt(h, wg, bg, gn, batch, seq, 256)
        o_mem_p = mem_attn_prompt(h, kv, batch, seq, 512)
        outs["kp"].append(kb.reshape(batch, WINDOW, SWA_KV, SWA_HD))
        outs["vp"].append(vb.reshape(batch, WINDOW, SWA_KV, SWA_HD))
        outs["sp"].append(sp)
        outs["mkp"].append(kv[:, :memw].reshape(batch, MEM_LEN, MEM_H, MEM_HD))
        outs["mvp"].append(kv[:, memw:].reshape(batch, MEM_LEN, MEM_H, MEM_HD))

        o_swa, kbs, vbs = swa_sample(
            hs[:, H_SQ:H_SQ + 2048].reshape(bd, 16, LANES), hs[:, H_SG:H_SG + 2048].reshape(bd, 16, LANES),
            hs[:, H_SK:H_SK + kvw].reshape(bd, 1, kvw), hs[:, H_SV:H_SV + kvw].reshape(bd, 1, kvw),
            swa_k_t, swa_v_t, l, cos_s, sin_s, jnp.broadcast_to(attn_sinks[l][perm][:, None], (32, LANES)))
        o_gla, ss = gla_sample(
            _heads8(hs[:, H_GQ:H_GQ + 512], GLA_DK), _heads8(hs[:, H_GK:H_GK + 512], GLA_DK),
            _heads8(hs[:, H_GV:H_GV + 1024], GLA_DV), _heads8(hs[:, H_GG:H_GG + 1024], GLA_DV),
            hs[:, H_GLR:H_GLR + LANES].reshape(bd, 1, LANES), wg, bg,
            jnp.pad(gla_norm_g[l].reshape(GLA_H, GLA_DV), ((0, 4), (0, 0))), state_gla, l)
        o_mem = mem_sample(hs[:, H_MQ:H_MQ + memw], hs[:, H_MG:H_MG + memw], cache_mem_k, cache_mem_v, l)
        mix_s = jnp.concatenate([o_swa.reshape(bd, 2048), o_gla[:, :GLA_H].reshape(bd, 1024), o_mem],
                                axis=1).astype(BF16)
        outs["ks"].append(kbs.transpose(0, 3, 1, 2))
        outs["vs"].append(vbs.transpose(0, 3, 1, 2))
        outs["ss"].append(ss)

        xp, xs = project([o_swa_p, o_gla_p, o_mem_p], w_out, l, a_small=mix_s, res=xp, res_small=xs,
                         name="out_proj")

    y_prompt = rmsnorm(xp, final_norm_g, F32, 512).reshape(batch, seq, d)
    y_sample = rmsnorm(xs, final_norm_g, F32, bd).reshape(bd, 1, d)
    st = lambda k: jnp.stack(outs[k])
    return (y_prompt, y_sample, st("kp"), st("vp"), st("sp"), st("mkp"), st("mvp"), st("ks"), st("vs"), st("ss"))
```
